```python
import numpy as np
import jax, jax.numpy as jnp
from jax import lax

D_MODEL = 1024
BATCH = 8
SEQ = 2048
DEPTH = 2

N_MIXERS = 4
MIX_W = D_MODEL // N_MIXERS
HEAD_DIM = 64
N_Q_HEADS = MIX_W // HEAD_DIM
N_KV_HEADS = 2
Q_PER_KV = N_Q_HEADS // N_KV_HEADS
KV_W = N_KV_HEADS * HEAD_DIM
GMLP_GROUPS = 4
GMLP_CHUNK = 128
CMP_LEN = 32
CMP_STRIDE = 16
CMP_HIDDEN = 128
SEL_BLOCK = 64
SEL_TOPK = 8
N_LOCAL_BLOCKS = 2
WINDOW = 512
Q_BLOCK = 128
CONF_KERNEL = 31
SCONV_KERNEL = 3
D_FF = 4 * D_MODEL
NSA_GATE_W = N_Q_HEADS * 3
IN_WIDTHS = (MIX_W, MIX_W, MIX_W, KV_W, KV_W, KV_W, KV_W, KV_W, KV_W, NSA_GATE_W, MIX_W, MIX_W, MIX_W, MIX_W, MIX_W)
IN_COLS = sum(IN_WIDTHS)
NEG_INF = -1e30

kernel_name = "hybrid_gated_nsa_gmlp_conv_block"


def _rms(x, g, eps=1e-6):
    xf = x.astype(jnp.float32)
    y = xf * lax.rsqrt(jnp.mean(xf * xf, axis=-1, keepdims=True) + eps)
    return (y * g.astype(jnp.float32)).astype(x.dtype)


def _layernorm(x, g, b, eps=1e-5):
    xf = x.astype(jnp.float32)
    mu = jnp.mean(xf, axis=-1, keepdims=True)
    var = jnp.mean(jnp.square(xf - mu), axis=-1, keepdims=True)
    return ((xf - mu) * lax.rsqrt(var + eps) * g.astype(jnp.float32) + b.astype(jnp.float32)).astype(x.dtype)


def _causal_depthwise_conv(x, w):
    k = w.shape[0]
    return lax.conv_general_dilated(x, w[:, None, :].astype(x.dtype), window_strides=(1,),
                                    padding=[(k - 1, 0)], dimension_numbers=('NWC', 'WIO', 'NWC'),
                                    feature_group_count=x.shape[-1])


def _gmlp_spatial_gate(u, v, ln_g, ln_b, ws, bs):
    B, S, _ = v.shape
    nc = S // GMLP_CHUNK
    v = _layernorm(v, ln_g, ln_b).reshape(B, nc, GMLP_CHUNK, GMLP_GROUPS, MIX_W // GMLP_GROUPS)
    mask = np.tril(np.ones((GMLP_CHUNK, GMLP_CHUNK), dtype=bool))
    wsm = jnp.where(mask, ws, 0).astype(v.dtype)
    mixed = jnp.einsum('gts,bnsgc->bntgc', wsm, v) + bs.T[None, None, :, :, None].astype(v.dtype)
    return u * mixed.reshape(B, S, MIX_W)


def _overlap_matrix(n_cmp, n_sel):
    cs = np.arange(n_cmp) * CMP_STRIDE
    ce = cs + CMP_LEN
    ss = np.arange(n_sel) * SEL_BLOCK
    se = ss + SEL_BLOCK
    return ((cs[:, None] < se[None, :]) & (ce[:, None] > ss[None, :])).astype(np.float32)


def _nsa(q, kc, vc, ks, vs, kw, vw, gate_logits, q_g, k_g, cmp_pe, cmp_w1, cmp_w2):
    B, S, _ = q.shape
    T = Q_BLOCK
    nqb = S // T
    pos = jnp.arange(S)
    scale = HEAD_DIM ** -0.5
    heads = lambda t: t.reshape(B, S, N_KV_HEADS, HEAD_DIM)
    q = _rms(q.reshape(B, S, N_KV_HEADS, Q_PER_KV, HEAD_DIM), q_g)

    n_cmp = (S - CMP_LEN) // CMP_STRIDE + 1
    cmp_idx = np.arange(n_cmp)[:, None] * CMP_STRIDE + np.arange(CMP_LEN)[None, :]

    def compress(t, pe, w1, w2):
        blk = jnp.take(heads(t), cmp_idx, axis=1) + pe[None, None, :, None, :]
        blk = jnp.swapaxes(blk, 2, 3).reshape(B, n_cmp, N_KV_HEADS, CMP_LEN * HEAD_DIM)
        hid = jax.nn.gelu(jnp.einsum('bnhf,fe->bnhe', blk, w1))
        return jnp.einsum('bnhe,ed->bnhd', hid, w2)

    k_cmp = _rms(compress(kc, cmp_pe[0], cmp_w1[0], cmp_w2[0]), k_g[0])
    v_cmp = compress(vc, cmp_pe[1], cmp_w1[1], cmp_w2[1])
    cmp_valid = cmp_idx[:, -1][None, :] <= np.arange(S)[:, None]
    s = jnp.einsum('bshgd,bnhd->bhgsn', q, k_cmp).astype(jnp.float32) * scale
    p_cmp = jnp.where(cmp_valid, jax.nn.softmax(jnp.where(cmp_valid, s, NEG_INF), axis=-1), 0.0)
    o_cmp = jnp.einsum('bhgsn,bnhd->bshgd', p_cmp.astype(v_cmp.dtype), v_cmp)

    n_sel = S // SEL_BLOCK
    k_top = min(SEL_TOPK, n_sel)
    p_slc = jnp.einsum('bhgsn,nj->bhsj', p_cmp, jnp.asarray(_overlap_matrix(n_cmp, n_sel)))
    blk = np.arange(n_sel)[None, :]
    cur = (np.arange(S) // SEL_BLOCK)[:, None]
    forced = (blk == 0) | ((cur - blk >= 0) & (cur - blk < N_LOCAL_BLOCKS))
    causal = blk <= cur
    score = jnp.where(forced, jnp.inf, jnp.where(causal, p_slc, -jnp.inf))
    sel_idx = lax.top_k(score, k_top)[1]

    ks_h = jnp.swapaxes(_rms(heads(ks), k_g[1]), 1, 2)
    vs_h = jnp.swapaxes(heads(vs), 1, 2)
    gather = jax.vmap(jax.vmap(lambda a, i: a[i]))
    tok_off = jnp.arange(SEL_BLOCK)

    def sel_block(args):
        qb, ib, pb = args
        tok = (ib[..., None] * SEL_BLOCK + tok_off).reshape(B, N_KV_HEADS, T, k_top * SEL_BLOCK)
        flat = tok.reshape(B, N_KV_HEADS, -1)
        kg = gather(ks_h, flat).reshape(B, N_KV_HEADS, T, k_top * SEL_BLOCK, HEAD_DIM)
        vg = gather(vs_h, flat).reshape(B, N_KV_HEADS, T, k_top * SEL_BLOCK, HEAD_DIM)
        sb = jnp.einsum('bthgd,bhtkd->bhgtk', qb, kg).astype(jnp.float32) * scale
        m = (tok <= pb[:, None])[:, :, None]
        pr = jax.nn.softmax(jnp.where(m, sb, NEG_INF), axis=-1).astype(vg.dtype)
        return jnp.einsum('bhgtk,bhtkd->bthgd', pr, vg)

    q_blocks = jnp.moveaxis(q.reshape(B, nqb, T, N_KV_HEADS, Q_PER_KV, HEAD_DIM), 1, 0)
    i_blocks = jnp.moveaxis(sel_idx.reshape(B, N_KV_HEADS, nqb, T, k_top), 2, 0)
    p_blocks = pos.reshape(nqb, T)
    o_sel = jnp.moveaxis(lax.map(sel_block, (q_blocks, i_blocks, p_blocks)), 0, 1)
    o_sel = o_sel.reshape(B, S, N_KV_HEADS, Q_PER_KV, HEAD_DIM)

    kw_h = _rms(heads(kw), k_g[2])
    vw_h = heads(vw)
    win_idx = np.arange(nqb)[:, None] * T + np.arange(WINDOW + T)[None, :]
    kpos = (win_idx - WINDOW)[:, None, :]
    qpos = np.arange(S).reshape(nqb, T)[:, :, None]
    wmask = (kpos <= qpos) & (kpos > qpos - WINDOW) & (kpos >= 0)
    pad = ((0, 0), (WINDOW, 0), (0, 0), (0, 0))
    kb = jnp.take(jnp.pad(kw_h, pad), win_idx, axis=1)
    vb = jnp.take(jnp.pad(vw_h, pad), win_idx, axis=1)
    qb = q.reshape(B, nqb, T, N_KV_HEADS, Q_PER_KV, HEAD_DIM)
    sw = jnp.einsum('bnthgd,bnmhd->bnhgtm', qb, kb).astype(jnp.float32) * scale
    pw = jax.nn.softmax(jnp.where(wmask[:, None, None], sw, NEG_INF), axis=-1).astype(vb.dtype)
    o_win = jnp.einsum('bnhgtm,bnmhd->bnthgd', pw, vb).reshape(B, S, N_KV_HEADS, Q_PER_KV, HEAD_DIM)

    g = jax.nn.sigmoid(gate_logits.reshape(B, S, N_KV_HEADS, Q_PER_KV, 3))
    o = g[..., 0:1] * o_cmp + g[..., 1:2] * o_sel + g[..., 2:3] * o_win
    return o.reshape(B, S, MIX_W)


def setup_inputs(seed: int = 0) -> dict:
    key = jax.random.key(seed)
    ks = jax.random.split(key, 26)
    L = DEPTH
    nrm = lambda k, shape, sc: jax.random.normal(k, shape, jnp.float32) * sc
    res_sc = (2.0 * DEPTH) ** -0.5
    return {
        "x": nrm(ks[0], (BATCH, SEQ, D_MODEL), 1.0),
        "norm1_g": 1.0 + nrm(ks[1], (L, D_MODEL), 0.02),
        "w_in": nrm(ks[2], (L, D_MODEL, IN_COLS), D_MODEL ** -0.5),
        "gmlp_ln_g": 1.0 + nrm(ks[3], (L, MIX_W), 0.02),
        "gmlp_ln_b": nrm(ks[4], (L, MIX_W), 0.02),
        "gmlp_ws": nrm(ks[5], (L, GMLP_GROUPS, GMLP_CHUNK, GMLP_CHUNK), GMLP_CHUNK ** -0.5),
        "gmlp_bs": 1.0 + nrm(ks[6], (L, GMLP_GROUPS, GMLP_CHUNK), 0.02),
        "nsa_q_norm_g": 1.0 + nrm(ks[7], (L, HEAD_DIM), 0.02),
        "nsa_k_norm_g": 1.0 + nrm(ks[8], (L, 3, HEAD_DIM), 0.02),
        "nsa_cmp_pe": nrm(ks[9], (L, 2, CMP_LEN, HEAD_DIM), 0.02),
        "nsa_cmp_w1": nrm(ks[10], (L, 2, CMP_LEN * HEAD_DIM, CMP_HIDDEN), (CMP_LEN * HEAD_DIM) ** -0.5),
        "nsa_cmp_w2": nrm(ks[11], (L, 2, CMP_HIDDEN, HEAD_DIM), CMP_HIDDEN ** -0.5),
        "conf_conv_w": nrm(ks[12], (L, CONF_KERNEL, MIX_W), CONF_KERNEL ** -0.5),
        "conf_conv_b": nrm(ks[13], (L, MIX_W), 0.02),
        "conf_ln_g": 1.0 + nrm(ks[14], (L, MIX_W), 0.02),
        "conf_ln_b": nrm(ks[15], (L, MIX_W), 0.02),
        "sconv_w": nrm(ks[16], (L, SCONV_KERNEL, MIX_W), SCONV_KERNEL ** -0.5),
        "w_branch": nrm(ks[17], (L, N_MIXERS, MIX_W, D_MODEL), MIX_W ** -0.5),
        "w_gate": nrm(ks[18], (L, D_MODEL, N_MIXERS * D_MODEL), D_MODEL ** -0.5),
        "b_gate": nrm(ks[19], (L, N_MIXERS * D_MODEL), 0.02),
        "w_out": nrm(ks[20], (L, D_MODEL, D_MODEL), D_MODEL ** -0.5 * res_sc),
        "norm2_g": 1.0 + nrm(ks[21], (L, D_MODEL), 0.02),
        "w_mlp1": nrm(ks[22], (L, D_MODEL, D_FF), D_MODEL ** -0.5),
        "w_mlp2": nrm(ks[23], (L, D_FF, D_MODEL), D_FF ** -0.5 * res_sc),
    }


def reference(x, norm1_g, w_in, gmlp_ln_g, gmlp_ln_b, gmlp_ws, gmlp_bs, nsa_q_norm_g, nsa_k_norm_g,
              nsa_cmp_pe, nsa_cmp_w1, nsa_cmp_w2, conf_conv_w, conf_conv_b, conf_ln_g, conf_ln_b,
              sconv_w, w_branch, w_gate, b_gate, w_out, norm2_g, w_mlp1, w_mlp2):
    B, S, D = x.shape
    split_points = np.cumsum(IN_WIDTHS)[:-1].tolist()
    for l in range(DEPTH):
        xn = _rms(x, norm1_g[l])
        (gu, gv, q, kc, vc, ks_, vs_, kw, vw, ng, ca, cb, sB, sC, sh) = jnp.split(
            jnp.einsum('bsd,de->bse', xn, w_in[l]), split_points, axis=-1)
        y_a = _gmlp_spatial_gate(jax.nn.gelu(gu), jax.nn.gelu(gv), gmlp_ln_g[l], gmlp_ln_b[l], gmlp_ws[l], gmlp_bs[l])
        y_b = _nsa(q, kc, vc, ks_, vs_, kw, vw, ng, nsa_q_norm_g[l], nsa_k_norm_g[l],
                   nsa_cmp_pe[l], nsa_cmp_w1[l], nsa_cmp_w2[l])
        z = ca * jax.nn.sigmoid(cb)
        z = _causal_depthwise_conv(z, conf_conv_w[l]) + conf_conv_b[l]
        y_c = jax.nn.silu(_layernorm(z, conf_ln_g[l], conf_ln_b[l]))
        y_d = sB * _causal_depthwise_conv(sC * sh, sconv_w[l])
        ys = jnp.stack([y_a, y_b, y_c, y_d], axis=2)
        proj = jnp.einsum('bsnc,ncd->bsnd', ys, w_branch[l])
        gates = jax.nn.sigmoid(jnp.einsum('bsd,de->bse', xn, w_gate[l]) + b_gate[l]).reshape(B, S, N_MIXERS, D)
        mixed = jnp.sum(gates * proj, axis=2)
        x = x + jnp.einsum('bsd,de->bse', mixed, w_out[l])
        hn = _rms(x, norm2_g[l])
        hid = jnp.square(jax.nn.relu(jnp.einsum('bsd,df->bsf', hn, w_mlp1[l])))
        x = x + jnp.einsum('bsf,fd->bsd', hid, w_mlp2[l])
    return x
```

```python
import functools

import numpy as np
import jax
import jax.numpy as jnp
from jax import lax
from jax.experimental import pallas as pl
from jax.experimental.pallas import tpu as pltpu

D_MODEL = 1024
MIX_W = 256
HEAD_DIM = 64
N_KV_HEADS = 2
Q_PER_KV = 2
KV_W = N_KV_HEADS * HEAD_DIM
GMLP_GROUPS = 4
GMLP_CHUNK = 128
CMP_LEN = 32
CMP_STRIDE = 16
CMP_HIDDEN = 128
SEL_BLOCK = 64
SEL_TOPK = 8
N_LOCAL_BLOCKS = 2
WINDOW = 512
CONF_KERNEL = 31
SCONV_KERNEL = 3
D_FF = 4 * D_MODEL
NSA_GATE_W = 12
NEG_INF = -1e30

LANES = 128
MXU_DTYPE = jnp.bfloat16
VMEM_LIMIT = 56 * 1024 * 1024

_C_GU, _C_GV, _C_Q = 0, 256, 512
_C_KC, _C_VC, _C_KS, _C_VS, _C_KW, _C_VW = 768, 896, 1024, 1152, 1280, 1408
_C_NG = 1536
_C_CA, _C_CB, _C_SB, _C_SC, _C_SH = 1664, 1920, 2176, 2432, 2688
IN_COLS_PAD = 2944
_NG_SRC = 1536

TM_PROJ = 256
TM_MERGE = 256
TM_MLP = 256
T_ATT = 256
CH_ATT = 128
CONV_HALO = 32
SCONV_HALO = 8


def _f32(x):
    return x.astype(jnp.float32)


def _rms_rows(x, g, eps=1e-6):
    return x * lax.rsqrt(jnp.mean(x * x, axis=-1, keepdims=True) + eps) * g


def _layernorm_rows(x, g, b, eps=1e-5):
    mu = jnp.mean(x, axis=-1, keepdims=True)
    xc = x - mu
    var = jnp.mean(xc * xc, axis=-1, keepdims=True)
    return xc * lax.rsqrt(var + eps) * g + b


def _head_rms_rows(x, g, eps=1e-6):
    n_heads = x.shape[-1] // HEAD_DIM
    head = lax.broadcasted_iota(jnp.int32, x.shape, 1) >> 6
    xx = x * x
    scale = jnp.zeros_like(x)
    for h in range(n_heads):
        ms = jnp.sum(jnp.where(head == h, xx, 0.0), axis=-1, keepdims=True) * (1.0 / HEAD_DIM)
        scale = jnp.where(head == h, lax.rsqrt(ms + eps), scale)
    return x * scale * g


def _proj_kernel(x_ref, g1_ref, w_ref, lng_ref, lnb_ref, ws_ref, bst_ref, qg_ref, kgs_ref, kgw_ref,
                 cw_ref, cbias_ref, clg_ref, clb_ref, sw_ref,
                 yacd_ref, qt_ref, kc_ref, vc_ref, ks_ref, vst_ref, kw_ref, vwt_ref, ngt_ref,
                 zbuf, cbuf, *, tm, ch):
    i = pl.program_id(1)
    x = x_ref[0]
    xn = _rms_rows(x, g1_ref[...]).astype(MXU_DTYPE)

    def proj(lo, width):
        return jnp.dot(xn, w_ref[:, lo:lo + width], preferred_element_type=jnp.float32)

    u = jax.nn.gelu(proj(_C_GU, MIX_W))
    v = jax.nn.gelu(proj(_C_GV, MIX_W))
    v = _layernorm_rows(v, lng_ref[...], lnb_ref[...]).astype(MXU_DTYPE)
    tri = (lax.broadcasted_iota(jnp.int32, (GMLP_CHUNK, GMLP_CHUNK), 0)
           >= lax.broadcasted_iota(jnp.int32, (GMLP_CHUNK, GMLP_CHUNK), 1))
    grp = lax.broadcasted_iota(jnp.int32, (GMLP_CHUNK, MIX_W), 1) >> 6
    wsm = [jnp.where(tri, ws_ref[g], 0.0).astype(MXU_DTYPE) for g in range(GMLP_GROUPS)]
    for c in range(tm // GMLP_CHUNK):
        rows = slice(c * GMLP_CHUNK, (c + 1) * GMLP_CHUNK)
        vch = v[rows]
        mixed = jnp.zeros((GMLP_CHUNK, MIX_W), jnp.float32)
        for g in range(GMLP_GROUPS):
            r = jnp.dot(wsm[g], vch, preferred_element_type=jnp.float32)
            mixed = jnp.where(grp == g, r, mixed)
        yacd_ref[0, rows, 0:MIX_W] = (u[rows] * (mixed + bst_ref[...])).astype(yacd_ref.dtype)

    z = proj(_C_CA, MIX_W) * jax.nn.sigmoid(proj(_C_CB, MIX_W))

    @pl.when(i == 0)
    def _():
        zbuf[0:CONV_HALO, :] = jnp.zeros((CONV_HALO, MIX_W), jnp.float32)
        cbuf[0:SCONV_HALO, :] = jnp.zeros((SCONV_HALO, MIX_W), jnp.float32)

    zbuf[CONV_HALO:CONV_HALO + tm, :] = z
    acc = jnp.zeros((tm, MIX_W), jnp.float32)
    for j in range(CONF_KERNEL):
        acc = acc + cw_ref[j:j + 1, :] * zbuf[pl.ds(CONV_HALO - (CONF_KERNEL - 1) + j, tm), :]
    acc = acc + cbias_ref[...]
    yacd_ref[0, :, MIX_W:2 * MIX_W] = jax.nn.silu(
        _layernorm_rows(acc, clg_ref[...], clb_ref[...])).astype(yacd_ref.dtype)
    zbuf[0:CONV_HALO, :] = zbuf[tm:tm + CONV_HALO, :]

    sb = proj(_C_SB, MIX_W)
    cbuf[SCONV_HALO:SCONV_HALO + tm, :] = proj(_C_SC, MIX_W) * proj(_C_SH, MIX_W)
    acc = jnp.zeros((tm, MIX_W), jnp.float32)
    for j in range(SCONV_KERNEL):
        acc = acc + sw_ref[j:j + 1, :] * cbuf[pl.ds(SCONV_HALO - (SCONV_KERNEL - 1) + j, tm), :]
    yacd_ref[0, :, 2 * MIX_W:3 * MIX_W] = (sb * acc).astype(yacd_ref.dtype)
    cbuf[0:SCONV_HALO, :] = cbuf[tm:tm + SCONV_HALO, :]

    qt = proj(_C_Q, MIX_W).T
    gq = jnp.concatenate([qg_ref[...]] * (tm // LANES), axis=1)
    pieces = []
    for h in range(MIX_W // HEAD_DIM):
        blk = qt[h * HEAD_DIM:(h + 1) * HEAD_DIM]
        ms = jnp.sum(blk * blk, axis=0, keepdims=True) * (1.0 / HEAD_DIM)
        pieces.append(blk * lax.rsqrt(ms + 1e-6))
    qt_ref[0] = (jnp.concatenate(pieces, axis=0) * gq * (HEAD_DIM ** -0.5)).astype(qt_ref.dtype)

    kc_ref[0] = proj(_C_KC, KV_W)
    vc_ref[0] = proj(_C_VC, KV_W)
    ks_ref[0] = _head_rms_rows(proj(_C_KS, KV_W), kgs_ref[...]).astype(ks_ref.dtype)
    kw_ref[0] = _head_rms_rows(proj(_C_KW, KV_W), kgw_ref[...]).astype(kw_ref.dtype)
    vst = proj(_C_VS, KV_W).T
    vwt = proj(_C_VW, KV_W).T
    for c in range(tm // ch):
        vst_ref[0, c] = vst[:, c * ch:(c + 1) * ch].astype(vst_ref.dtype)
        vwt_ref[0, c] = vwt[:, c * ch:(c + 1) * ch].astype(vwt_ref.dtype)
    ngt_ref[0] = proj(_C_NG, LANES).T[0:16, :]


def _proj_call(x, g1, w, lng, lnb, ws, bst, qg, kgs, kgw, cw, cbias, clg, clb, sw):
    B, S, D = x.shape
    tm, ch = min(TM_PROJ, S), CH_ATT
    full = lambda a: pl.BlockSpec(a.shape, lambda b, i: (0,) * a.ndim)
    seq = lambda width: pl.BlockSpec((1, tm, width), lambda b, i: (b, i, 0))
    ins = (g1, w, lng, lnb, ws, bst, qg, kgs, kgw, cw, cbias, clg, clb, sw)
    out_shape = (
        jax.ShapeDtypeStruct((B, S, 3 * MIX_W), MXU_DTYPE),
        jax.ShapeDtypeStruct((B, MIX_W, S), MXU_DTYPE),
        jax.ShapeDtypeStruct((B, S, KV_W), jnp.float32),
        jax.ShapeDtypeStruct((B, S, KV_W), jnp.float32),
        jax.ShapeDtypeStruct((B, S, KV_W), MXU_DTYPE),
        jax.ShapeDtypeStruct((B, S // ch, KV_W, ch), MXU_DTYPE),
        jax.ShapeDtypeStruct((B, S, KV_W), MXU_DTYPE),
        jax.ShapeDtypeStruct((B, S // ch, KV_W, ch), MXU_DTYPE),
        jax.ShapeDtypeStruct((B, 16, S), jnp.float32),
    )
    vt_spec = pl.BlockSpec((1, tm // ch, KV_W, ch), lambda b, i: (b, i, 0, 0))
    out_specs = (
        seq(3 * MIX_W),
        pl.BlockSpec((1, MIX_W, tm), lambda b, i: (b, 0, i)),
        seq(KV_W), seq(KV_W), seq(KV_W), vt_spec, seq(KV_W), vt_spec,
        pl.BlockSpec((1, 16, tm), lambda b, i: (b, 0, i)),
    )
    return pl.pallas_call(
        functools.partial(_proj_kernel, tm=tm, ch=ch),
        grid=(B, S // tm),
        in_specs=[seq(D)] + [full(a) for a in ins],
        out_specs=out_specs,
        out_shape=out_shape,
        scratch_shapes=[pltpu.VMEM((CONV_HALO + tm, MIX_W), jnp.float32),
                        pltpu.VMEM((SCONV_HALO + tm, MIX_W), jnp.float32)],
        compiler_params=pltpu.CompilerParams(
            dimension_semantics=("arbitrary", "arbitrary"), vmem_limit_bytes=VMEM_LIMIT),
        name="proj_mixers",
    )(x, *ins)


def _compress_kernel(kc_ref, vc_ref, pet_ref, peb_ref, wt_ref, wb_ref, w2_ref, kg_ref,
                     kcmp_ref, vcmpt_ref):
    nc = kc_ref.shape[1]

    def compress(src_ref, idx):
        chunks = src_ref[0]
        a1 = jnp.dot((chunks + pet_ref[idx]).astype(MXU_DTYPE), wt_ref[idx],
                     preferred_element_type=jnp.float32)
        a2 = jnp.dot((chunks + peb_ref[idx]).astype(MXU_DTYPE), wb_ref[idx],
                     preferred_element_type=jnp.float32)
        hid = jax.nn.gelu(a1 + pltpu.roll(a2, shift=nc - 1, axis=0))
        return jnp.dot(hid.astype(MXU_DTYPE), w2_ref[idx], preferred_element_type=jnp.float32)

    kcmp_ref[0] = _head_rms_rows(compress(kc_ref, 0), kg_ref[...]).astype(kcmp_ref.dtype)
    vcmpt_ref[0] = compress(vc_ref, 1).T.astype(vcmpt_ref.dtype)


def _compress_call(kc, vc, pet, peb, wt, wb, w2, kg):
    B, S, _ = kc.shape
    nc = S // CMP_STRIDE
    kc = kc.reshape(B, nc, CMP_STRIDE * KV_W)
    vc = vc.reshape(B, nc, CMP_STRIDE * KV_W)
    full = lambda a: pl.BlockSpec(a.shape, lambda b: (0,) * a.ndim)
    per_b = lambda r, c: pl.BlockSpec((1, r, c), lambda b: (b, 0, 0))
    ins = (pet, peb, wt, wb, w2, kg)
    return pl.pallas_call(
        _compress_kernel,
        grid=(B,),
        in_specs=[per_b(nc, CMP_STRIDE * KV_W)] * 2 + [full(a) for a in ins],
        out_specs=(per_b(nc, KV_W), per_b(KV_W, nc)),
        out_shape=(jax.ShapeDtypeStruct((B, nc, KV_W), MXU_DTYPE),
                   jax.ShapeDtypeStruct((B, KV_W, nc), MXU_DTYPE)),
        compiler_params=pltpu.CompilerParams(
            dimension_semantics=("arbitrary",), vmem_limit_bytes=VMEM_LIMIT),
        name="nsa_compress",
    )(kc, vc, *ins)


def _attn_kernel(qt_ref, ngt_ref, kcmp_ref, vcmpt_ref, ks_ref, vst_ref, kw_ref, vwt_ref, ov_ref,
                 y_ref, sel_scr, *, t, ch, seq_len):
    i = pl.program_id(1)
    nc = seq_len // CMP_STRIDE
    n_cmp = nc - 1
    n_sel = seq_len // SEL_BLOCK
    k_top = min(SEL_TOPK, n_sel)
    blocks_per_chunk = ch // SEL_BLOCK
    t2 = 2 * t

    qt = qt_ref[0]
    sig = jax.nn.sigmoid(ngt_ref[0])
    lane2 = lax.broadcasted_iota(jnp.int32, (1, t2), 1)
    tpos2 = i * t + jnp.where(lane2 >= t, lane2 - t, lane2)
    hi = ((i + 1) * t) // ch
    lo_win = jnp.maximum(i * t - WINDOW, 0) // ch

    def flash(k_ref, vt_ref, h, qpad, lo, mask_fn):
        def body(c, carry):
            m, l, acc = carry
            k = k_ref[0, pl.ds(pl.multiple_of(c * ch, ch), ch), :]
            s = jnp.dot(k, qpad, preferred_element_type=jnp.float32)
            key = c * ch + lax.broadcasted_iota(jnp.int32, (ch, t2), 0)
            s = jnp.where(mask_fn(c, key), s, NEG_INF)
            m_new = jnp.maximum(m, jnp.max(s, axis=0, keepdims=True))
            alpha = jnp.exp(m - m_new)
            p = jnp.exp(s - m_new)
            l = alpha * l + jnp.sum(p, axis=0, keepdims=True)
            vt = vt_ref[0, c, h * HEAD_DIM:(h + 1) * HEAD_DIM, :]
            acc = alpha * acc + jnp.dot(vt, p.astype(MXU_DTYPE), preferred_element_type=jnp.float32)
            return m_new, l, acc

        init = (jnp.full((1, t2), NEG_INF, jnp.float32), jnp.zeros((1, t2), jnp.float32),
                jnp.zeros((HEAD_DIM, t2), jnp.float32))
        _, l, acc = lax.fori_loop(lo, hi, body, init)
        return acc / l

    def sel_mask(c, key):
        rows = [jnp.broadcast_to(sel_scr[pl.ds(c * blocks_per_chunk + r, 1), :], (SEL_BLOCK, t))
                for r in range(blocks_per_chunk)]
        chosen = jnp.concatenate(rows, axis=0)
        chosen = jnp.concatenate([chosen, chosen], axis=1)
        return (key <= tpos2) & (chosen > 0.5)

    def win_mask(c, key):
        return (key <= tpos2) & (key > tpos2 - WINDOW)

    out_rows = []
    for h in range(N_KV_HEADS):
        qh = jnp.concatenate([qt[(2 * h) * HEAD_DIM:(2 * h + 1) * HEAD_DIM],
                              qt[(2 * h + 1) * HEAD_DIM:(2 * h + 2) * HEAD_DIM]], axis=1)
        zero = jnp.zeros_like(qh)
        qpad = jnp.concatenate([qh, zero] if h == 0 else [zero, qh], axis=0)

        sc = jnp.dot(kcmp_ref[0], qpad, preferred_element_type=jnp.float32)
        n_i = lax.broadcasted_iota(jnp.int32, (nc, t2), 0)
        valid = (n_i * CMP_STRIDE + (CMP_LEN - 1) <= tpos2) & (n_i < n_cmp)
        sm = jnp.where(valid, sc, NEG_INF)
        e = jnp.where(valid, jnp.exp(sm - jnp.max(sm, axis=0, keepdims=True)), 0.0)
        den = jnp.sum(e, axis=0, keepdims=True)
        p_cmp = e / jnp.where(den > 0.0, den, 1.0)
        o_cmp = jnp.dot(vcmpt_ref[0, h * HEAD_DIM:(h + 1) * HEAD_DIM, :], p_cmp.astype(MXU_DTYPE),
                        preferred_element_type=jnp.float32)

        p_slc = jnp.dot(ov_ref[...], p_cmp, preferred_element_type=jnp.float32,
                        precision=lax.Precision.HIGHEST)
        p_slc = p_slc[:, :t] + p_slc[:, t:]
        blk = lax.broadcasted_iota(jnp.int32, (n_sel, t), 0)
        cur = (i * t + lax.broadcasted_iota(jnp.int32, (n_sel, t), 1)) >> 6
        dist = cur - blk
        forced = (blk == 0) | ((dist >= 0) & (dist < N_LOCAL_BLOCKS))
        causal = dist >= 0
        score = jnp.where(forced, jnp.inf, jnp.where(causal, p_slc, -jnp.inf))
        rank = jnp.zeros((n_sel, t), jnp.int32)
        for r in range(n_sel):
            row = score[r:r + 1, :]
            beats = (row > score) | ((row == score) & (blk > r))
            rank = rank + beats.astype(jnp.int32)
        sel_scr[...] = jnp.where((rank < k_top) & causal, 1.0, 0.0)

        o_sel = flash(ks_ref, vst_ref, h, qpad, 0, sel_mask)
        o_win = flash(kw_ref, vwt_ref, h, qpad, lo_win, win_mask)

        def gate(r):
            a = (2 * h) * 3 + r
            b = (2 * h + 1) * 3 + r
            return jnp.concatenate([sig[a:a + 1, :], sig[b:b + 1, :]], axis=1)

        o = gate(0) * o_cmp + gate(1) * o_sel + gate(2) * o_win
        out_rows += [o[:, :t], o[:, t:]]

    y_ref[0] = jnp.concatenate(out_rows, axis=0).T.astype(y_ref.dtype)


def _overlap_t(n_cmp_pad, n_sel):
    cs = np.arange(n_cmp_pad) * CMP_STRIDE
    ce = cs + CMP_LEN
    ss = np.arange(n_sel) * SEL_BLOCK
    se = ss + SEL_BLOCK
    return ((cs[None, :] < se[:, None]) & (ce[None, :] > ss[:, None])).astype(np.float32)


def _attn_call(qt, ngt, kcmp, vcmpt, ks, vst, kw, vwt):
    B, _, S = qt.shape
    t, ch = min(T_ATT, S), CH_ATT
    nc = S // CMP_STRIDE
    n_sel = S // SEL_BLOCK
    ov = jnp.asarray(_overlap_t(nc, n_sel))
    per_b3 = lambda a: pl.BlockSpec((1,) + a.shape[1:], lambda b, i: (b, 0, 0))
    per_b4 = lambda a: pl.BlockSpec((1,) + a.shape[1:], lambda b, i: (b, 0, 0, 0))
    return pl.pallas_call(
        functools.partial(_attn_kernel, t=t, ch=ch, seq_len=S),
        grid=(B, S // t),
        in_specs=[pl.BlockSpec((1, MIX_W, t), lambda b, i: (b, 0, i)),
                  pl.BlockSpec((1, 16, t), lambda b, i: (b, 0, i)),
                  per_b3(kcmp), per_b3(vcmpt), per_b3(ks), per_b4(vst), per_b3(kw), per_b4(vwt),
                  pl.BlockSpec(ov.shape, lambda b, i: (0, 0))],
        out_specs=pl.BlockSpec((1, t, MIX_W), lambda b, i: (b, i, 0)),
        out_shape=jax.ShapeDtypeStruct((B, S, MIX_W), MXU_DTYPE),
        scratch_shapes=[pltpu.VMEM((n_sel, t), jnp.float32)],
        compiler_params=pltpu.CompilerParams(
            dimension_semantics=("arbitrary", "arbitrary"), vmem_limit_bytes=VMEM_LIMIT),
        name="nsa_attention",
    )(qt, ngt, kcmp, vcmpt, ks, vst, kw, vwt, ov)


def _merge_kernel(x_ref, g1_ref, yacd_ref, yb_ref, wbr_ref, wg_ref, bg_ref, wo_ref, h_ref):
    x = x_ref[...]
    xn = _rms_rows(x, g1_ref[...]).astype(MXU_DTYPE)
    ys = (yacd_ref[:, 0:MIX_W], yb_ref[...], yacd_ref[:, MIX_W:2 * MIX_W],
          yacd_ref[:, 2 * MIX_W:3 * MIX_W])
    mixed = jnp.zeros(x.shape, jnp.float32)
    for n, y in enumerate(ys):
        cols = slice(n * D_MODEL, (n + 1) * D_MODEL)
        proj = jnp.dot(y, wbr_ref[n], preferred_element_type=jnp.float32)
        logits = jnp.dot(xn, wg_ref[:, cols], preferred_element_type=jnp.float32) + bg_ref[:, cols]
        mixed = mixed + jax.nn.sigmoid(logits) * proj
    h_ref[...] = x + jnp.dot(mixed.astype(MXU_DTYPE), wo_ref[...],
                             preferred_element_type=jnp.float32)


def _merge_call(x2, g1, yacd2, yb2, wbr, wg, bg, wo):
    n, d = x2.shape
    tm = min(TM_MERGE, n)
    full = lambda a: pl.BlockSpec(a.shape, lambda i: (0,) * a.ndim)
    rows = lambda w: pl.BlockSpec((tm, w), lambda i: (i, 0))
    return pl.pallas_call(
        _merge_kernel,
        grid=(n // tm,),
        in_specs=[rows(d), full(g1), rows(3 * MIX_W), rows(MIX_W), full(wbr), full(wg), full(bg),
                  full(wo)],
        out_specs=rows(d),
        out_shape=jax.ShapeDtypeStruct((n, d), jnp.float32),
        compiler_params=pltpu.CompilerParams(
            dimension_semantics=("arbitrary",), vmem_limit_bytes=VMEM_LIMIT),
        name="gated_merge",
    )(x2, g1, yacd2, yb2, wbr, wg, bg, wo)


def _mlp_kernel(h_ref, g2_ref, w1_ref, w2_ref, o_ref):
    h = h_ref[...]
    hn = _rms_rows(h, g2_ref[...]).astype(MXU_DTYPE)
    acc = h
    for c in range(D_FF // D_MODEL):
        cols = slice(c * D_MODEL, (c + 1) * D_MODEL)
        hid = jnp.dot(hn, w1_ref[:, cols], preferred_element_type=jnp.float32)
        hid = jnp.square(jnp.maximum(hid, 0.0)).astype(MXU_DTYPE)
        acc = acc + jnp.dot(hid, w2_ref[cols, :], preferred_element_type=jnp.float32)
    o_ref[...] = acc


def _mlp_call(h2, g2, w1, w2):
    n, d = h2.shape
    tm = min(TM_MLP, n)
    full = lambda a: pl.BlockSpec(a.shape, lambda i: (0,) * a.ndim)
    rows = pl.BlockSpec((tm, d), lambda i: (i, 0))
    return pl.pallas_call(
        _mlp_kernel,
        grid=(n // tm,),
        in_specs=[rows, full(g2), full(w1), full(w2)],
        out_specs=rows,
        out_shape=jax.ShapeDtypeStruct((n, d), jnp.float32),
        compiler_params=pltpu.CompilerParams(
            dimension_semantics=("arbitrary",), vmem_limit_bytes=VMEM_LIMIT),
        name="relu2_mlp",
    )(h2, g2, w1, w2)


def _pad_w_in(w_in):
    pad = jnp.zeros((w_in.shape[0], LANES - NSA_GATE_W), w_in.dtype)
    return jnp.concatenate(
        [w_in[:, :_NG_SRC + NSA_GATE_W], pad, w_in[:, _NG_SRC + NSA_GATE_W:]], axis=1)


def _compress_weights(pe, w1, w2):
    eye = jnp.eye(N_KV_HEADS, dtype=w1.dtype)
    w1r = w1.reshape(CMP_LEN, HEAD_DIM, CMP_HIDDEN)

    def expand(w):
        return jnp.einsum('lde,hg->lhdge', w, eye).reshape(
            CMP_STRIDE * KV_W, N_KV_HEADS * CMP_HIDDEN)

    def expand_pe(p):
        return jnp.tile(p[:, None, :], (1, N_KV_HEADS, 1)).reshape(1, CMP_STRIDE * KV_W)

    w2b = jnp.einsum('ed,hg->hegd', w2, eye).reshape(N_KV_HEADS * CMP_HIDDEN, KV_W)
    return (expand_pe(pe[:CMP_STRIDE]), expand_pe(pe[CMP_STRIDE:]),
            expand(w1r[:CMP_STRIDE]), expand(w1r[CMP_STRIDE:]), w2b)


def _layer_mixers(x, l, p):
    row = lambda a: a.reshape(1, -1)
    kg = p["nsa_k_norm_g"][l]
    tile_heads = lambda g: jnp.tile(g, N_KV_HEADS).reshape(1, KV_W)
    bst = jnp.repeat(p["gmlp_bs"][l].T, MIX_W // GMLP_GROUPS, axis=1)
    qg = jnp.broadcast_to(jnp.tile(p["nsa_q_norm_g"][l], MIX_W // HEAD_DIM)[:, None], (MIX_W, LANES))
    (yacd, qt, kc, vc, ks, vst, kw, vwt, ngt) = _proj_call(
        x, row(p["norm1_g"][l]), _pad_w_in(p["w_in"][l]).astype(MXU_DTYPE),
        row(p["gmlp_ln_g"][l]), row(p["gmlp_ln_b"][l]), p["gmlp_ws"][l], bst, qg,
        tile_heads(kg[1]), tile_heads(kg[2]),
        p["conf_conv_w"][l], row(p["conf_conv_b"][l]), row(p["conf_ln_g"][l]), row(p["conf_ln_b"][l]),
        p["sconv_w"][l])
    cw = [_compress_weights(p["nsa_cmp_pe"][l, s], p["nsa_cmp_w1"][l, s], p["nsa_cmp_w2"][l, s])
          for s in range(2)]
    pet, peb, wt, wb, w2b = (jnp.stack([cw[0][n], cw[1][n]]) for n in range(5))
    kcmp, vcmpt = _compress_call(kc, vc, pet, peb, wt.astype(MXU_DTYPE), wb.astype(MXU_DTYPE),
                                 w2b.astype(MXU_DTYPE), tile_heads(kg[0]))
    yb = _attn_call(qt, ngt, kcmp, vcmpt, ks, vst, kw, vwt)
    return yacd, yb


def kernel(x, norm1_g, w_in, gmlp_ln_g, gmlp_ln_b, gmlp_ws, gmlp_bs, nsa_q_norm_g, nsa_k_norm_g,
           nsa_cmp_pe, nsa_cmp_w1, nsa_cmp_w2, conf_conv_w, conf_conv_b, conf_ln_g, conf_ln_b,
           sconv_w, w_branch, w_gate, b_gate, w_out, norm2_g, w_mlp1, w_mlp2):
    B, S, D = x.shape
    assert D == D_MODEL and S % min(TM_PROJ, S) == 0 and S % min(T_ATT, S) == 0 and S % CH_ATT == 0
    p = dict(norm1_g=norm1_g, w_in=w_in, gmlp_ln_g=gmlp_ln_g, gmlp_ln_b=gmlp_ln_b, gmlp_ws=gmlp_ws,
             gmlp_bs=gmlp_bs, nsa_q_norm_g=nsa_q_norm_g, nsa_k_norm_g=nsa_k_norm_g,
             nsa_cmp_pe=nsa_cmp_pe, nsa_cmp_w1=nsa_cmp_w1, nsa_cmp_w2=nsa_cmp_w2,
             conf_conv_w=conf_conv_w, conf_conv_b=conf_conv_b, conf_ln_g=conf_ln_g,
             conf_ln_b=conf_ln_b, sconv_w=sconv_w)
    for l in range(norm1_g.shape[0]):
        yacd, yb = _layer_mixers(x, l, p)
        h = _merge_call(
            x.reshape(B * S, D), norm1_g[l].reshape(1, D), yacd.reshape(B * S, 3 * MIX_W),
            yb.reshape(B * S, MIX_W), w_branch[l].astype(MXU_DTYPE), w_gate[l].astype(MXU_DTYPE),
            b_gate[l].reshape(1, -1), w_out[l].astype(MXU_DTYPE))
        x = _mlp_call(h, norm2_g[l].reshape(1, D), w_mlp1[l].astype(MXU_DTYPE),
                      w_mlp2[l].astype(MXU_DTYPE)).reshape(B, S, D)
    return x
```

```python
import functools

import numpy as np
import jax
import jax.numpy as jnp
from jax import lax
from jax.experimental import pallas as pl
from jax.experimental.pallas import tpu as pltpu

D_MODEL = 1024
MIX_W = 256
HEAD_DIM = 64
N_KV_HEADS = 2
Q_PER_KV = 2
KV_W = N_KV_HEADS * HEAD_DIM
GMLP_GROUPS = 4
GMLP_CHUNK = 128
CMP_LEN = 32
CMP_STRIDE = 16
CMP_HIDDEN = 128
SEL_BLOCK = 64
SEL_TOPK = 8
N_LOCAL_BLOCKS = 2
WINDOW = 512
CONF_KERNEL = 31
SCONV_KERNEL = 3
D_FF = 4 * D_MODEL
NSA_GATE_W = 12
NEG_INF = -1e30

LANES = 128
MXU_DTYPE = jnp.bfloat16
VMEM_LIMIT = 56 * 1024 * 1024

_C_GU, _C_GV, _C_Q = 0, 256, 512
_C_KC, _C_VC, _C_KS, _C_VS, _C_KW, _C_VW = 768, 896, 1024, 1152, 1280, 1408
_C_NG = 1536
_C_CA, _C_CB, _C_SB, _C_SC, _C_SH = 1664, 1920, 2176, 2432, 2688
IN_COLS_PAD = 2944
_NG_SRC = 1536

TM_PROJ = 256
TM_MERGE = 256
TM_MLP = 256
T_ATT = 256
CH_ATT = 256
LOG2E = 1.4426950408889634
CONV_HALO = 32
SCONV_HALO = 8


def _f32(x):
    return x.astype(jnp.float32)


def _rms_rows(x, g, eps=1e-6):
    return x * lax.rsqrt(jnp.mean(x * x, axis=-1, keepdims=True) + eps) * g


def _layernorm_rows(x, g, b, eps=1e-5):
    mu = jnp.mean(x, axis=-1, keepdims=True)
    xc = x - mu
    var = jnp.mean(xc * xc, axis=-1, keepdims=True)
    return xc * lax.rsqrt(var + eps) * g + b


def _head_rms_rows(x, g, eps=1e-6):
    n_heads = x.shape[-1] // HEAD_DIM
    head = lax.broadcasted_iota(jnp.int32, x.shape, 1) >> 6
    xx = x * x
    scale = jnp.zeros_like(x)
    for h in range(n_heads):
        ms = jnp.sum(jnp.where(head == h, xx, 0.0), axis=-1, keepdims=True) * (1.0 / HEAD_DIM)
        scale = jnp.where(head == h, lax.rsqrt(ms + eps), scale)
    return x * scale * g


def _proj_kernel(x_ref, g1_ref, w_ref, lng_ref, lnb_ref, ws_ref, bst_ref, qg_ref, kgs_ref, kgw_ref,
                 cw_ref, cbias_ref, clg_ref, clb_ref, sw_ref,
                 yacd_ref, qt_ref, kc_ref, vc_ref, ks_ref, vst_ref, kw_ref, vwt_ref, ngt_ref,
                 zbuf, cbuf, *, tm, ch):
    i = pl.program_id(1)
    x = x_ref[0]
    xn = _rms_rows(x, g1_ref[...]).astype(MXU_DTYPE)

    def proj(lo, width):
        return jnp.dot(xn, w_ref[:, lo:lo + width], preferred_element_type=jnp.float32)

    u = jax.nn.gelu(proj(_C_GU, MIX_W))
    v = jax.nn.gelu(proj(_C_GV, MIX_W))
    v = _layernorm_rows(v, lng_ref[...], lnb_ref[...]).astype(MXU_DTYPE)
    tri = (lax.broadcasted_iota(jnp.int32, (GMLP_CHUNK, GMLP_CHUNK), 0)
           >= lax.broadcasted_iota(jnp.int32, (GMLP_CHUNK, GMLP_CHUNK), 1))
    grp = lax.broadcasted_iota(jnp.int32, (GMLP_CHUNK, MIX_W), 1) >> 6
    wsm = [jnp.where(tri, ws_ref[g], 0.0).astype(MXU_DTYPE) for g in range(GMLP_GROUPS)]
    for c in range(tm // GMLP_CHUNK):
        rows = slice(c * GMLP_CHUNK, (c + 1) * GMLP_CHUNK)
        vch = v[rows]
        mixed = jnp.zeros((GMLP_CHUNK, MIX_W), jnp.float32)
        for g in range(GMLP_GROUPS):
            r = jnp.dot(wsm[g], vch, preferred_element_type=jnp.float32)
            mixed = jnp.where(grp == g, r, mixed)
        yacd_ref[0, rows, 0:MIX_W] = (u[rows] * (mixed + bst_ref[...])).astype(yacd_ref.dtype)

    z = proj(_C_CA, MIX_W) * jax.nn.sigmoid(proj(_C_CB, MIX_W))

    @pl.when(i == 0)
    def _():
        zbuf[0:CONV_HALO, :] = jnp.zeros((CONV_HALO, MIX_W), jnp.float32)
        cbuf[0:SCONV_HALO, :] = jnp.zeros((SCONV_HALO, MIX_W), jnp.float32)

    zbuf[CONV_HALO:CONV_HALO + tm, :] = z
    acc = jnp.zeros((tm, MIX_W), jnp.float32)
    for j in range(CONF_KERNEL):
        acc = acc + cw_ref[j:j + 1, :] * zbuf[pl.ds(CONV_HALO - (CONF_KERNEL - 1) + j, tm), :]
    acc = acc + cbias_ref[...]
    yacd_ref[0, :, MIX_W:2 * MIX_W] = jax.nn.silu(
        _layernorm_rows(acc, clg_ref[...], clb_ref[...])).astype(yacd_ref.dtype)
    zbuf[0:CONV_HALO, :] = zbuf[tm:tm + CONV_HALO, :]

    sb = proj(_C_SB, MIX_W)
    cbuf[SCONV_HALO:SCONV_HALO + tm, :] = proj(_C_SC, MIX_W) * proj(_C_SH, MIX_W)
    acc = jnp.zeros((tm, MIX_W), jnp.float32)
    for j in range(SCONV_KERNEL):
        acc = acc + sw_ref[j:j + 1, :] * cbuf[pl.ds(SCONV_HALO - (SCONV_KERNEL - 1) + j, tm), :]
    yacd_ref[0, :, 2 * MIX_W:3 * MIX_W] = (sb * acc).astype(yacd_ref.dtype)
    cbuf[0:SCONV_HALO, :] = cbuf[tm:tm + SCONV_HALO, :]

    qt = proj(_C_Q, MIX_W).T
    gq = jnp.concatenate([qg_ref[...]] * (tm // LANES), axis=1)
    pieces = []
    for h in range(MIX_W // HEAD_DIM):
        blk = qt[h * HEAD_DIM:(h + 1) * HEAD_DIM]
        ms = jnp.sum(blk * blk, axis=0, keepdims=True) * (1.0 / HEAD_DIM)
        pieces.append(blk * lax.rsqrt(ms + 1e-6))
    qt_ref[0] = (jnp.concatenate(pieces, axis=0) * gq
                 * (HEAD_DIM ** -0.5 * LOG2E)).astype(qt_ref.dtype)

    kc_ref[0] = proj(_C_KC, KV_W)
    vc_ref[0] = proj(_C_VC, KV_W)
    ks_ref[0] = _head_rms_rows(proj(_C_KS, KV_W), kgs_ref[...]).astype(ks_ref.dtype)
    kw_ref[0] = _head_rms_rows(proj(_C_KW, KV_W), kgw_ref[...]).astype(kw_ref.dtype)
    vst = proj(_C_VS, KV_W).T
    vwt = proj(_C_VW, KV_W).T
    for c in range(tm // ch):
        vst_ref[0, c] = vst[:, c * ch:(c + 1) * ch].astype(vst_ref.dtype)
        vwt_ref[0, c] = vwt[:, c * ch:(c + 1) * ch].astype(vwt_ref.dtype)
    ngt_ref[0] = proj(_C_NG, LANES).T[0:16, :]


def _proj_call(x, g1, w, lng, lnb, ws, bst, qg, kgs, kgw, cw, cbias, clg, clb, sw):
    B, S, D = x.shape
    tm, ch = min(TM_PROJ, S), CH_ATT
    full = lambda a: pl.BlockSpec(a.shape, lambda b, i: (0,) * a.ndim)
    seq = lambda width: pl.BlockSpec((1, tm, width), lambda b, i: (b, i, 0))
    ins = (g1, w, lng, lnb, ws, bst, qg, kgs, kgw, cw, cbias, clg, clb, sw)
    out_shape = (
        jax.ShapeDtypeStruct((B, S, 3 * MIX_W), MXU_DTYPE),
        jax.ShapeDtypeStruct((B, MIX_W, S), MXU_DTYPE),
        jax.ShapeDtypeStruct((B, S, KV_W), jnp.float32),
        jax.ShapeDtypeStruct((B, S, KV_W), jnp.float32),
        jax.ShapeDtypeStruct((B, S, KV_W), MXU_DTYPE),
        jax.ShapeDtypeStruct((B, S // ch, KV_W, ch), MXU_DTYPE),
        jax.ShapeDtypeStruct((B, S, KV_W), MXU_DTYPE),
        jax.ShapeDtypeStruct((B, S // ch, KV_W, ch), MXU_DTYPE),
        jax.ShapeDtypeStruct((B, 16, S), jnp.float32),
    )
    vt_spec = pl.BlockSpec((1, tm // ch, KV_W, ch), lambda b, i: (b, i, 0, 0))
    out_specs = (
        seq(3 * MIX_W),
        pl.BlockSpec((1, MIX_W, tm), lambda b, i: (b, 0, i)),
        seq(KV_W), seq(KV_W), seq(KV_W), vt_spec, seq(KV_W), vt_spec,
        pl.BlockSpec((1, 16, tm), lambda b, i: (b, 0, i)),
    )
    return pl.pallas_call(
        functools.partial(_proj_kernel, tm=tm, ch=ch),
        grid=(B, S // tm),
        in_specs=[seq(D)] + [full(a) for a in ins],
        out_specs=out_specs,
        out_shape=out_shape,
        scratch_shapes=[pltpu.VMEM((CONV_HALO + tm, MIX_W), jnp.float32),
                        pltpu.VMEM((SCONV_HALO + tm, MIX_W), jnp.float32)],
        compiler_params=pltpu.CompilerParams(
            dimension_semantics=("arbitrary", "arbitrary"), vmem_limit_bytes=VMEM_LIMIT),
        name="proj_mixers",
    )(x, *ins)


def _compress_kernel(kc_ref, vc_ref, pet_ref, peb_ref, wt_ref, wb_ref, w2_ref, kg_ref,
                     kcmp_ref, vcmpt_ref):
    nc = kc_ref.shape[1] // CMP_STRIDE

    def compress(src_ref, idx):
        chunks = jnp.concatenate(
            [src_ref[0, pl.ds(l, nc, stride=CMP_STRIDE), :] for l in range(CMP_STRIDE)], axis=1)
        a1 = jnp.dot((chunks + pet_ref[idx]).astype(MXU_DTYPE), wt_ref[idx],
                     preferred_element_type=jnp.float32)
        a2 = jnp.dot((chunks + peb_ref[idx]).astype(MXU_DTYPE), wb_ref[idx],
                     preferred_element_type=jnp.float32)
        hid = jax.nn.gelu(a1 + pltpu.roll(a2, shift=nc - 1, axis=0))
        return jnp.dot(hid.astype(MXU_DTYPE), w2_ref[idx], preferred_element_type=jnp.float32)

    kcmp_ref[0] = _head_rms_rows(compress(kc_ref, 0), kg_ref[...]).astype(kcmp_ref.dtype)
    vcmpt_ref[0] = compress(vc_ref, 1).T.astype(vcmpt_ref.dtype)


def _compress_call(kc, vc, pet, peb, wt, wb, w2, kg):
    B, S, _ = kc.shape
    nc = S // CMP_STRIDE
    full = lambda a: pl.BlockSpec(a.shape, lambda b: (0,) * a.ndim)
    per_b = lambda r, c: pl.BlockSpec((1, r, c), lambda b: (b, 0, 0))
    ins = (pet, peb, wt, wb, w2, kg)
    return pl.pallas_call(
        _compress_kernel,
        grid=(B,),
        in_specs=[per_b(S, KV_W)] * 2 + [full(a) for a in ins],
        out_specs=(per_b(nc, KV_W), per_b(KV_W, nc)),
        out_shape=(jax.ShapeDtypeStruct((B, nc, KV_W), MXU_DTYPE),
                   jax.ShapeDtypeStruct((B, KV_W, nc), MXU_DTYPE)),
        compiler_params=pltpu.CompilerParams(
            dimension_semantics=("arbitrary",), vmem_limit_bytes=VMEM_LIMIT),
        name="nsa_compress",
    )(kc, vc, *ins)


def _attn_kernel(qt_ref, ngt_ref, kcmp_ref, vcmpt_ref, ks_ref, vst_ref, kw_ref, vwt_ref, ov_ref,
                 y_ref, selb_scr, *, t, seq_len):
    i = pl.program_id(1)
    nc = seq_len // CMP_STRIDE
    n_cmp = nc - 1
    n_sel = seq_len // SEL_BLOCK
    k_top = min(SEL_TOPK, n_sel)
    bpc = t // SEL_BLOCK
    nw = WINDOW // t
    t2 = 2 * t
    f32 = jnp.float32

    def both(a):
        return jnp.concatenate([a, a], axis=1)

    def keep_bias(cond):
        return jnp.where(cond, 0.0, NEG_INF).astype(f32)

    qt = qt_ref[0]
    qpads = []
    for h in range(N_KV_HEADS):
        qh = jnp.concatenate([qt[(2 * h) * HEAD_DIM:(2 * h + 1) * HEAD_DIM],
                              qt[(2 * h + 1) * HEAD_DIM:(2 * h + 2) * HEAD_DIM]], axis=1)
        zero = jnp.zeros_like(qh)
        qpads.append(jnp.concatenate([qh, zero] if h == 0 else [zero, qh], axis=0))

    n_i = lax.broadcasted_iota(jnp.int32, (nc, t), 0)
    qpos_c = i * t + lax.broadcasted_iota(jnp.int32, (nc, t), 1)
    cmp_ok = (n_i * CMP_STRIDE + (CMP_LEN - 1) <= qpos_c) & (n_i < n_cmp)
    cmp_bias = both(keep_bias(cmp_ok))
    cmp_keep = both(jnp.where(cmp_ok, 1.0, 0.0).astype(f32))

    blk = lax.broadcasted_iota(jnp.int32, (n_sel, t), 0)
    cur = (i * t + lax.broadcasted_iota(jnp.int32, (n_sel, t), 1)) >> 6
    dist = cur - blk
    causal = dist >= 0
    forced = (blk == 0) | (causal & (dist < N_LOCAL_BLOCKS))
    blk8 = lax.broadcasted_iota(jnp.int32, (8, t), 0)

    kcmp = kcmp_ref[0]
    o_cmp = []
    for h in range(N_KV_HEADS):
        sm = jnp.dot(kcmp, qpads[h], preferred_element_type=f32) + cmp_bias
        e = jnp.exp2(sm - jnp.max(sm, axis=0, keepdims=True)) * cmp_keep
        den = jnp.sum(e, axis=0, keepdims=True)
        p_cmp = e * (1.0 / jnp.where(den > 0.0, den, 1.0))
        o_cmp.append(jnp.dot(vcmpt_ref[0, h * HEAD_DIM:(h + 1) * HEAD_DIM, :],
                             p_cmp.astype(MXU_DTYPE), preferred_element_type=f32))

        p_slc = jnp.dot(ov_ref[...], p_cmp, preferred_element_type=f32,
                        precision=lax.Precision.HIGHEST)
        p_slc = p_slc[:, :t] + p_slc[:, t:]
        score = jnp.where(forced, jnp.inf, jnp.where(causal, p_slc, -jnp.inf))
        groups = [score[8 * g:8 * g + 8] for g in range(n_sel // 8)]
        rank = [jnp.zeros((8, t), f32) for _ in groups]
        for r in range(n_sel):
            row = score[r:r + 1, :]
            for g, sg in enumerate(groups):
                if 8 * g > r:
                    beats = row >= sg
                elif 8 * g + 7 <= r:
                    beats = row > sg
                else:
                    beats = (row > sg) | ((row == sg) & (blk8 > r - 8 * g))
                rank[g] = rank[g] + jnp.where(beats, 1.0, 0.0)
        selected = (jnp.concatenate(rank, axis=0) < k_top) & causal
        selb_scr[h] = both(keep_bias(selected))

    r_i = lax.broadcasted_iota(jnp.int32, (t, t), 0)
    q_i = lax.broadcasted_iota(jnp.int32, (t, t), 1)
    diag_bias = both(keep_bias(r_i <= q_i))

    def sel_scores(c, k, h, extra=None):
        s = jnp.dot(k, qpads[h], preferred_element_type=f32)
        parts = []
        for r in range(bpc):
            rows = slice(r * SEL_BLOCK, (r + 1) * SEL_BLOCK)
            part = s[rows] + selb_scr[h, pl.ds(c * bpc + r, 1), :]
            parts.append(part if extra is None else part + extra[rows])
        return jnp.concatenate(parts, axis=0)

    def online(state, s, vt):
        m, l, acc = state
        m_new = jnp.maximum(m, jnp.max(s, axis=0, keepdims=True))
        alpha = jnp.exp2(m - m_new)
        p = jnp.exp2(s - m_new)
        l = alpha * l + jnp.sum(p, axis=0, keepdims=True)
        acc = alpha * acc + jnp.dot(vt, p.astype(MXU_DTYPE), preferred_element_type=f32)
        return m_new, l, acc

    def sel_chunk(c, states, extra=None):
        k = ks_ref[0, pl.ds(pl.multiple_of(c * t, t), t), :]
        return tuple(
            online(states[h], sel_scores(c, k, h, extra),
                   vst_ref[0, c, h * HEAD_DIM:(h + 1) * HEAD_DIM, :])
            for h in range(N_KV_HEADS))

    init = tuple((jnp.full((1, t2), NEG_INF, f32), jnp.zeros((1, t2), f32),
                  jnp.zeros((HEAD_DIM, t2), f32)) for _ in range(N_KV_HEADS))
    states = lax.fori_loop(0, i, lambda c, st: sel_chunk(c, st), init)
    states = sel_chunk(i, states, diag_bias)
    o_sel = [acc * (1.0 / l) for (_, l, acc) in states]

    first_bias = both(keep_bias(r_i > q_i))
    kws, biases, cis = [], [], []
    for cc in range(nw + 1):
        ci = i - nw + cc
        cic = jnp.maximum(ci, 0)
        exists = jnp.where(ci >= 0, 0.0, NEG_INF).astype(f32)
        if cc == nw:
            bias = diag_bias
        elif cc == 0:
            bias = first_bias + exists
        else:
            bias = jnp.full((1, t2), exists, f32)
        kws.append(kw_ref[0, pl.ds(pl.multiple_of(cic * t, t), t), :])
        biases.append(bias)
        cis.append(cic)
    o_win = []
    for h in range(N_KV_HEADS):
        ss = [jnp.dot(k, qpads[h], preferred_element_type=f32) + b for k, b in zip(kws, biases)]
        m = functools.reduce(jnp.maximum, [jnp.max(s, axis=0, keepdims=True) for s in ss])
        ps = [jnp.exp2(s - m) for s in ss]
        l = functools.reduce(lambda a, b: a + b, [jnp.sum(p, axis=0, keepdims=True) for p in ps])
        acc = functools.reduce(lambda a, b: a + b, [
            jnp.dot(vwt_ref[0, cic, h * HEAD_DIM:(h + 1) * HEAD_DIM, :], p.astype(MXU_DTYPE),
                    preferred_element_type=f32) for cic, p in zip(cis, ps)])
        o_win.append(acc * (1.0 / l))

    sig = jax.nn.sigmoid(ngt_ref[0])
    out_rows = []
    for h in range(N_KV_HEADS):
        def gate(r):
            a = (2 * h) * 3 + r
            b = (2 * h + 1) * 3 + r
            return jnp.concatenate([sig[a:a + 1, :], sig[b:b + 1, :]], axis=1)

        o = gate(0) * o_cmp[h] + gate(1) * o_sel[h] + gate(2) * o_win[h]
        out_rows += [o[:, :t], o[:, t:]]
    y_ref[0] = jnp.concatenate(out_rows, axis=0).T.astype(y_ref.dtype)


def _overlap_t(n_cmp_pad, n_sel):
    cs = np.arange(n_cmp_pad) * CMP_STRIDE
    ce = cs + CMP_LEN
    ss = np.arange(n_sel) * SEL_BLOCK
    se = ss + SEL_BLOCK
    return ((cs[None, :] < se[:, None]) & (ce[None, :] > ss[:, None])).astype(np.float32)


def _attn_call(qt, ngt, kcmp, vcmpt, ks, vst, kw, vwt):
    B, _, S = qt.shape
    t = T_ATT
    assert CH_ATT == t and WINDOW % t == 0 and S % t == 0 and (S // SEL_BLOCK) % 8 == 0
    nc = S // CMP_STRIDE
    n_sel = S // SEL_BLOCK
    ov = jnp.asarray(_overlap_t(nc, n_sel))
    per_b3 = lambda a: pl.BlockSpec((1,) + a.shape[1:], lambda b, i: (b, 0, 0))
    per_b4 = lambda a: pl.BlockSpec((1,) + a.shape[1:], lambda b, i: (b, 0, 0, 0))
    return pl.pallas_call(
        functools.partial(_attn_kernel, t=t, seq_len=S),
        grid=(B, S // t),
        in_specs=[pl.BlockSpec((1, MIX_W, t), lambda b, i: (b, 0, i)),
                  pl.BlockSpec((1, 16, t), lambda b, i: (b, 0, i)),
                  per_b3(kcmp), per_b3(vcmpt), per_b3(ks), per_b4(vst), per_b3(kw), per_b4(vwt),
                  pl.BlockSpec(ov.shape, lambda b, i: (0, 0))],
        out_specs=pl.BlockSpec((1, t, MIX_W), lambda b, i: (b, i, 0)),
        out_shape=jax.ShapeDtypeStruct((B, S, MIX_W), MXU_DTYPE),
        scratch_shapes=[pltpu.VMEM((N_KV_HEADS, n_sel, 2 * t), jnp.float32)],
        compiler_params=pltpu.CompilerParams(
            dimension_semantics=("arbitrary", "arbitrary"), vmem_limit_bytes=VMEM_LIMIT),
        name="nsa_attention",
    )(qt, ngt, kcmp, vcmpt, ks, vst, kw, vwt, ov)


def _merge_kernel(x_ref, g1_ref, yacd_ref, yb_ref, wbr_ref, wg_ref, bg_ref, wo_ref, h_ref):
    x = x_ref[...]
    xn = _rms_rows(x, g1_ref[...]).astype(MXU_DTYPE)
    ys = (yacd_ref[:, 0:MIX_W], yb_ref[...], yacd_ref[:, MIX_W:2 * MIX_W],
          yacd_ref[:, 2 * MIX_W:3 * MIX_W])
    mixed = jnp.zeros(x.shape, jnp.float32)
    for n, y in enumerate(ys):
        cols = slice(n * D_MODEL, (n + 1) * D_MODEL)
        proj = jnp.dot(y, wbr_ref[n], preferred_element_type=jnp.float32)
        logits = jnp.dot(xn, wg_ref[:, cols], preferred_element_type=jnp.float32) + bg_ref[:, cols]
        mixed = mixed + jax.nn.sigmoid(logits) * proj
    h_ref[...] = x + jnp.dot(mixed.astype(MXU_DTYPE), wo_ref[...],
                             preferred_element_type=jnp.float32)


def _merge_call(x2, g1, yacd2, yb2, wbr, wg, bg, wo):
    n, d = x2.shape
    tm = min(TM_MERGE, n)
    full = lambda a: pl.BlockSpec(a.shape, lambda i: (0,) * a.ndim)
    rows = lambda w: pl.BlockSpec((tm, w), lambda i: (i, 0))
    return pl.pallas_call(
        _merge_kernel,
        grid=(n // tm,),
        in_specs=[rows(d), full(g1), rows(3 * MIX_W), rows(MIX_W), full(wbr), full(wg), full(bg),
                  full(wo)],
        out_specs=rows(d),
        out_shape=jax.ShapeDtypeStruct((n, d), jnp.float32),
        compiler_params=pltpu.CompilerParams(
            dimension_semantics=("arbitrary",), vmem_limit_bytes=VMEM_LIMIT),
        name="gated_merge",
    )(x2, g1, yacd2, yb2, wbr, wg, bg, wo)


def _mlp_kernel(h_ref, g2_ref, w1_ref, w2_ref, o_ref):
    h = h_ref[...]
    hn = _rms_rows(h, g2_ref[...]).astype(MXU_DTYPE)
    acc = h
    for c in range(D_FF // D_MODEL):
        cols = slice(c * D_MODEL, (c + 1) * D_MODEL)
        hid = jnp.dot(hn, w1_ref[:, cols], preferred_element_type=jnp.float32)
        hid = jnp.square(jnp.maximum(hid, 0.0)).astype(MXU_DTYPE)
        acc = acc + jnp.dot(hid, w2_ref[cols, :], preferred_element_type=jnp.float32)
    o_ref[...] = acc


def _mlp_call(h2, g2, w1, w2):
    n, d = h2.shape
    tm = min(TM_MLP, n)
    full = lambda a: pl.BlockSpec(a.shape, lambda i: (0,) * a.ndim)
    rows = pl.BlockSpec((tm, d), lambda i: (i, 0))
    return pl.pallas_call(
        _mlp_kernel,
        grid=(n // tm,),
        in_specs=[rows, full(g2), full(w1), full(w2)],
        out_specs=rows,
        out_shape=jax.ShapeDtypeStruct((n, d), jnp.float32),
        compiler_params=pltpu.CompilerParams(
            dimension_semantics=("arbitrary",), vmem_limit_bytes=VMEM_LIMIT),
        name="relu2_mlp",
    )(h2, g2, w1, w2)


def _pad_w_in(w_in):
    pad = jnp.zeros((w_in.shape[0], LANES - NSA_GATE_W), w_in.dtype)
    return jnp.concatenate(
        [w_in[:, :_NG_SRC + NSA_GATE_W], pad, w_in[:, _NG_SRC + NSA_GATE_W:]], axis=1)


def _compress_weights(pe, w1, w2):
    eye = jnp.eye(N_KV_HEADS, dtype=w1.dtype)
    w1r = w1.reshape(CMP_LEN, HEAD_DIM, CMP_HIDDEN)

    def expand(w):
        return jnp.einsum('lde,hg->lhdge', w, eye).reshape(
            CMP_STRIDE * KV_W, N_KV_HEADS * CMP_HIDDEN)

    def expand_pe(p):
        return jnp.tile(p[:, None, :], (1, N_KV_HEADS, 1)).reshape(1, CMP_STRIDE * KV_W)

    w2b = jnp.einsum('ed,hg->hegd', w2, eye).reshape(N_KV_HEADS * CMP_HIDDEN, KV_W)
    return (expand_pe(pe[:CMP_STRIDE]), expand_pe(pe[CMP_STRIDE:]),
            expand(w1r[:CMP_STRIDE]), expand(w1r[CMP_STRIDE:]), w2b)


def _layer_mixers(x, l, p):
    row = lambda a: a.reshape(1, -1)
    kg = p["nsa_k_norm_g"][l]
    tile_heads = lambda g: jnp.tile(g, N_KV_HEADS).reshape(1, KV_W)
    bst = jnp.repeat(p["gmlp_bs"][l].T, MIX_W // GMLP_GROUPS, axis=1)
    qg = jnp.broadcast_to(jnp.tile(p["nsa_q_norm_g"][l], MIX_W // HEAD_DIM)[:, None], (MIX_W, LANES))
    (yacd, qt, kc, vc, ks, vst, kw, vwt, ngt) = _proj_call(
        x, row(p["norm1_g"][l]), _pad_w_in(p["w_in"][l]).astype(MXU_DTYPE),
        row(p["gmlp_ln_g"][l]), row(p["gmlp_ln_b"][l]), p["gmlp_ws"][l], bst, qg,
        tile_heads(kg[1]), tile_heads(kg[2]),
        p["conf_conv_w"][l], row(p["conf_conv_b"][l]), row(p["conf_ln_g"][l]), row(p["conf_ln_b"][l]),
        p["sconv_w"][l])
    cw = [_compress_weights(p["nsa_cmp_pe"][l, s], p["nsa_cmp_w1"][l, s], p["nsa_cmp_w2"][l, s])
          for s in range(2)]
    pet, peb, wt, wb, w2b = (jnp.stack([cw[0][n], cw[1][n]]) for n in range(5))
    kcmp, vcmpt = _compress_call(kc, vc, pet, peb, wt.astype(MXU_DTYPE), wb.astype(MXU_DTYPE),
                                 w2b.astype(MXU_DTYPE), tile_heads(kg[0]))
    yb = _attn_call(qt, ngt, kcmp, vcmpt, ks, vst, kw, vwt)
    return yacd, yb


def kernel(x, norm1_g, w_in, gmlp_ln_g, gmlp_ln_b, gmlp_ws, gmlp_bs, nsa_q_norm_g, nsa_k_norm_g,
           nsa_cmp_pe, nsa_cmp_w1, nsa_cmp_w2, conf_conv_w, conf_conv_b, conf_ln_g, conf_ln_b,
           sconv_w, w_branch, w_gate, b_gate, w_out, norm2_g, w_mlp1, w_mlp2):
    B, S, D = x.shape
    assert D == D_MODEL and S % min(TM_PROJ, S) == 0 and S % min(T_ATT, S) == 0 and S % CH_ATT == 0
    p = dict(norm1_g=norm1_g, w_in=w_in, gmlp_ln_g=gmlp_ln_g, gmlp_ln_b=gmlp_ln_b, gmlp_ws=gmlp_ws,
             gmlp_bs=gmlp_bs, nsa_q_norm_g=nsa_q_norm_g, nsa_k_norm_g=nsa_k_norm_g,
             nsa_cmp_pe=nsa_cmp_pe, nsa_cmp_w1=nsa_cmp_w1, nsa_cmp_w2=nsa_cmp_w2,
             conf_conv_w=conf_conv_w, conf_conv_b=conf_conv_b, conf_ln_g=conf_ln_g,
             conf_ln_b=conf_ln_b, sconv_w=sconv_w)
    for l in range(norm1_g.shape[0]):
        yacd, yb = _layer_mixers(x, l, p)
        h = _merge_call(
            x.reshape(B * S, D), norm1_g[l].reshape(1, D), yacd.reshape(B * S, 3 * MIX_W),
            yb.reshape(B * S, MIX_W), w_branch[l].astype(MXU_DTYPE), w_gate[l].astype(MXU_DTYPE),
            b_gate[l].reshape(1, -1), w_out[l].astype(MXU_DTYPE))
        x = _mlp_call(h, norm2_g[l].reshape(1, D), w_mlp1[l].astype(MXU_DTYPE),
                      w_mlp2[l].astype(MXU_DTYPE)).reshape(B, S, D)
    return x
```

```python
import functools

import numpy as np
import jax
import jax.numpy as jnp
from jax import lax
from jax.experimental import pallas as pl
from jax.experimental.pallas import tpu as pltpu

D_MODEL = 1024
MIX_W = 256
HEAD_DIM = 64
N_KV_HEADS = 2
Q_PER_KV = 2
KV_W = N_KV_HEADS * HEAD_DIM
GMLP_GROUPS = 4
GMLP_CHUNK = 128
CMP_LEN = 32
CMP_STRIDE = 16
CMP_HIDDEN = 128
SEL_BLOCK = 64
SEL_TOPK = 8
N_LOCAL_BLOCKS = 2
WINDOW = 512
CONF_KERNEL = 31
SCONV_KERNEL = 3
D_FF = 4 * D_MODEL
NSA_GATE_W = 12
NEG_INF = -1e30

LANES = 128
SUBLANES = 8
MXU_DTYPE = jnp.bfloat16
VMEM_LIMIT = 56 * 1024 * 1024

_C_GU, _C_GV, _C_Q = 0, 256, 512
_C_KC, _C_VC, _C_KS, _C_VS, _C_KW, _C_VW = 768, 896, 1024, 1152, 1280, 1408
_C_NG = 1536
_C_CA, _C_CB, _C_SB, _C_SC, _C_SH = 1664, 1920, 2176, 2432, 2688
IN_COLS_PAD = 2944
_NG_SRC = 1536

TM_PROJ = 256
TM_MERGE = 256
TM_MLP = 256
T_ATT = 256
CH_ATT = 256
LOG2E = 1.4426950408889634
V_AUG_ROWS = 16
CMP_PAD = 8
CONV_HALO = 32
SCONV_HALO = 8


def _f32(x):
    return x.astype(jnp.float32)


def _rms_rows(x, g, eps=1e-6):
    return x * lax.rsqrt(jnp.mean(x * x, axis=-1, keepdims=True) + eps) * g


def _layernorm_rows(x, g, b, eps=1e-5):
    mu = jnp.mean(x, axis=-1, keepdims=True)
    xc = x - mu
    var = jnp.mean(xc * xc, axis=-1, keepdims=True)
    return xc * lax.rsqrt(var + eps) * g + b


def _head_rms_rows(x, g, eps=1e-6):
    n_heads = x.shape[-1] // HEAD_DIM
    head = lax.broadcasted_iota(jnp.int32, x.shape, 1) >> 6
    xx = x * x
    scale = jnp.zeros_like(x)
    for h in range(n_heads):
        ms = jnp.sum(jnp.where(head == h, xx, 0.0), axis=-1, keepdims=True) * (1.0 / HEAD_DIM)
        scale = jnp.where(head == h, lax.rsqrt(ms + eps), scale)
    return x * scale * g


def _proj_kernel(x_ref, g1_ref, wa_ref, wng_ref, wb_ref, lng_ref, lnb_ref, ws_ref, bst_ref, qg_ref,
                 kgs_ref, kgw_ref, cw_ref, cbias_ref, clg_ref, clb_ref, sw_ref,
                 yacd_ref, qt_ref, kc_ref, vc_ref, ks_ref, vst_ref, kw_ref, vwt_ref, ngt_ref,
                 zbuf, cbuf, *, tm, ch):
    i = pl.program_id(1)
    x = x_ref[0]
    xn = _rms_rows(x, g1_ref[...]).astype(MXU_DTYPE)

    def proj(lo, width):
        if lo < _C_NG:
            w = wa_ref[:, lo:lo + width]
        elif lo == _C_NG:
            w = wng_ref[...]
        else:
            w = wb_ref[:, lo - _C_CA:lo - _C_CA + width]
        return jnp.dot(xn, w, preferred_element_type=jnp.float32)

    u = jax.nn.gelu(proj(_C_GU, MIX_W))
    v = jax.nn.gelu(proj(_C_GV, MIX_W))
    v = _layernorm_rows(v, lng_ref[...], lnb_ref[...]).astype(MXU_DTYPE)
    tri = (lax.broadcasted_iota(jnp.int32, (GMLP_CHUNK, GMLP_CHUNK), 0)
           >= lax.broadcasted_iota(jnp.int32, (GMLP_CHUNK, GMLP_CHUNK), 1))
    grp = lax.broadcasted_iota(jnp.int32, (GMLP_CHUNK, MIX_W), 1) >> 6
    wsm = [jnp.where(tri, ws_ref[g], 0.0).astype(MXU_DTYPE) for g in range(GMLP_GROUPS)]
    for c in range(tm // GMLP_CHUNK):
        rows = slice(c * GMLP_CHUNK, (c + 1) * GMLP_CHUNK)
        vch = v[rows]
        mixed = jnp.zeros((GMLP_CHUNK, MIX_W), jnp.float32)
        for g in range(GMLP_GROUPS):
            r = jnp.dot(wsm[g], vch, preferred_element_type=jnp.float32)
            mixed = jnp.where(grp == g, r, mixed)
        yacd_ref[0, rows, 0:MIX_W] = (u[rows] * (mixed + bst_ref[...])).astype(yacd_ref.dtype)

    z = proj(_C_CA, MIX_W) * jax.nn.sigmoid(proj(_C_CB, MIX_W))

    @pl.when(i == 0)
    def _():
        zbuf[0:CONV_HALO, :] = jnp.zeros((CONV_HALO, MIX_W), jnp.float32)
        cbuf[0:SCONV_HALO, :] = jnp.zeros((SCONV_HALO, MIX_W), jnp.float32)

    def causal_conv(buf, w_ref, width, halo):
        offsets = [halo - (width - 1) + j for j in range(width)]
        out = jnp.zeros((tm, MIX_W), jnp.float32)
        for b in range(SUBLANES):
            taps = [j for j in range(width) if offsets[j] % SUBLANES == b]
            if not taps:
                continue
            rows = tm if b == 0 else tm + SUBLANES
            part = jnp.zeros((rows, MIX_W), jnp.float32)
            for j in taps:
                part = part + w_ref[j:j + 1, :] * buf[pl.ds(offsets[j] - b, rows), :]
            out = out + part[b:b + tm]
        return out

    zbuf[CONV_HALO:CONV_HALO + tm, :] = z
    acc = causal_conv(zbuf, cw_ref, CONF_KERNEL, CONV_HALO) + cbias_ref[...]
    yacd_ref[0, :, MIX_W:2 * MIX_W] = jax.nn.silu(
        _layernorm_rows(acc, clg_ref[...], clb_ref[...])).astype(yacd_ref.dtype)
    zbuf[0:CONV_HALO, :] = zbuf[tm:tm + CONV_HALO, :]

    sb = proj(_C_SB, MIX_W)
    cbuf[SCONV_HALO:SCONV_HALO + tm, :] = proj(_C_SC, MIX_W) * proj(_C_SH, MIX_W)
    yacd_ref[0, :, 2 * MIX_W:3 * MIX_W] = (
        sb * causal_conv(cbuf, sw_ref, SCONV_KERNEL, SCONV_HALO)).astype(yacd_ref.dtype)
    cbuf[0:SCONV_HALO, :] = cbuf[tm:tm + SCONV_HALO, :]

    qt = proj(_C_Q, MIX_W).T
    gq = jnp.concatenate([qg_ref[...]] * (tm // LANES), axis=1)
    pieces = []
    for h in range(MIX_W // HEAD_DIM):
        blk = qt[h * HEAD_DIM:(h + 1) * HEAD_DIM]
        ms = jnp.sum(blk * blk, axis=0, keepdims=True) * (1.0 / HEAD_DIM)
        pieces.append(blk * lax.rsqrt(ms + 1e-6))
    qt_ref[0] = (jnp.concatenate(pieces, axis=0) * gq
                 * (HEAD_DIM ** -0.5 * LOG2E)).astype(qt_ref.dtype)

    kc_ref[0] = proj(_C_KC, KV_W)
    vc_ref[0] = proj(_C_VC, KV_W)
    ks_ref[0] = _head_rms_rows(proj(_C_KS, KV_W), kgs_ref[...]).astype(ks_ref.dtype)
    kw_ref[0] = _head_rms_rows(proj(_C_KW, KV_W), kgw_ref[...]).astype(kw_ref.dtype)
    vst = proj(_C_VS, KV_W).T
    vwt = proj(_C_VW, KV_W).T
    for c in range(tm // ch):
        vst_ref[0, c] = vst[:, c * ch:(c + 1) * ch].astype(vst_ref.dtype)
        vwt_ref[0, c] = vwt[:, c * ch:(c + 1) * ch].astype(vwt_ref.dtype)
    ngt_ref[0] = proj(_C_NG, LANES).T[0:16, :]


def _proj_call(x, g1, wa, wng, wb, lng, lnb, ws, bst, qg, kgs, kgw, cw, cbias, clg, clb, sw):
    B, S, D = x.shape
    tm, ch = min(TM_PROJ, S), CH_ATT
    full = lambda a: pl.BlockSpec(a.shape, lambda b, i: (0,) * a.ndim)
    seq = lambda width: pl.BlockSpec((1, tm, width), lambda b, i: (b, i, 0))
    ins = (g1, wa, wng, wb, lng, lnb, ws, bst, qg, kgs, kgw, cw, cbias, clg, clb, sw)
    out_shape = (
        jax.ShapeDtypeStruct((B, S, 3 * MIX_W), MXU_DTYPE),
        jax.ShapeDtypeStruct((B, MIX_W, S), MXU_DTYPE),
        jax.ShapeDtypeStruct((B, S, KV_W), jnp.float32),
        jax.ShapeDtypeStruct((B, S, KV_W), jnp.float32),
        jax.ShapeDtypeStruct((B, S, KV_W), MXU_DTYPE),
        jax.ShapeDtypeStruct((B, S // ch, KV_W, ch), MXU_DTYPE),
        jax.ShapeDtypeStruct((B, S, KV_W), MXU_DTYPE),
        jax.ShapeDtypeStruct((B, S // ch, KV_W, ch), MXU_DTYPE),
        jax.ShapeDtypeStruct((B, 16, S), jnp.float32),
    )
    vt_spec = pl.BlockSpec((1, tm // ch, KV_W, ch), lambda b, i: (b, i, 0, 0))
    out_specs = (
        seq(3 * MIX_W),
        pl.BlockSpec((1, MIX_W, tm), lambda b, i: (b, 0, i)),
        seq(KV_W), seq(KV_W), seq(KV_W), vt_spec, seq(KV_W), vt_spec,
        pl.BlockSpec((1, 16, tm), lambda b, i: (b, 0, i)),
    )
    return pl.pallas_call(
        functools.partial(_proj_kernel, tm=tm, ch=ch),
        grid=(B, S // tm),
        in_specs=[seq(D)] + [full(a) for a in ins],
        out_specs=out_specs,
        out_shape=out_shape,
        scratch_shapes=[pltpu.VMEM((CONV_HALO + tm, MIX_W), jnp.float32),
                        pltpu.VMEM((SCONV_HALO + tm, MIX_W), jnp.float32)],
        compiler_params=pltpu.CompilerParams(
            dimension_semantics=("arbitrary", "arbitrary"), vmem_limit_bytes=VMEM_LIMIT),
        name="proj_mixers",
    )(x, *ins)


def _compress_kernel(kc_ref, vc_ref, pet_ref, peb_ref, wt_ref, wb_ref, w2_ref, kg_ref,
                     kcmp_ref, vcmpt_ref):
    nc = kc_ref.shape[1] // CMP_STRIDE

    def compress(src_ref, idx):
        chunks = jnp.concatenate(
            [src_ref[0, pl.ds(l, nc, stride=CMP_STRIDE), :] for l in range(CMP_STRIDE)], axis=1)
        a1 = jnp.dot((chunks + pet_ref[idx]).astype(MXU_DTYPE), wt_ref[idx],
                     preferred_element_type=jnp.float32)
        a2 = jnp.dot((chunks + peb_ref[idx]).astype(MXU_DTYPE), wb_ref[idx],
                     preferred_element_type=jnp.float32)
        hid = jax.nn.gelu(a1 + pltpu.roll(a2, shift=nc - 1, axis=0))
        return jnp.dot(hid.astype(MXU_DTYPE), w2_ref[idx], preferred_element_type=jnp.float32)

    kcmp_ref[0] = _head_rms_rows(compress(kc_ref, 0), kg_ref[...]).astype(kcmp_ref.dtype)
    vcmpt_ref[0] = compress(vc_ref, 1).T.astype(vcmpt_ref.dtype)


def _compress_call(kc, vc, pet, peb, wt, wb, w2, kg):
    B, S, _ = kc.shape
    nc = S // CMP_STRIDE
    full = lambda a: pl.BlockSpec(a.shape, lambda b: (0,) * a.ndim)
    per_b = lambda r, c: pl.BlockSpec((1, r, c), lambda b: (b, 0, 0))
    ins = (pet, peb, wt, wb, w2, kg)
    return pl.pallas_call(
        _compress_kernel,
        grid=(B,),
        in_specs=[per_b(S, KV_W)] * 2 + [full(a) for a in ins],
        out_specs=(per_b(nc, KV_W), per_b(KV_W, nc)),
        out_shape=(jax.ShapeDtypeStruct((B, nc, KV_W), MXU_DTYPE),
                   jax.ShapeDtypeStruct((B, KV_W, nc), MXU_DTYPE)),
        compiler_params=pltpu.CompilerParams(
            dimension_semantics=("arbitrary",), vmem_limit_bytes=VMEM_LIMIT),
        name="nsa_compress",
    )(kc, vc, *ins)


def _attn_kernel(qt_ref, ngt_ref, kcmp_ref, vcmpt_ref, ks_ref, vst_ref, kw_ref, vwt_ref,
                 y_ref, selb_scr, pcmp_scr, ssel_scr, stail_scr, *, t, seq_len):
    i = pl.program_id(1)
    nc = seq_len // CMP_STRIDE
    n_cmp = nc - 1
    n_sel = seq_len // SEL_BLOCK
    k_top = min(SEL_TOPK, n_sel)
    bpc = t // SEL_BLOCK
    cmp_per_sel = SEL_BLOCK // CMP_STRIDE
    nw = WINDOW // t
    t2 = 2 * t
    f32 = jnp.float32

    def both(a):
        return jnp.concatenate([a, a], axis=1)

    def keep_bias(cond):
        return jnp.where(cond, 0.0, NEG_INF).astype(f32)

    qt = qt_ref[0]
    qpads = []
    for h in range(N_KV_HEADS):
        qh = jnp.concatenate([qt[(2 * h) * HEAD_DIM:(2 * h + 1) * HEAD_DIM],
                              qt[(2 * h + 1) * HEAD_DIM:(2 * h + 2) * HEAD_DIM]], axis=1)
        zero = jnp.zeros_like(qh)
        qpads.append(jnp.concatenate([qh, zero] if h == 0 else [zero, qh], axis=0))

    n_i = lax.broadcasted_iota(jnp.int32, (nc, t), 0)
    qpos_c = i * t + lax.broadcasted_iota(jnp.int32, (nc, t), 1)
    cmp_ok = (n_i * CMP_STRIDE + (CMP_LEN - 1) <= qpos_c) & (n_i < n_cmp)
    cmp_bias = both(keep_bias(cmp_ok))
    cmp_keep = both(jnp.where(cmp_ok, 1.0, 0.0).astype(f32))

    blk = lax.broadcasted_iota(jnp.int32, (n_sel, t), 0)
    cur = (i * t + lax.broadcasted_iota(jnp.int32, (n_sel, t), 1)) >> 6
    dist = cur - blk
    causal = dist >= 0
    forced = (blk == 0) | (causal & (dist < N_LOCAL_BLOCKS))
    blk8 = lax.broadcasted_iota(jnp.int32, (8, t), 0)

    kcmp = kcmp_ref[0]
    pcmp_scr[:, 0:CMP_PAD, :] = jnp.zeros((t // LANES, CMP_PAD, LANES), f32)
    o_cmp = []
    for h in range(N_KV_HEADS):
        sm = jnp.dot(kcmp, qpads[h], preferred_element_type=f32) + cmp_bias
        e = jnp.exp2(sm - jnp.max(sm, axis=0, keepdims=True)) * cmp_keep
        den = jnp.sum(e, axis=0, keepdims=True)
        p_cmp = e * (1.0 / jnp.where(den > 0.0, den, 1.0))
        o_cmp.append(jnp.dot(vcmpt_ref[0, h * HEAD_DIM:(h + 1) * HEAD_DIM, :],
                             p_cmp.astype(MXU_DTYPE), preferred_element_type=f32))

        p_pair = p_cmp[:, :t] + p_cmp[:, t:]
        slabs = []
        for lb in range(t // LANES):
            pcmp_scr[lb, CMP_PAD:CMP_PAD + nc, :] = p_pair[:, lb * LANES:(lb + 1) * LANES]
            slabs.append(functools.reduce(lambda a, b: a + b, [
                pcmp_scr[lb, pl.ds(CMP_PAD + off, n_sel, stride=cmp_per_sel), :]
                for off in range(1 - CMP_LEN // CMP_STRIDE, cmp_per_sel)]))
        p_slc = jnp.concatenate(slabs, axis=1)
        score = jnp.where(forced, jnp.inf, jnp.where(causal, p_slc, -jnp.inf))
        groups = [score[8 * g:8 * g + 8] for g in range(n_sel // 8)]
        rank = [jnp.zeros((8, t), f32) for _ in groups]
        for r in range(n_sel):
            row = score[r:r + 1, :]
            for g, sg in enumerate(groups):
                if 8 * g > r:
                    beats = row >= sg
                elif 8 * g + 7 <= r:
                    beats = row > sg
                else:
                    beats = (row > sg) | ((row == sg) & (blk8 > r - 8 * g))
                rank[g] = rank[g] + jnp.where(beats, 1.0, 0.0)
        selected = (jnp.concatenate(rank, axis=0) < k_top) & causal
        selb_scr[h] = both(keep_bias(selected))

    r_i = lax.broadcasted_iota(jnp.int32, (t, t), 0)
    q_i = lax.broadcasted_iota(jnp.int32, (t, t), 1)
    diag_bias = both(keep_bias(r_i <= q_i))
    ones_rows = jnp.where(lax.broadcasted_iota(jnp.int32, (V_AUG_ROWS, t), 0) == 0,
                          1.0, 0.0).astype(MXU_DTYPE)

    def k_chunk(k_ref, c):
        return k_ref[0, pl.ds(pl.multiple_of(c * t, t), t), :]

    def v_aug(vt_ref, c, h):
        return jnp.concatenate([vt_ref[0, c, h * HEAD_DIM:(h + 1) * HEAD_DIM, :], ones_rows], axis=0)

    def colmax8(s):
        return jnp.max(s.reshape(t // 8, 8, t2), axis=0)

    def weighted_values(s, m, vaug):
        p = jnp.exp2(s - m).astype(MXU_DTYPE)
        return jnp.dot(vaug, p, preferred_element_type=f32)

    def normalise(acc):
        return acc[0:HEAD_DIM] * (1.0 / acc[HEAD_DIM:HEAD_DIM + 1])

    heads = range(N_KV_HEADS)
    neg8 = jnp.full((8, t2), NEG_INF, f32)

    def sel_scores(c, k, h, extra):
        s = jnp.dot(k, qpads[h], preferred_element_type=f32)
        parts = []
        for r in range(bpc):
            rows = slice(r * SEL_BLOCK, (r + 1) * SEL_BLOCK)
            bias = selb_scr[h, pl.ds(c * bpc + r, 1), :]
            if extra is not None:
                bias = bias + (extra[rows] if extra.shape[0] == t else extra)
            parts.append(s[rows] + bias)
        return jnp.concatenate(parts, axis=0)

    def sel_pass1(pair, pm):
        pm = list(pm)
        for u in range(2):
            c = 2 * pair + u
            k = k_chunk(ks_ref, c)
            for h in heads:
                s = sel_scores(c, k, h, None)
                ssel_scr[h, c] = s
                pm[h] = jnp.maximum(pm[h], colmax8(s))
        return tuple(pm)

    n_pairs = i >> 1
    pm = lax.fori_loop(0, n_pairs, sel_pass1, (neg8, neg8))
    c_left = jnp.maximum(i - 1, 0)
    left_bias = jnp.full((1, t2), jnp.where((i & 1) == 1, 0.0, NEG_INF), f32)
    tail = ((c_left, left_bias), (i, diag_bias))
    pm = list(pm)
    for u, (c, extra) in enumerate(tail):
        k = k_chunk(ks_ref, c)
        for h in heads:
            s = sel_scores(c, k, h, extra)
            stail_scr[h, u] = s
            pm[h] = jnp.maximum(pm[h], colmax8(s))
    m_sel = [jnp.max(pm[h], axis=0, keepdims=True) for h in heads]

    def sel_pass2(pair, accs):
        accs = list(accs)
        for u in range(2):
            c = 2 * pair + u
            for h in heads:
                accs[h] = accs[h] + weighted_values(ssel_scr[h, c], m_sel[h], v_aug(vst_ref, c, h))
        return tuple(accs)

    zero_acc = jnp.zeros((HEAD_DIM + V_AUG_ROWS, t2), f32)
    accs = list(lax.fori_loop(0, n_pairs, sel_pass2, (zero_acc, zero_acc)))
    for u, (c, _) in enumerate(tail):
        for h in heads:
            accs[h] = accs[h] + weighted_values(stail_scr[h, u], m_sel[h], v_aug(vst_ref, c, h))
    o_sel = [normalise(accs[h]) for h in heads]

    first_bias = both(keep_bias(r_i > q_i))
    kws, biases, cis = [], [], []
    for cc in range(nw + 1):
        ci = i - nw + cc
        cic = jnp.maximum(ci, 0)
        exists = jnp.where(ci >= 0, 0.0, NEG_INF).astype(f32)
        if cc == nw:
            bias = diag_bias
        elif cc == 0:
            bias = first_bias + exists
        else:
            bias = jnp.full((1, t2), exists, f32)
        kws.append(k_chunk(kw_ref, cic))
        biases.append(bias)
        cis.append(cic)
    o_win = []
    for h in heads:
        ss = [jnp.dot(k, qpads[h], preferred_element_type=f32) + b for k, b in zip(kws, biases)]
        m = jnp.max(functools.reduce(jnp.maximum, [colmax8(s) for s in ss]), axis=0, keepdims=True)
        acc = functools.reduce(lambda a, b: a + b, [
            weighted_values(s, m, v_aug(vwt_ref, cic, h)) for cic, s in zip(cis, ss)])
        o_win.append(normalise(acc))

    sig = jax.nn.sigmoid(ngt_ref[0])
    out_rows = []
    for h in range(N_KV_HEADS):
        def gate(r):
            a = (2 * h) * 3 + r
            b = (2 * h + 1) * 3 + r
            return jnp.concatenate([sig[a:a + 1, :], sig[b:b + 1, :]], axis=1)

        o = gate(0) * o_cmp[h] + gate(1) * o_sel[h] + gate(2) * o_win[h]
        out_rows += [o[:, :t], o[:, t:]]
    y_ref[0] = jnp.concatenate(out_rows, axis=0).T.astype(y_ref.dtype)


def _attn_call(qt, ngt, kcmp, vcmpt, ks, vst, kw, vwt):
    B, _, S = qt.shape
    t = T_ATT
    assert CH_ATT == t and WINDOW % t == 0 and S % t == 0 and (S // SEL_BLOCK) % 8 == 0
    assert SEL_BLOCK % CMP_STRIDE == 0 and CMP_LEN % CMP_STRIDE == 0 and CMP_LEN // CMP_STRIDE <= CMP_PAD
    nc = S // CMP_STRIDE
    n_sel = S // SEL_BLOCK
    per_b3 = lambda a: pl.BlockSpec((1,) + a.shape[1:], lambda b, i: (b, 0, 0))
    per_b4 = lambda a: pl.BlockSpec((1,) + a.shape[1:], lambda b, i: (b, 0, 0, 0))
    return pl.pallas_call(
        functools.partial(_attn_kernel, t=t, seq_len=S),
        grid=(B, S // t),
        in_specs=[pl.BlockSpec((1, MIX_W, t), lambda b, i: (b, 0, i)),
                  pl.BlockSpec((1, 16, t), lambda b, i: (b, 0, i)),
                  per_b3(kcmp), per_b3(vcmpt), per_b3(ks), per_b4(vst), per_b3(kw), per_b4(vwt)],
        out_specs=pl.BlockSpec((1, t, MIX_W), lambda b, i: (b, i, 0)),
        out_shape=jax.ShapeDtypeStruct((B, S, MIX_W), MXU_DTYPE),
        scratch_shapes=[
            pltpu.VMEM((N_KV_HEADS, n_sel, 2 * t), jnp.float32),
            pltpu.VMEM((t // LANES, CMP_PAD + nc, LANES), jnp.float32),
            pltpu.VMEM((N_KV_HEADS, S // t, t, 2 * t), jnp.float32),
            pltpu.VMEM((N_KV_HEADS, 2, t, 2 * t), jnp.float32),
        ],
        compiler_params=pltpu.CompilerParams(
            dimension_semantics=("arbitrary", "arbitrary"), vmem_limit_bytes=VMEM_LIMIT),
        name="nsa_attention",
    )(qt, ngt, kcmp, vcmpt, ks, vst, kw, vwt)


def _merge_kernel(x_ref, g1_ref, yacd_ref, yb_ref, wbr_ref, wg_ref, bg_ref, wo_ref, h_ref):
    x = x_ref[...]
    xn = _rms_rows(x, g1_ref[...]).astype(MXU_DTYPE)
    ys = (yacd_ref[:, 0:MIX_W], yb_ref[...], yacd_ref[:, MIX_W:2 * MIX_W],
          yacd_ref[:, 2 * MIX_W:3 * MIX_W])
    mixed = jnp.zeros(x.shape, jnp.float32)
    for n, y in enumerate(ys):
        cols = slice(n * D_MODEL, (n + 1) * D_MODEL)
        proj = jnp.dot(y, wbr_ref[n], preferred_element_type=jnp.float32)
        logits = jnp.dot(xn, wg_ref[:, cols], preferred_element_type=jnp.float32) + bg_ref[:, cols]
        mixed = mixed + jax.nn.sigmoid(logits) * proj
    h_ref[...] = x + jnp.dot(mixed.astype(MXU_DTYPE), wo_ref[...],
                             preferred_element_type=jnp.float32)


def _merge_call(x2, g1, yacd2, yb2, wbr, wg, bg, wo):
    n, d = x2.shape
    tm = min(TM_MERGE, n)
    full = lambda a: pl.BlockSpec(a.shape, lambda i: (0,) * a.ndim)
    rows = lambda w: pl.BlockSpec((tm, w), lambda i: (i, 0))
    return pl.pallas_call(
        _merge_kernel,
        grid=(n // tm,),
        in_specs=[rows(d), full(g1), rows(3 * MIX_W), rows(MIX_W), full(wbr), full(wg), full(bg),
                  full(wo)],
        out_specs=rows(d),
        out_shape=jax.ShapeDtypeStruct((n, d), jnp.float32),
        compiler_params=pltpu.CompilerParams(
            dimension_semantics=("arbitrary",), vmem_limit_bytes=VMEM_LIMIT),
        name="gated_merge",
    )(x2, g1, yacd2, yb2, wbr, wg, bg, wo)


def _mlp_kernel(h_ref, g2_ref, w1_ref, w2_ref, o_ref):
    h = h_ref[...]
    hn = _rms_rows(h, g2_ref[...]).astype(MXU_DTYPE)
    acc = h
    for c in range(D_FF // D_MODEL):
        cols = slice(c * D_MODEL, (c + 1) * D_MODEL)
        hid = jnp.dot(hn, w1_ref[:, cols], preferred_element_type=jnp.float32)
        hid = jnp.square(jnp.maximum(hid, 0.0)).astype(MXU_DTYPE)
        acc = acc + jnp.dot(hid, w2_ref[cols, :], preferred_element_type=jnp.float32)
    o_ref[...] = acc


def _mlp_call(h2, g2, w1, w2):
    n, d = h2.shape
    tm = min(TM_MLP, n)
    full = lambda a: pl.BlockSpec(a.shape, lambda i: (0,) * a.ndim)
    rows = pl.BlockSpec((tm, d), lambda i: (i, 0))
    return pl.pallas_call(
        _mlp_kernel,
        grid=(n // tm,),
        in_specs=[rows, full(g2), full(w1), full(w2)],
        out_specs=rows,
        out_shape=jax.ShapeDtypeStruct((n, d), jnp.float32),
        compiler_params=pltpu.CompilerParams(
            dimension_semantics=("arbitrary",), vmem_limit_bytes=VMEM_LIMIT),
        name="relu2_mlp",
    )(h2, g2, w1, w2)


def _split_w_in_kernel(w_ref, wa_ref, wng_ref, wb_ref):
    w = w_ref[0]
    wa_ref[...] = w[:, :_NG_SRC].astype(wa_ref.dtype)
    ng = w[:, _NG_SRC:_NG_SRC + LANES]
    lane = lax.broadcasted_iota(jnp.int32, ng.shape, 1)
    wng_ref[...] = jnp.where(lane < NSA_GATE_W, ng, 0.0).astype(wng_ref.dtype)
    wb_ref[...] = w[:, _NG_SRC + NSA_GATE_W:].astype(wb_ref.dtype)


def _split_w_in(w_in, l):
    _, d, cols = w_in.shape
    tail = cols - _NG_SRC - NSA_GATE_W
    rows = 256
    blk = lambda width: pl.BlockSpec((rows, width), lambda r: (r, 0))
    return pl.pallas_call(
        _split_w_in_kernel,
        grid=(d // rows,),
        in_specs=[pl.BlockSpec((1, rows, cols), lambda r: (l, r, 0))],
        out_specs=(blk(_NG_SRC), blk(LANES), blk(tail)),
        out_shape=(jax.ShapeDtypeStruct((d, _NG_SRC), MXU_DTYPE),
                   jax.ShapeDtypeStruct((d, LANES), MXU_DTYPE),
                   jax.ShapeDtypeStruct((d, tail), MXU_DTYPE)),
        compiler_params=pltpu.CompilerParams(
            dimension_semantics=("arbitrary",), vmem_limit_bytes=VMEM_LIMIT),
        name="split_w_in",
    )(w_in)


def _compress_weights(pe, w1, w2):
    eye = jnp.eye(N_KV_HEADS, dtype=w1.dtype)
    w1r = w1.reshape(CMP_LEN, HEAD_DIM, CMP_HIDDEN)

    def expand(w):
        return jnp.einsum('lde,hg->lhdge', w, eye).reshape(
            CMP_STRIDE * KV_W, N_KV_HEADS * CMP_HIDDEN)

    def expand_pe(p):
        return jnp.tile(p[:, None, :], (1, N_KV_HEADS, 1)).reshape(1, CMP_STRIDE * KV_W)

    w2b = jnp.einsum('ed,hg->hegd', w2, eye).reshape(N_KV_HEADS * CMP_HIDDEN, KV_W)
    return (expand_pe(pe[:CMP_STRIDE]), expand_pe(pe[CMP_STRIDE:]),
            expand(w1r[:CMP_STRIDE]), expand(w1r[CMP_STRIDE:]), w2b)


def _layer_mixers(x, l, p):
    row = lambda a: a.reshape(1, -1)
    kg = p["nsa_k_norm_g"][l]
    tile_heads = lambda g: jnp.tile(g, N_KV_HEADS).reshape(1, KV_W)
    bst = jnp.repeat(p["gmlp_bs"][l].T, MIX_W // GMLP_GROUPS, axis=1)
    qg = jnp.broadcast_to(jnp.tile(p["nsa_q_norm_g"][l], MIX_W // HEAD_DIM)[:, None], (MIX_W, LANES))
    (yacd, qt, kc, vc, ks, vst, kw, vwt, ngt) = _proj_call(
        x, row(p["norm1_g"][l]), *_split_w_in(p["w_in"], l),
        row(p["gmlp_ln_g"][l]), row(p["gmlp_ln_b"][l]), p["gmlp_ws"][l], bst, qg,
        tile_heads(kg[1]), tile_heads(kg[2]),
        p["conf_conv_w"][l], row(p["conf_conv_b"][l]), row(p["conf_ln_g"][l]), row(p["conf_ln_b"][l]),
        p["sconv_w"][l])
    cw = [_compress_weights(p["nsa_cmp_pe"][l, s], p["nsa_cmp_w1"][l, s], p["nsa_cmp_w2"][l, s])
          for s in range(2)]
    pet, peb, wt, wb, w2b = (jnp.stack([cw[0][n], cw[1][n]]) for n in range(5))
    kcmp, vcmpt = _compress_call(kc, vc, pet, peb, wt.astype(MXU_DTYPE), wb.astype(MXU_DTYPE),
                                 w2b.astype(MXU_DTYPE), tile_heads(kg[0]))
    yb = _attn_call(qt, ngt, kcmp, vcmpt, ks, vst, kw, vwt)
    return yacd, yb


def kernel(x, norm1_g, w_in, gmlp_ln_g, gmlp_ln_b, gmlp_ws, gmlp_bs, nsa_q_norm_g, nsa_k_norm_g,
           nsa_cmp_pe, nsa_cmp_w1, nsa_cmp_w2, conf_conv_w, conf_conv_b, conf_ln_g, conf_ln_b,
           sconv_w, w_branch, w_gate, b_gate, w_out, norm2_g, w_mlp1, w_mlp2):
    B, S, D = x.shape
    assert D == D_MODEL and S % min(TM_PROJ, S) == 0 and S % min(T_ATT, S) == 0 and S % CH_ATT == 0
    p = dict(norm1_g=norm1_g, w_in=w_in, gmlp_ln_g=gmlp_ln_g, gmlp_ln_b=gmlp_ln_b, gmlp_ws=gmlp_ws,
             gmlp_bs=gmlp_bs, nsa_q_norm_g=nsa_q_norm_g, nsa_k_norm_g=nsa_k_norm_g,
             nsa_cmp_pe=nsa_cmp_pe, nsa_cmp_w1=nsa_cmp_w1, nsa_cmp_w2=nsa_cmp_w2,
             conf_conv_w=conf_conv_w, conf_conv_b=conf_conv_b, conf_ln_g=conf_ln_g,
             conf_ln_b=conf_ln_b, sconv_w=sconv_w)
    for l in range(norm1_g.shape[0]):
        yacd, yb = _layer_mixers(x, l, p)
        h = _merge_call(
            x.reshape(B * S, D), norm1_g[l].reshape(1, D), yacd.reshape(B * S, 3 * MIX_W),
            yb.reshape(B * S, MIX_W), w_branch[l].astype(MXU_DTYPE), w_gate[l].astype(MXU_DTYPE),
            b_gate[l].reshape(1, -1), w_out[l].astype(MXU_DTYPE))
        x = _mlp_call(h, norm2_g[l].reshape(1, D), w_mlp1[l].astype(MXU_DTYPE),
                      w_mlp2[l].astype(MXU_DTYPE)).reshape(B, S, D)
    return x
```

```python
import functools

import numpy as np
import jax
import jax.numpy as jnp
from jax import lax
from jax.experimental import pallas as pl
from jax.experimental.pallas import tpu as pltpu

D_MODEL = 1024
MIX_W = 256
HEAD_DIM = 64
N_KV_HEADS = 2
Q_PER_KV = 2
KV_W = N_KV_HEADS * HEAD_DIM
GMLP_GROUPS = 4
GMLP_CHUNK = 128
CMP_LEN = 32
CMP_STRIDE = 16
CMP_HIDDEN = 128
SEL_BLOCK = 64
SEL_TOPK = 8
N_LOCAL_BLOCKS = 2
WINDOW = 512
CONF_KERNEL = 31
SCONV_KERNEL = 3
D_FF = 4 * D_MODEL
NSA_GATE_W = 12
NEG_INF = -1e30

LANES = 128
SUBLANES = 8
MXU_DTYPE = jnp.bfloat16
VMEM_LIMIT = 56 * 1024 * 1024

_C_GU, _C_GV, _C_Q = 0, 256, 512
_C_KC, _C_VC, _C_KS, _C_VS, _C_KW, _C_VW = 768, 896, 1024, 1152, 1280, 1408
_C_NG = 1536
_C_CA, _C_CB, _C_SB, _C_SC, _C_SH = 1664, 1920, 2176, 2432, 2688
IN_COLS_PAD = 2944
_NG_SRC = 1536

TM_PROJ = 512
TM_MERGE = 512
T_ATT = 256
CH_ATT = 256
LOG2E = 1.4426950408889634
V_AUG_ROWS = 16
CMP_PAD = 8
CONV_HALO = 32
SCONV_HALO = 8


def _f32(x):
    return x.astype(jnp.float32)


def _rms_rows(x, g, eps=1e-6):
    return x * lax.rsqrt(jnp.mean(x * x, axis=-1, keepdims=True) + eps) * g


def _layernorm_rows(x, g, b, eps=1e-5):
    mu = jnp.mean(x, axis=-1, keepdims=True)
    xc = x - mu
    var = jnp.mean(xc * xc, axis=-1, keepdims=True)
    return xc * lax.rsqrt(var + eps) * g + b


def _head_rms_rows(x, g, eps=1e-6):
    n_heads = x.shape[-1] // HEAD_DIM
    head = lax.broadcasted_iota(jnp.int32, x.shape, 1) >> 6
    xx = x * x
    scale = jnp.zeros_like(x)
    for h in range(n_heads):
        ms = jnp.sum(jnp.where(head == h, xx, 0.0), axis=-1, keepdims=True) * (1.0 / HEAD_DIM)
        scale = jnp.where(head == h, lax.rsqrt(ms + eps), scale)
    return x * scale * g


def _proj_kernel(x_ref, g1_ref, wa_ref, wng_ref, wb_ref, lng_ref, lnb_ref, ws_ref, bst_ref, qg_ref,
                 kgs_ref, kgw_ref, cw_ref, cbias_ref, clg_ref, clb_ref, sw_ref,
                 yacd_ref, qt_ref, kc_ref, vc_ref, ks_ref, vst_ref, kw_ref, vwt_ref, ngt_ref,
                 zbuf, cbuf, *, tm, ch):
    @pl.when(pl.program_id(1) == 0)
    def _():
        zbuf[...] = jnp.zeros((CONV_HALO, MIX_W), jnp.float32)
        cbuf[...] = jnp.zeros((SCONV_HALO, MIX_W), jnp.float32)

    x = x_ref[0]
    xn = _rms_rows(x, g1_ref[...]).astype(MXU_DTYPE)

    def proj(lo, width):
        if lo < _C_NG:
            w = wa_ref[:, lo:lo + width]
        elif lo == _C_NG:
            w = wng_ref[...]
        else:
            w = wb_ref[:, lo - _C_CA:lo - _C_CA + width]
        return jnp.dot(xn, w, preferred_element_type=jnp.float32)

    def causal_conv(buf, w_ref, width, halo):
        offsets = [halo - (width - 1) + j for j in range(width)]
        w = w_ref[...]
        out = jnp.zeros((tm, MIX_W), jnp.float32)
        for b in range(SUBLANES):
            taps = [j for j in range(width) if offsets[j] % SUBLANES == b]
            if not taps:
                continue
            rows = tm if b == 0 else tm + SUBLANES
            part = jnp.zeros((rows, MIX_W), jnp.float32)
            for j in taps:
                part = part + w[j:j + 1] * buf[offsets[j] - b:offsets[j] - b + rows]
            out = out + part[b:b + tm]
            yield out


    z = proj(_C_CA, MIX_W) * jax.nn.sigmoid(proj(_C_CB, MIX_W))
    conv = causal_conv(jnp.concatenate([zbuf[...], z], axis=0), cw_ref, CONF_KERNEL, CONV_HALO)
    zbuf[...] = z[tm - CONV_HALO:tm]

    def emit_q():
        qt = proj(_C_Q, MIX_W).T
        gq = jnp.concatenate([qg_ref[...]] * (tm // LANES), axis=1)
        pieces = []
        for h in range(MIX_W // HEAD_DIM):
            blk = qt[h * HEAD_DIM:(h + 1) * HEAD_DIM]
            ms = jnp.sum(blk * blk, axis=0, keepdims=True) * (1.0 / HEAD_DIM)
            pieces.append(blk * lax.rsqrt(ms + 1e-6))
        qt_ref[0] = (jnp.concatenate(pieces, axis=0) * gq
                     * (HEAD_DIM ** -0.5 * LOG2E)).astype(qt_ref.dtype)

    def emit_raw(dst_ref, col):
        dst_ref[0] = proj(col, KV_W)

    def emit_key(dst_ref, col, gain_ref):
        dst_ref[0] = _head_rms_rows(proj(col, KV_W), gain_ref[...]).astype(dst_ref.dtype)

    def emit_value_t(dst_ref, col):
        vt = proj(col, KV_W).T
        for c in range(tm // ch):
            dst_ref[0, c] = vt[:, c * ch:(c + 1) * ch].astype(dst_ref.dtype)

    def emit_gates():
        ngt_ref[0] = proj(_C_NG, LANES).T[0:16, :]

    held = {}

    def hold(name, col):
        held[name] = proj(col, MIX_W)

    slices = [
        [lambda: emit_raw(kc_ref, _C_KC), lambda: emit_raw(vc_ref, _C_VC)],
        [lambda: emit_key(ks_ref, _C_KS, kgs_ref), lambda: emit_key(kw_ref, _C_KW, kgw_ref)],
        [lambda: emit_value_t(vst_ref, _C_VS), lambda: emit_value_t(vwt_ref, _C_VW)],
        [emit_q],
        [emit_gates, lambda: hold("sb", _C_SB)],
        [lambda: hold("sc", _C_SC), lambda: hold("sh", _C_SH)],
        [lambda: hold("gu", _C_GU)],
        [lambda: hold("gv", _C_GV)],
    ]
    acc = None
    for tasks in slices:
        for task in tasks:
            task()
        acc = next(conv, acc)
    for acc in conv:
        pass
    yacd_ref[0, :, MIX_W:2 * MIX_W] = jax.nn.silu(_layernorm_rows(
        acc + cbias_ref[...], clg_ref[...], clb_ref[...])).astype(yacd_ref.dtype)

    sc = held["sc"] * held["sh"]
    for acc in causal_conv(jnp.concatenate([cbuf[...], sc], axis=0), sw_ref, SCONV_KERNEL, SCONV_HALO):
        pass
    cbuf[...] = sc[tm - SCONV_HALO:tm]
    yacd_ref[0, :, 2 * MIX_W:3 * MIX_W] = (held["sb"] * acc).astype(yacd_ref.dtype)

    u = jax.nn.gelu(held["gu"])
    v = jax.nn.gelu(held["gv"])
    v = _layernorm_rows(v, lng_ref[...], lnb_ref[...]).astype(MXU_DTYPE)
    tri = (lax.broadcasted_iota(jnp.int32, (GMLP_CHUNK, GMLP_CHUNK), 0)
           >= lax.broadcasted_iota(jnp.int32, (GMLP_CHUNK, GMLP_CHUNK), 1))
    grp = lax.broadcasted_iota(jnp.int32, (GMLP_CHUNK, MIX_W), 1) >> 6
    wsm = [jnp.where(tri, ws_ref[g], 0.0).astype(MXU_DTYPE) for g in range(GMLP_GROUPS)]
    for c in range(tm // GMLP_CHUNK):
        rows = slice(c * GMLP_CHUNK, (c + 1) * GMLP_CHUNK)
        vch = v[rows]
        mixed = jnp.zeros((GMLP_CHUNK, MIX_W), jnp.float32)
        for g in range(GMLP_GROUPS):
            r = jnp.dot(wsm[g], vch, preferred_element_type=jnp.float32)
            mixed = jnp.where(grp == g, r, mixed)
        yacd_ref[0, rows, 0:MIX_W] = (u[rows] * (mixed + bst_ref[...])).astype(yacd_ref.dtype)


def _proj_call(x, g1, wa, wng, wb, lng, lnb, ws, bst, qg, kgs, kgw, cw, cbias, clg, clb, sw):
    B, S, D = x.shape
    tm, ch = min(TM_PROJ, S), CH_ATT
    full = lambda a: pl.BlockSpec(a.shape, lambda b, i: (0,) * a.ndim)
    seq = lambda width: pl.BlockSpec((1, tm, width), lambda b, i: (b, i, 0))
    ins = (g1, wa, wng, wb, lng, lnb, ws, bst, qg, kgs, kgw, cw, cbias, clg, clb, sw)
    out_shape = (
        jax.ShapeDtypeStruct((B, S, 3 * MIX_W), MXU_DTYPE),
        jax.ShapeDtypeStruct((B, MIX_W, S), MXU_DTYPE),
        jax.ShapeDtypeStruct((B, S, KV_W), jnp.float32),
        jax.ShapeDtypeStruct((B, S, KV_W), jnp.float32),
        jax.ShapeDtypeStruct((B, S, KV_W), MXU_DTYPE),
        jax.ShapeDtypeStruct((B, S // ch, KV_W, ch), MXU_DTYPE),
        jax.ShapeDtypeStruct((B, S, KV_W), MXU_DTYPE),
        jax.ShapeDtypeStruct((B, S // ch, KV_W, ch), MXU_DTYPE),
        jax.ShapeDtypeStruct((B, 16, S), jnp.float32),
    )
    vt_spec = pl.BlockSpec((1, tm // ch, KV_W, ch), lambda b, i: (b, i, 0, 0))
    out_specs = (
        seq(3 * MIX_W),
        pl.BlockSpec((1, MIX_W, tm), lambda b, i: (b, 0, i)),
        seq(KV_W), seq(KV_W), seq(KV_W), vt_spec, seq(KV_W), vt_spec,
        pl.BlockSpec((1, 16, tm), lambda b, i: (b, 0, i)),
    )
    return pl.pallas_call(
        functools.partial(_proj_kernel, tm=tm, ch=ch),
        grid=(B, S // tm),
        in_specs=[seq(D)] + [full(a) for a in ins],
        out_specs=out_specs,
        out_shape=out_shape,
        scratch_shapes=[pltpu.VMEM((CONV_HALO, MIX_W), jnp.float32),
                        pltpu.VMEM((SCONV_HALO, MIX_W), jnp.float32)],
        compiler_params=pltpu.CompilerParams(
            dimension_semantics=("arbitrary", "arbitrary"), vmem_limit_bytes=VMEM_LIMIT),
        name="proj_mixers",
    )(x, *ins)


def _compress_kernel(kc_ref, vc_ref, pet_ref, peb_ref, wt_ref, wb_ref, w2_ref, kg_ref,
                     kcmp_ref, vcmpt_ref):
    nc = kc_ref.shape[1] // CMP_STRIDE

    def compress(src_ref, idx):
        chunks = jnp.concatenate(
            [src_ref[0, pl.ds(l, nc, stride=CMP_STRIDE), :] for l in range(CMP_STRIDE)], axis=1)
        a1 = jnp.dot((chunks + pet_ref[idx]).astype(MXU_DTYPE), wt_ref[idx],
                     preferred_element_type=jnp.float32)
        a2 = jnp.dot((chunks + peb_ref[idx]).astype(MXU_DTYPE), wb_ref[idx],
                     preferred_element_type=jnp.float32)
        hid = jax.nn.gelu(a1 + pltpu.roll(a2, shift=nc - 1, axis=0))
        return jnp.dot(hid.astype(MXU_DTYPE), w2_ref[idx], preferred_element_type=jnp.float32)

    kcmp_ref[0] = _head_rms_rows(compress(kc_ref, 0), kg_ref[...]).astype(kcmp_ref.dtype)
    vcmpt_ref[0] = compress(vc_ref, 1).T.astype(vcmpt_ref.dtype)


def _compress_call(kc, vc, pet, peb, wt, wb, w2, kg):
    B, S, _ = kc.shape
    nc = S // CMP_STRIDE
    full = lambda a: pl.BlockSpec(a.shape, lambda b: (0,) * a.ndim)
    per_b = lambda r, c: pl.BlockSpec((1, r, c), lambda b: (b, 0, 0))
    ins = (pet, peb, wt, wb, w2, kg)
    return pl.pallas_call(
        _compress_kernel,
        grid=(B,),
        in_specs=[per_b(S, KV_W)] * 2 + [full(a) for a in ins],
        out_specs=(per_b(nc, KV_W), per_b(KV_W, nc)),
        out_shape=(jax.ShapeDtypeStruct((B, nc, KV_W), MXU_DTYPE),
                   jax.ShapeDtypeStruct((B, KV_W, nc), MXU_DTYPE)),
        compiler_params=pltpu.CompilerParams(
            dimension_semantics=("arbitrary",), vmem_limit_bytes=VMEM_LIMIT),
        name="nsa_compress",
    )(kc, vc, *ins)


def _attn_kernel(qt_ref, ngt_ref, kcmp_ref, vcmpt_ref, ks_ref, vst_ref, kw_ref, vwt_ref,
                 y_ref, selb_scr, pcmp_scr, ssel_scr, stail_scr, *, t, seq_len):
    i = pl.program_id(1)
    nc = seq_len // CMP_STRIDE
    n_cmp = nc - 1
    n_sel = seq_len // SEL_BLOCK
    k_top = min(SEL_TOPK, n_sel)
    bpc = t // SEL_BLOCK
    cmp_per_sel = SEL_BLOCK // CMP_STRIDE
    nw = WINDOW // t
    t2 = 2 * t
    f32 = jnp.float32

    def both(a):
        return jnp.concatenate([a, a], axis=1)

    def keep_bias(cond):
        return jnp.where(cond, 0.0, NEG_INF).astype(f32)

    qt = qt_ref[0]
    qpads = []
    for h in range(N_KV_HEADS):
        qh = jnp.concatenate([qt[(2 * h) * HEAD_DIM:(2 * h + 1) * HEAD_DIM],
                              qt[(2 * h + 1) * HEAD_DIM:(2 * h + 2) * HEAD_DIM]], axis=1)
        zero = jnp.zeros_like(qh)
        qpads.append(jnp.concatenate([qh, zero] if h == 0 else [zero, qh], axis=0))

    n_i = lax.broadcasted_iota(jnp.int32, (nc, t), 0)
    qpos_c = i * t + lax.broadcasted_iota(jnp.int32, (nc, t), 1)
    cmp_ok = (n_i * CMP_STRIDE + (CMP_LEN - 1) <= qpos_c) & (n_i < n_cmp)
    cmp_bias = both(keep_bias(cmp_ok))
    cmp_keep = both(jnp.where(cmp_ok, 1.0, 0.0).astype(f32))

    blk = lax.broadcasted_iota(jnp.int32, (n_sel, t), 0)
    cur = (i * t + lax.broadcasted_iota(jnp.int32, (n_sel, t), 1)) >> 6
    dist = cur - blk
    causal = dist >= 0
    forced = (blk == 0) | (causal & (dist < N_LOCAL_BLOCKS))
    blk8 = lax.broadcasted_iota(jnp.int32, (8, t), 0)

    kcmp = kcmp_ref[0]
    pcmp_scr[:, 0:CMP_PAD, :] = jnp.zeros((t // LANES, CMP_PAD, LANES), f32)
    o_cmp = []
    for h in range(N_KV_HEADS):
        sm = jnp.dot(kcmp, qpads[h], preferred_element_type=f32) + cmp_bias
        e = jnp.exp2(sm - jnp.max(sm, axis=0, keepdims=True)) * cmp_keep
        den = jnp.sum(e, axis=0, keepdims=True)
        p_cmp = e * (1.0 / jnp.where(den > 0.0, den, 1.0))
        o_cmp.append(jnp.dot(vcmpt_ref[0, h * HEAD_DIM:(h + 1) * HEAD_DIM, :],
                             p_cmp.astype(MXU_DTYPE), preferred_element_type=f32))

        p_pair = p_cmp[:, :t] + p_cmp[:, t:]
        slabs = []
        for lb in range(t // LANES):
            pcmp_scr[lb, CMP_PAD:CMP_PAD + nc, :] = p_pair[:, lb * LANES:(lb + 1) * LANES]
            slabs.append(functools.reduce(lambda a, b: a + b, [
                pcmp_scr[lb, pl.ds(CMP_PAD + off, n_sel, stride=cmp_per_sel), :]
                for off in range(1 - CMP_LEN // CMP_STRIDE, cmp_per_sel)]))
        p_slc = jnp.concatenate(slabs, axis=1)
        score = jnp.where(forced, jnp.inf, jnp.where(causal, p_slc, -jnp.inf))
        groups = [score[8 * g:8 * g + 8] for g in range(n_sel // 8)]
        rank = [jnp.zeros((8, t), f32) for _ in groups]
        for r in range(n_sel):
            row = score[r:r + 1, :]
            for g, sg in enumerate(groups):
                if 8 * g > r:
                    beats = row >= sg
                elif 8 * g + 7 <= r:
                    beats = row > sg
                else:
                    beats = (row > sg) | ((row == sg) & (blk8 > r - 8 * g))
                rank[g] = rank[g] + jnp.where(beats, 1.0, 0.0)
        selected = (jnp.concatenate(rank, axis=0) < k_top) & causal
        selb_scr[h] = both(keep_bias(selected))

    r_i = lax.broadcasted_iota(jnp.int32, (t, t), 0)
    q_i = lax.broadcasted_iota(jnp.int32, (t, t), 1)
    diag_bias = both(keep_bias(r_i <= q_i))
    ones_rows = jnp.where(lax.broadcasted_iota(jnp.int32, (V_AUG_ROWS, t), 0) == 0,
                          1.0, 0.0).astype(MXU_DTYPE)

    def k_chunk(k_ref, c):
        return k_ref[0, pl.ds(pl.multiple_of(c * t, t), t), :]

    def v_aug(vt_ref, c, h):
        return jnp.concatenate([vt_ref[0, c, h * HEAD_DIM:(h + 1) * HEAD_DIM, :], ones_rows], axis=0)

    def colmax8(s):
        return jnp.max(s.reshape(t // 8, 8, t2), axis=0)

    def weighted_values(s, m, vaug):
        p = jnp.exp2(s - m).astype(MXU_DTYPE)
        return jnp.dot(vaug, p, preferred_element_type=f32)

    def normalise(acc):
        return acc[0:HEAD_DIM] * (1.0 / acc[HEAD_DIM:HEAD_DIM + 1])

    heads = range(N_KV_HEADS)
    neg8 = jnp.full((8, t2), NEG_INF, f32)

    def sel_scores(c, k, h, extra):
        s = jnp.dot(k, qpads[h], preferred_element_type=f32)
        parts = []
        for r in range(bpc):
            rows = slice(r * SEL_BLOCK, (r + 1) * SEL_BLOCK)
            bias = selb_scr[h, pl.ds(c * bpc + r, 1), :]
            if extra is not None:
                bias = bias + (extra[rows] if extra.shape[0] == t else extra)
            parts.append(s[rows] + bias)
        return jnp.concatenate(parts, axis=0)

    def sel_pass1(pair, pm):
        pm = list(pm)
        for u in range(2):
            c = 2 * pair + u
            k = k_chunk(ks_ref, c)
            for h in heads:
                s = sel_scores(c, k, h, None)
                ssel_scr[h, c] = s
                pm[h] = jnp.maximum(pm[h], colmax8(s))
        return tuple(pm)

    n_pairs = i >> 1
    pm = lax.fori_loop(0, n_pairs, sel_pass1, (neg8, neg8))
    c_left = jnp.maximum(i - 1, 0)
    left_bias = jnp.full((1, t2), jnp.where((i & 1) == 1, 0.0, NEG_INF), f32)
    tail = ((c_left, left_bias), (i, diag_bias))
    pm = list(pm)
    for u, (c, extra) in enumerate(tail):
        k = k_chunk(ks_ref, c)
        for h in heads:
            s = sel_scores(c, k, h, extra)
            stail_scr[h, u] = s
            pm[h] = jnp.maximum(pm[h], colmax8(s))
    m_sel = [jnp.max(pm[h], axis=0, keepdims=True) for h in heads]

    def sel_pass2(pair, accs):
        accs = list(accs)
        for u in range(2):
            c = 2 * pair + u
            for h in heads:
                accs[h] = accs[h] + weighted_values(ssel_scr[h, c], m_sel[h], v_aug(vst_ref, c, h))
        return tuple(accs)

    zero_acc = jnp.zeros((HEAD_DIM + V_AUG_ROWS, t2), f32)
    accs = list(lax.fori_loop(0, n_pairs, sel_pass2, (zero_acc, zero_acc)))
    for u, (c, _) in enumerate(tail):
        for h in heads:
            accs[h] = accs[h] + weighted_values(stail_scr[h, u], m_sel[h], v_aug(vst_ref, c, h))
    o_sel = [normalise(accs[h]) for h in heads]

    first_bias = both(keep_bias(r_i > q_i))
    kws, biases, cis = [], [], []
    for cc in range(nw + 1):
        ci = i - nw + cc
        cic = jnp.maximum(ci, 0)
        exists = jnp.where(ci >= 0, 0.0, NEG_INF).astype(f32)
        if cc == nw:
            bias = diag_bias
        elif cc == 0:
            bias = first_bias + exists
        else:
            bias = jnp.full((1, t2), exists, f32)
        kws.append(k_chunk(kw_ref, cic))
        biases.append(bias)
        cis.append(cic)
    o_win = []
    for h in heads:
        ss = [jnp.dot(k, qpads[h], preferred_element_type=f32) + b for k, b in zip(kws, biases)]
        m = jnp.max(functools.reduce(jnp.maximum, [colmax8(s) for s in ss]), axis=0, keepdims=True)
        acc = functools.reduce(lambda a, b: a + b, [
            weighted_values(s, m, v_aug(vwt_ref, cic, h)) for cic, s in zip(cis, ss)])
        o_win.append(normalise(acc))

    sig = jax.nn.sigmoid(ngt_ref[0])
    out_rows = []
    for h in range(N_KV_HEADS):
        def gate(r):
            a = (2 * h) * 3 + r
            b = (2 * h + 1) * 3 + r
            return jnp.concatenate([sig[a:a + 1, :], sig[b:b + 1, :]], axis=1)

        o = gate(0) * o_cmp[h] + gate(1) * o_sel[h] + gate(2) * o_win[h]
        out_rows += [o[:, :t], o[:, t:]]
    y_ref[0] = jnp.concatenate(out_rows, axis=0).T.astype(y_ref.dtype)


def _attn_call(qt, ngt, kcmp, vcmpt, ks, vst, kw, vwt):
    B, _, S = qt.shape
    t = T_ATT
    assert CH_ATT == t and WINDOW % t == 0 and S % t == 0 and (S // SEL_BLOCK) % 8 == 0
    assert SEL_BLOCK % CMP_STRIDE == 0 and CMP_LEN % CMP_STRIDE == 0 and CMP_LEN // CMP_STRIDE <= CMP_PAD
    nc = S // CMP_STRIDE
    n_sel = S // SEL_BLOCK
    per_b3 = lambda a: pl.BlockSpec((1,) + a.shape[1:], lambda b, i: (b, 0, 0))
    per_b4 = lambda a: pl.BlockSpec((1,) + a.shape[1:], lambda b, i: (b, 0, 0, 0))
    return pl.pallas_call(
        functools.partial(_attn_kernel, t=t, seq_len=S),
        grid=(B, S // t),
        in_specs=[pl.BlockSpec((1, MIX_W, t), lambda b, i: (b, 0, i)),
                  pl.BlockSpec((1, 16, t), lambda b, i: (b, 0, i)),
                  per_b3(kcmp), per_b3(vcmpt), per_b3(ks), per_b4(vst), per_b3(kw), per_b4(vwt)],
        out_specs=pl.BlockSpec((1, t, MIX_W), lambda b, i: (b, i, 0)),
        out_shape=jax.ShapeDtypeStruct((B, S, MIX_W), MXU_DTYPE),
        scratch_shapes=[
            pltpu.VMEM((N_KV_HEADS, n_sel, 2 * t), jnp.float32),
            pltpu.VMEM((t // LANES, CMP_PAD + nc, LANES), jnp.float32),
            pltpu.VMEM((N_KV_HEADS, S // t, t, 2 * t), jnp.float32),
            pltpu.VMEM((N_KV_HEADS, 2, t, 2 * t), jnp.float32),
        ],
        compiler_params=pltpu.CompilerParams(
            dimension_semantics=("arbitrary", "arbitrary"), vmem_limit_bytes=VMEM_LIMIT),
        name="nsa_attention",
    )(qt, ngt, kcmp, vcmpt, ks, vst, kw, vwt)


def _merge_mlp_kernel(x_ref, g1_ref, yacd_ref, yb_ref, wbr_ref, wg_ref, bg_ref, wo_ref,
                      g2_ref, w1_ref, w2_ref, o_ref):
    x = x_ref[...]
    xn = _rms_rows(x, g1_ref[...]).astype(MXU_DTYPE)
    ys = (yacd_ref[:, 0:MIX_W], yb_ref[...], yacd_ref[:, MIX_W:2 * MIX_W],
          yacd_ref[:, 2 * MIX_W:3 * MIX_W])
    mixed = jnp.zeros(x.shape, jnp.float32)
    for n, y in enumerate(ys):
        cols = slice(n * D_MODEL, (n + 1) * D_MODEL)
        proj = jnp.dot(y, wbr_ref[n], preferred_element_type=jnp.float32)
        logits = jnp.dot(xn, wg_ref[:, cols], preferred_element_type=jnp.float32) + bg_ref[:, cols]
        mixed = mixed + jax.nn.sigmoid(logits) * proj
    h = x + jnp.dot(mixed.astype(MXU_DTYPE), wo_ref[...], preferred_element_type=jnp.float32)

    hn = _rms_rows(h, g2_ref[...]).astype(MXU_DTYPE)
    acc = h
    for c in range(D_FF // D_MODEL):
        cols = slice(c * D_MODEL, (c + 1) * D_MODEL)
        hid = jnp.dot(hn, w1_ref[:, cols], preferred_element_type=jnp.float32)
        hid = jnp.square(jnp.maximum(hid, 0.0)).astype(MXU_DTYPE)
        acc = acc + jnp.dot(hid, w2_ref[cols, :], preferred_element_type=jnp.float32)
    o_ref[...] = acc


def _merge_mlp_call(x2, g1, yacd2, yb2, wbr, wg, bg, wo, g2, w1, w2):
    n, d = x2.shape
    tm = min(TM_MERGE, n)
    full = lambda a: pl.BlockSpec(a.shape, lambda i: (0,) * a.ndim)
    rows = lambda w: pl.BlockSpec((tm, w), lambda i: (i, 0))
    return pl.pallas_call(
        _merge_mlp_kernel,
        grid=(n // tm,),
        in_specs=[rows(d), full(g1), rows(3 * MIX_W), rows(MIX_W), full(wbr), full(wg), full(bg),
                  full(wo), full(g2), full(w1), full(w2)],
        out_specs=rows(d),
        out_shape=jax.ShapeDtypeStruct((n, d), jnp.float32),
        compiler_params=pltpu.CompilerParams(
            dimension_semantics=("arbitrary",), vmem_limit_bytes=VMEM_LIMIT),
        name="merge_mlp",
    )(x2, g1, yacd2, yb2, wbr, wg, bg, wo, g2, w1, w2)


def _split_w_in_kernel(w_ref, wa_ref, wng_ref, wb_ref):
    w = w_ref[0]
    wa_ref[...] = w[:, :_NG_SRC].astype(wa_ref.dtype)
    ng = w[:, _NG_SRC:_NG_SRC + LANES]
    lane = lax.broadcasted_iota(jnp.int32, ng.shape, 1)
    wng_ref[...] = jnp.where(lane < NSA_GATE_W, ng, 0.0).astype(wng_ref.dtype)
    wb_ref[...] = w[:, _NG_SRC + NSA_GATE_W:].astype(wb_ref.dtype)


def _split_w_in(w_in, l):
    _, d, cols = w_in.shape
    tail = cols - _NG_SRC - NSA_GATE_W
    rows = 256
    blk = lambda width: pl.BlockSpec((rows, width), lambda r: (r, 0))
    return pl.pallas_call(
        _split_w_in_kernel,
        grid=(d // rows,),
        in_specs=[pl.BlockSpec((1, rows, cols), lambda r: (l, r, 0))],
        out_specs=(blk(_NG_SRC), blk(LANES), blk(tail)),
        out_shape=(jax.ShapeDtypeStruct((d, _NG_SRC), MXU_DTYPE),
                   jax.ShapeDtypeStruct((d, LANES), MXU_DTYPE),
                   jax.ShapeDtypeStruct((d, tail), MXU_DTYPE)),
        compiler_params=pltpu.CompilerParams(
            dimension_semantics=("arbitrary",), vmem_limit_bytes=VMEM_LIMIT),
        name="split_w_in",
    )(w_in)


def _compress_weights(pe, w1, w2):
    eye = jnp.eye(N_KV_HEADS, dtype=w1.dtype)
    w1r = w1.reshape(CMP_LEN, HEAD_DIM, CMP_HIDDEN)

    def expand(w):
        return jnp.einsum('lde,hg->lhdge', w, eye).reshape(
            CMP_STRIDE * KV_W, N_KV_HEADS * CMP_HIDDEN)

    def expand_pe(p):
        return jnp.tile(p[:, None, :], (1, N_KV_HEADS, 1)).reshape(1, CMP_STRIDE * KV_W)

    w2b = jnp.einsum('ed,hg->hegd', w2, eye).reshape(N_KV_HEADS * CMP_HIDDEN, KV_W)
    return (expand_pe(pe[:CMP_STRIDE]), expand_pe(pe[CMP_STRIDE:]),
            expand(w1r[:CMP_STRIDE]), expand(w1r[CMP_STRIDE:]), w2b)


def _layer_mixers(x, l, p):
    row = lambda a: a.reshape(1, -1)
    kg = p["nsa_k_norm_g"][l]
    tile_heads = lambda g: jnp.tile(g, N_KV_HEADS).reshape(1, KV_W)
    bst = jnp.repeat(p["gmlp_bs"][l].T, MIX_W // GMLP_GROUPS, axis=1)
    qg = jnp.broadcast_to(jnp.tile(p["nsa_q_norm_g"][l], MIX_W // HEAD_DIM)[:, None], (MIX_W, LANES))
    (yacd, qt, kc, vc, ks, vst, kw, vwt, ngt) = _proj_call(
        x, row(p["norm1_g"][l]), *_split_w_in(p["w_in"], l),
        row(p["gmlp_ln_g"][l]), row(p["gmlp_ln_b"][l]), p["gmlp_ws"][l], bst, qg,
        tile_heads(kg[1]), tile_heads(kg[2]),
        p["conf_conv_w"][l], row(p["conf_conv_b"][l]), row(p["conf_ln_g"][l]), row(p["conf_ln_b"][l]),
        p["sconv_w"][l])
    cw = [_compress_weights(p["nsa_cmp_pe"][l, s], p["nsa_cmp_w1"][l, s], p["nsa_cmp_w2"][l, s])
          for s in range(2)]
    pet, peb, wt, wb, w2b = (jnp.stack([cw[0][n], cw[1][n]]) for n in range(5))
    kcmp, vcmpt = _compress_call(kc, vc, pet, peb, wt.astype(MXU_DTYPE), wb.astype(MXU_DTYPE),
                                 w2b.astype(MXU_DTYPE), tile_heads(kg[0]))
    yb = _attn_call(qt, ngt, kcmp, vcmpt, ks, vst, kw, vwt)
    return yacd, yb


def kernel(x, norm1_g, w_in, gmlp_ln_g, gmlp_ln_b, gmlp_ws, gmlp_bs, nsa_q_norm_g, nsa_k_norm_g,
           nsa_cmp_pe, nsa_cmp_w1, nsa_cmp_w2, conf_conv_w, conf_conv_b, conf_ln_g, conf_ln_b,
           sconv_w, w_branch, w_gate, b_gate, w_out, norm2_g, w_mlp1, w_mlp2):
    B, S, D = x.shape
    assert D == D_MODEL and S % min(TM_PROJ, S) == 0 and S % min(T_ATT, S) == 0 and S % CH_ATT == 0
    p = dict(norm1_g=norm1_g, w_in=w_in, gmlp_ln_g=gmlp_ln_g, gmlp_ln_b=gmlp_ln_b, gmlp_ws=gmlp_ws,
             gmlp_bs=gmlp_bs, nsa_q_norm_g=nsa_q_norm_g, nsa_k_norm_g=nsa_k_norm_g,
             nsa_cmp_pe=nsa_cmp_pe, nsa_cmp_w1=nsa_cmp_w1, nsa_cmp_w2=nsa_cmp_w2,
             conf_conv_w=conf_conv_w, conf_conv_b=conf_conv_b, conf_ln_g=conf_ln_g,
             conf_ln_b=conf_ln_b, sconv_w=sconv_w)
    for l in range(norm1_g.shape[0]):
        yacd, yb = _layer_mixers(x, l, p)
        x = _merge_mlp_call(
            x.reshape(B * S, D), norm1_g[l].reshape(1, D), yacd.reshape(B * S, 3 * MIX_W),
            yb.reshape(B * S, MIX_W), w_branch[l].astype(MXU_DTYPE), w_gate[l].astype(MXU_DTYPE),
            b_gate[l].reshape(1, -1), w_out[l].astype(MXU_DTYPE),
            norm2_g[l].reshape(1, D), w_mlp1[l].astype(MXU_DTYPE),
            w_mlp2[l].astype(MXU_DTYPE)).reshape(B, S, D)
    return x
```

```python
import functools

import numpy as np
import jax
import jax.numpy as jnp
from jax import lax
from jax.experimental import pallas as pl
from jax.experimental.pallas import tpu as pltpu

D_MODEL = 1024
MIX_W = 256
HEAD_DIM = 64
N_KV_HEADS = 2
Q_PER_KV = 2
KV_W = N_KV_HEADS * HEAD_DIM
GMLP_GROUPS = 4
GMLP_CHUNK = 128
CMP_LEN = 32
CMP_STRIDE = 16
CMP_HIDDEN = 128
SEL_BLOCK = 64
SEL_TOPK = 8
N_LOCAL_BLOCKS = 2
WINDOW = 512
CONF_KERNEL = 31
SCONV_KERNEL = 3
D_FF = 4 * D_MODEL
NSA_GATE_W = 12
NEG_INF = -1e30

LANES = 128
SUBLANES = 8
MXU_DTYPE = jnp.bfloat16
VMEM_LIMIT = 56 * 1024 * 1024

_C_GU, _C_GV, _C_Q = 0, 256, 512
_C_KC, _C_VC, _C_KS, _C_VS, _C_KW, _C_VW = 768, 896, 1024, 1152, 1280, 1408
_C_NG = 1536
_C_CA, _C_CB, _C_SB, _C_SC, _C_SH = 1664, 1920, 2176, 2432, 2688
IN_COLS_PAD = 2944
_NG_SRC = 1536

TM_PROJ = 512
TM_MERGE = 512
T_ATT = 256
CH_ATT = 256
LOG2E = 1.4426950408889634
V_AUG_ROWS = 16
CMP_PAD = 8
CONV_HALO = 32
SCONV_HALO = 8


def _f32(x):
    return x.astype(jnp.float32)


def _rms_rows(x, g, eps=1e-6):
    return x * lax.rsqrt(jnp.mean(x * x, axis=-1, keepdims=True) + eps) * g


def _layernorm_rows(x, g, b, eps=1e-5):
    mu = jnp.mean(x, axis=-1, keepdims=True)
    xc = x - mu
    var = jnp.mean(xc * xc, axis=-1, keepdims=True)
    return xc * lax.rsqrt(var + eps) * g + b


def _head_rms_rows(x, g, eps=1e-6):
    n_heads = x.shape[-1] // HEAD_DIM
    head = lax.broadcasted_iota(jnp.int32, x.shape, 1) >> 6
    xx = x * x
    scale = jnp.zeros_like(x)
    for h in range(n_heads):
        ms = jnp.sum(jnp.where(head == h, xx, 0.0), axis=-1, keepdims=True) * (1.0 / HEAD_DIM)
        scale = jnp.where(head == h, lax.rsqrt(ms + eps), scale)
    return x * scale * g


def _proj_kernel(x_ref, g1_ref, wa_ref, wng_ref, wb_ref, lng_ref, lnb_ref, ws_ref, bst_ref, qg_ref,
                 kg_ref, cw_ref, cbias_ref, clg_ref, clb_ref, sw_ref,
                 yacd_ref, qt_ref, kc_ref, vc_ref, ks_ref, vst_ref, kw_ref, vwt_ref, ngt_ref,
                 zbuf, cbuf, *, tm, ch):
    @pl.when(pl.program_id(1) == 0)
    def _():
        zbuf[...] = jnp.zeros((CONV_HALO, MIX_W), jnp.float32)
        cbuf[...] = jnp.zeros((SCONV_HALO, MIX_W), jnp.float32)

    x = x_ref[0]
    xn = _rms_rows(x, g1_ref[0]).astype(MXU_DTYPE)

    def proj(lo, width):
        if lo < _C_NG:
            w = wa_ref[:, lo:lo + width]
        elif lo == _C_NG:
            w = wng_ref[...]
        else:
            w = wb_ref[:, lo - _C_CA:lo - _C_CA + width]
        return jnp.dot(xn, w, preferred_element_type=jnp.float32)

    def causal_conv(buf, w, width, halo):
        offsets = [halo - (width - 1) + j for j in range(width)]
        out = jnp.zeros((tm, MIX_W), jnp.float32)
        for b in range(SUBLANES):
            taps = [j for j in range(width) if offsets[j] % SUBLANES == b]
            if not taps:
                continue
            rows = tm if b == 0 else tm + SUBLANES
            part = jnp.zeros((rows, MIX_W), jnp.float32)
            for j in taps:
                part = part + w[j:j + 1] * buf[offsets[j] - b:offsets[j] - b + rows]
            out = out + part[b:b + tm]
            yield out


    z = proj(_C_CA, MIX_W) * jax.nn.sigmoid(proj(_C_CB, MIX_W))
    conv = causal_conv(jnp.concatenate([zbuf[...], z], axis=0), cw_ref[0], CONF_KERNEL, CONV_HALO)
    zbuf[...] = z[tm - CONV_HALO:tm]

    def emit_q():
        qt = proj(_C_Q, MIX_W).T
        gq = jnp.concatenate([qg_ref[0]] * (tm // LANES), axis=1)
        pieces = []
        for h in range(MIX_W // HEAD_DIM):
            blk = qt[h * HEAD_DIM:(h + 1) * HEAD_DIM]
            ms = jnp.sum(blk * blk, axis=0, keepdims=True) * (1.0 / HEAD_DIM)
            pieces.append(blk * lax.rsqrt(ms + 1e-6))
        qt_ref[0] = (jnp.concatenate(pieces, axis=0) * gq
                     * (HEAD_DIM ** -0.5 * LOG2E)).astype(qt_ref.dtype)

    def emit_raw(dst_ref, col):
        dst_ref[0] = proj(col, KV_W)

    def emit_key(dst_ref, col, branch):
        dst_ref[0] = _head_rms_rows(proj(col, KV_W), kg_ref[0, branch:branch + 1, :]).astype(dst_ref.dtype)

    def emit_value_t(dst_ref, col):
        vt = proj(col, KV_W).T
        for c in range(tm // ch):
            dst_ref[0, c] = vt[:, c * ch:(c + 1) * ch].astype(dst_ref.dtype)

    def emit_gates():
        ngt_ref[0] = proj(_C_NG, LANES).T[0:16, :]

    held = {}

    def hold(name, col):
        held[name] = proj(col, MIX_W)

    slices = [
        [lambda: emit_raw(kc_ref, _C_KC), lambda: emit_raw(vc_ref, _C_VC)],
        [lambda: emit_key(ks_ref, _C_KS, 1), lambda: emit_key(kw_ref, _C_KW, 2)],
        [lambda: emit_value_t(vst_ref, _C_VS), lambda: emit_value_t(vwt_ref, _C_VW)],
        [emit_q],
        [emit_gates, lambda: hold("sb", _C_SB)],
        [lambda: hold("sc", _C_SC), lambda: hold("sh", _C_SH)],
        [lambda: hold("gu", _C_GU)],
        [lambda: hold("gv", _C_GV)],
    ]
    acc = None
    for tasks in slices:
        for task in tasks:
            task()
        acc = next(conv, acc)
    for acc in conv:
        pass
    yacd_ref[0, :, MIX_W:2 * MIX_W] = jax.nn.silu(_layernorm_rows(
        acc + cbias_ref[0], clg_ref[0], clb_ref[0])).astype(yacd_ref.dtype)

    sc = held["sc"] * held["sh"]
    for acc in causal_conv(jnp.concatenate([cbuf[...], sc], axis=0), sw_ref[0], SCONV_KERNEL, SCONV_HALO):
        pass
    cbuf[...] = sc[tm - SCONV_HALO:tm]
    yacd_ref[0, :, 2 * MIX_W:3 * MIX_W] = (held["sb"] * acc).astype(yacd_ref.dtype)

    u = jax.nn.gelu(held["gu"])
    v = jax.nn.gelu(held["gv"])
    v = _layernorm_rows(v, lng_ref[0], lnb_ref[0]).astype(MXU_DTYPE)
    tri = (lax.broadcasted_iota(jnp.int32, (GMLP_CHUNK, GMLP_CHUNK), 0)
           >= lax.broadcasted_iota(jnp.int32, (GMLP_CHUNK, GMLP_CHUNK), 1))
    grp = lax.broadcasted_iota(jnp.int32, (GMLP_CHUNK, MIX_W), 1) >> 6
    wsm = [jnp.where(tri, ws_ref[0, g], 0.0).astype(MXU_DTYPE) for g in range(GMLP_GROUPS)]
    for c in range(tm // GMLP_CHUNK):
        rows = slice(c * GMLP_CHUNK, (c + 1) * GMLP_CHUNK)
        vch = v[rows]
        mixed = jnp.zeros((GMLP_CHUNK, MIX_W), jnp.float32)
        for g in range(GMLP_GROUPS):
            r = jnp.dot(wsm[g], vch, preferred_element_type=jnp.float32)
            mixed = jnp.where(grp == g, r, mixed)
        yacd_ref[0, rows, 0:MIX_W] = (u[rows] * (mixed + bst_ref[0])).astype(yacd_ref.dtype)


def _layer_spec(a, l):
    return pl.BlockSpec((1,) + a.shape[1:], lambda *_: (l,) + (0,) * (a.ndim - 1))


def _proj_call(x, l, wa, wng, wb, g1, lng, lnb, ws, bst, qg, kg, cw, cbias, clg, clb, sw):
    B, S, D = x.shape
    tm, ch = min(TM_PROJ, S), CH_ATT
    full = lambda a: pl.BlockSpec(a.shape, lambda b, i: (0,) * a.ndim)
    seq = lambda width: pl.BlockSpec((1, tm, width), lambda b, i: (b, i, 0))
    stacked = (lng, lnb, ws, bst, qg, kg, cw, cbias, clg, clb, sw)
    ins = (g1, wa, wng, wb) + stacked
    in_specs = ([seq(D), _layer_spec(g1, l), full(wa), full(wng), full(wb)]
                + [_layer_spec(a, l) for a in stacked])
    out_shape = (
        jax.ShapeDtypeStruct((B, S, 3 * MIX_W), MXU_DTYPE),
        jax.ShapeDtypeStruct((B, MIX_W, S), MXU_DTYPE),
        jax.ShapeDtypeStruct((B, S, KV_W), jnp.float32),
        jax.ShapeDtypeStruct((B, S, KV_W), jnp.float32),
        jax.ShapeDtypeStruct((B, S, KV_W), MXU_DTYPE),
        jax.ShapeDtypeStruct((B, S // ch, KV_W, ch), MXU_DTYPE),
        jax.ShapeDtypeStruct((B, S, KV_W), MXU_DTYPE),
        jax.ShapeDtypeStruct((B, S // ch, KV_W, ch), MXU_DTYPE),
        jax.ShapeDtypeStruct((B, 16, S), jnp.float32),
    )
    vt_spec = pl.BlockSpec((1, tm // ch, KV_W, ch), lambda b, i: (b, i, 0, 0))
    out_specs = (
        seq(3 * MIX_W),
        pl.BlockSpec((1, MIX_W, tm), lambda b, i: (b, 0, i)),
        seq(KV_W), seq(KV_W), seq(KV_W), vt_spec, seq(KV_W), vt_spec,
        pl.BlockSpec((1, 16, tm), lambda b, i: (b, 0, i)),
    )
    return pl.pallas_call(
        functools.partial(_proj_kernel, tm=tm, ch=ch),
        grid=(B, S // tm),
        in_specs=in_specs,
        out_specs=out_specs,
        out_shape=out_shape,
        scratch_shapes=[pltpu.VMEM((CONV_HALO, MIX_W), jnp.float32),
                        pltpu.VMEM((SCONV_HALO, MIX_W), jnp.float32)],
        compiler_params=pltpu.CompilerParams(
            dimension_semantics=("arbitrary", "arbitrary"), vmem_limit_bytes=VMEM_LIMIT),
        name="proj_mixers",
    )(x, *ins)


def _compress_kernel(kc_ref, vc_ref, pet_ref, peb_ref, wt_ref, wb_ref, w2_ref, kg_ref,
                     kcmp_ref, vcmpt_ref):
    nc = kc_ref.shape[1] // CMP_STRIDE

    def compress(src_ref, idx):
        chunks = jnp.concatenate(
            [src_ref[0, pl.ds(l, nc, stride=CMP_STRIDE), :] for l in range(CMP_STRIDE)], axis=1)
        a1 = jnp.dot((chunks + pet_ref[0, idx]).astype(MXU_DTYPE), wt_ref[0, idx],
                     preferred_element_type=jnp.float32)
        a2 = jnp.dot((chunks + peb_ref[0, idx]).astype(MXU_DTYPE), wb_ref[0, idx],
                     preferred_element_type=jnp.float32)
        hid = jax.nn.gelu(a1 + pltpu.roll(a2, shift=nc - 1, axis=0))
        return jnp.dot(hid.astype(MXU_DTYPE), w2_ref[0, idx], preferred_element_type=jnp.float32)

    kcmp_ref[0] = _head_rms_rows(compress(kc_ref, 0), kg_ref[0, 0:1, :]).astype(kcmp_ref.dtype)
    vcmpt_ref[0] = compress(vc_ref, 1).T.astype(vcmpt_ref.dtype)


def _compress_call(kc, vc, l, pet, peb, wt, wb, w2, kg):
    B, S, _ = kc.shape
    nc = S // CMP_STRIDE
    per_b = lambda r, c: pl.BlockSpec((1, r, c), lambda b: (b, 0, 0))
    ins = (pet, peb, wt, wb, w2, kg)
    return pl.pallas_call(
        _compress_kernel,
        grid=(B,),
        in_specs=[per_b(S, KV_W)] * 2 + [_layer_spec(a, l) for a in ins],
        out_specs=(per_b(nc, KV_W), per_b(KV_W, nc)),
        out_shape=(jax.ShapeDtypeStruct((B, nc, KV_W), MXU_DTYPE),
                   jax.ShapeDtypeStruct((B, KV_W, nc), MXU_DTYPE)),
        compiler_params=pltpu.CompilerParams(
            dimension_semantics=("arbitrary",), vmem_limit_bytes=VMEM_LIMIT),
        name="nsa_compress",
    )(kc, vc, *ins)


def _attn_kernel(qt_ref, ngt_ref, kcmp_ref, vcmpt_ref, ks_ref, vst_ref, kw_ref, vwt_ref,
                 y_ref, selb_scr, pcmp_scr, ssel_scr, stail_scr, *, t, seq_len):
    i = pl.program_id(1)
    nc = seq_len // CMP_STRIDE
    n_cmp = nc - 1
    n_sel = seq_len // SEL_BLOCK
    k_top = min(SEL_TOPK, n_sel)
    bpc = t // SEL_BLOCK
    cmp_per_sel = SEL_BLOCK // CMP_STRIDE
    nw = WINDOW // t
    t2 = 2 * t
    f32 = jnp.float32

    def both(a):
        return jnp.concatenate([a, a], axis=1)

    def keep_bias(cond):
        return jnp.where(cond, 0.0, NEG_INF).astype(f32)

    qt = qt_ref[0]
    qpads = []
    for h in range(N_KV_HEADS):
        qh = jnp.concatenate([qt[(2 * h) * HEAD_DIM:(2 * h + 1) * HEAD_DIM],
                              qt[(2 * h + 1) * HEAD_DIM:(2 * h + 2) * HEAD_DIM]], axis=1)
        zero = jnp.zeros_like(qh)
        qpads.append(jnp.concatenate([qh, zero] if h == 0 else [zero, qh], axis=0))

    n_i = lax.broadcasted_iota(jnp.int32, (nc, t), 0)
    qpos_c = i * t + lax.broadcasted_iota(jnp.int32, (nc, t), 1)
    cmp_ok = (n_i * CMP_STRIDE + (CMP_LEN - 1) <= qpos_c) & (n_i < n_cmp)
    cmp_bias = both(keep_bias(cmp_ok))
    cmp_keep = both(jnp.where(cmp_ok, 1.0, 0.0).astype(f32))

    blk = lax.broadcasted_iota(jnp.int32, (n_sel, t), 0)
    cur = (i * t + lax.broadcasted_iota(jnp.int32, (n_sel, t), 1)) >> 6
    dist = cur - blk
    causal = dist >= 0
    forced = (blk == 0) | (causal & (dist < N_LOCAL_BLOCKS))
    blk8 = lax.broadcasted_iota(jnp.int32, (8, t), 0)

    kcmp = kcmp_ref[0]
    pcmp_scr[:, 0:CMP_PAD, :] = jnp.zeros((t // LANES, CMP_PAD, LANES), f32)
    o_cmp = []
    for h in range(N_KV_HEADS):
        sm = jnp.dot(kcmp, qpads[h], preferred_element_type=f32) + cmp_bias
        e = jnp.exp2(sm - jnp.max(sm, axis=0, keepdims=True)) * cmp_keep
        den = jnp.sum(e, axis=0, keepdims=True)
        p_cmp = e * (1.0 / jnp.where(den > 0.0, den, 1.0))
        o_cmp.append(jnp.dot(vcmpt_ref[0, h * HEAD_DIM:(h + 1) * HEAD_DIM, :],
                             p_cmp.astype(MXU_DTYPE), preferred_element_type=f32))

        p_pair = p_cmp[:, :t] + p_cmp[:, t:]
        slabs = []
        for lb in range(t // LANES):
            pcmp_scr[lb, CMP_PAD:CMP_PAD + nc, :] = p_pair[:, lb * LANES:(lb + 1) * LANES]
            slabs.append(functools.reduce(lambda a, b: a + b, [
                pcmp_scr[lb, pl.ds(CMP_PAD + off, n_sel, stride=cmp_per_sel), :]
                for off in range(1 - CMP_LEN // CMP_STRIDE, cmp_per_sel)]))
        p_slc = jnp.concatenate(slabs, axis=1)
        score = jnp.where(forced, jnp.inf, jnp.where(causal, p_slc, -jnp.inf))
        groups = [score[8 * g:8 * g + 8] for g in range(n_sel // 8)]
        rank = [jnp.zeros((8, t), f32) for _ in groups]
        for r in range(n_sel):
            row = score[r:r + 1, :]
            for g, sg in enumerate(groups):
                if 8 * g > r:
                    beats = row >= sg
                elif 8 * g + 7 <= r:
                    beats = row > sg
                else:
                    beats = (row > sg) | ((row == sg) & (blk8 > r - 8 * g))
                rank[g] = rank[g] + jnp.where(beats, 1.0, 0.0)
        selected = (jnp.concatenate(rank, axis=0) < k_top) & causal
        selb_scr[h] = both(keep_bias(selected))

    r_i = lax.broadcasted_iota(jnp.int32, (t, t), 0)
    q_i = lax.broadcasted_iota(jnp.int32, (t, t), 1)
    diag_bias = both(keep_bias(r_i <= q_i))
    ones_rows = jnp.where(lax.broadcasted_iota(jnp.int32, (V_AUG_ROWS, t), 0) == 0,
                          1.0, 0.0).astype(MXU_DTYPE)

    def k_chunk(k_ref, c):
        return k_ref[0, pl.ds(pl.multiple_of(c * t, t), t), :]

    def v_aug(vt_ref, c, h):
        return jnp.concatenate([vt_ref[0, c, h * HEAD_DIM:(h + 1) * HEAD_DIM, :], ones_rows], axis=0)

    def colmax8(s):
        return jnp.max(s.reshape(t // 8, 8, t2), axis=0)

    def weighted_values(s, m, vaug):
        p = jnp.exp2(s - m).astype(MXU_DTYPE)
        return jnp.dot(vaug, p, preferred_element_type=f32)

    def normalise(acc):
        return acc[0:HEAD_DIM] * (1.0 / acc[HEAD_DIM:HEAD_DIM + 1])

    heads = range(N_KV_HEADS)
    neg8 = jnp.full((8, t2), NEG_INF, f32)

    def sel_scores(c, k, h, extra):
        s = jnp.dot(k, qpads[h], preferred_element_type=f32)
        parts = []
        for r in range(bpc):
            rows = slice(r * SEL_BLOCK, (r + 1) * SEL_BLOCK)
            bias = selb_scr[h, pl.ds(c * bpc + r, 1), :]
            if extra is not None:
                bias = bias + (extra[rows] if extra.shape[0] == t else extra)
            parts.append(s[rows] + bias)
        return jnp.concatenate(parts, axis=0)

    def sel_pass1(pair, pm):
        pm = list(pm)
        for u in range(2):
            c = 2 * pair + u
            k = k_chunk(ks_ref, c)
            for h in heads:
                s = sel_scores(c, k, h, None)
                ssel_scr[h, c] = s
                pm[h] = jnp.maximum(pm[h], colmax8(s))
        return tuple(pm)

    n_pairs = i >> 1
    pm = lax.fori_loop(0, n_pairs, sel_pass1, (neg8, neg8))
    c_left = jnp.maximum(i - 1, 0)
    left_bias = jnp.full((1, t2), jnp.where((i & 1) == 1, 0.0, NEG_INF), f32)
    tail = ((c_left, left_bias), (i, diag_bias))
    pm = list(pm)
    for u, (c, extra) in enumerate(tail):
        k = k_chunk(ks_ref, c)
        for h in heads:
            s = sel_scores(c, k, h, extra)
            stail_scr[h, u] = s
            pm[h] = jnp.maximum(pm[h], colmax8(s))
    m_sel = [jnp.max(pm[h], axis=0, keepdims=True) for h in heads]

    def sel_pass2(pair, accs):
        accs = list(accs)
        for u in range(2):
            c = 2 * pair + u
            for h in heads:
                accs[h] = accs[h] + weighted_values(ssel_scr[h, c], m_sel[h], v_aug(vst_ref, c, h))
        return tuple(accs)

    zero_acc = jnp.zeros((HEAD_DIM + V_AUG_ROWS, t2), f32)
    accs = list(lax.fori_loop(0, n_pairs, sel_pass2, (zero_acc, zero_acc)))
    for u, (c, _) in enumerate(tail):
        for h in heads:
            accs[h] = accs[h] + weighted_values(stail_scr[h, u], m_sel[h], v_aug(vst_ref, c, h))
    o_sel = [normalise(accs[h]) for h in heads]

    first_bias = both(keep_bias(r_i > q_i))
    kws, biases, cis = [], [], []
    for cc in range(nw + 1):
        ci = i - nw + cc
        cic = jnp.maximum(ci, 0)
        exists = jnp.where(ci >= 0, 0.0, NEG_INF).astype(f32)
        if cc == nw:
            bias = diag_bias
        elif cc == 0:
            bias = first_bias + exists
        else:
            bias = jnp.full((1, t2), exists, f32)
        kws.append(k_chunk(kw_ref, cic))
        biases.append(bias)
        cis.append(cic)
    o_win = []
    for h in heads:
        ss = [jnp.dot(k, qpads[h], preferred_element_type=f32) + b for k, b in zip(kws, biases)]
        m = jnp.max(functools.reduce(jnp.maximum, [colmax8(s) for s in ss]), axis=0, keepdims=True)
        acc = functools.reduce(lambda a, b: a + b, [
            weighted_values(s, m, v_aug(vwt_ref, cic, h)) for cic, s in zip(cis, ss)])
        o_win.append(normalise(acc))

    sig = jax.nn.sigmoid(ngt_ref[0])
    out_rows = []
    for h in range(N_KV_HEADS):
        def gate(r):
            a = (2 * h) * 3 + r
            b = (2 * h + 1) * 3 + r
            return jnp.concatenate([sig[a:a + 1, :], sig[b:b + 1, :]], axis=1)

        o = gate(0) * o_cmp[h] + gate(1) * o_sel[h] + gate(2) * o_win[h]
        out_rows += [o[:, :t], o[:, t:]]
    y_ref[0] = jnp.concatenate(out_rows, axis=0).T.astype(y_ref.dtype)


def _attn_call(qt, ngt, kcmp, vcmpt, ks, vst, kw, vwt):
    B, _, S = qt.shape
    t = T_ATT
    assert CH_ATT == t and WINDOW % t == 0 and S % t == 0 and (S // SEL_BLOCK) % 8 == 0
    assert SEL_BLOCK % CMP_STRIDE == 0 and CMP_LEN % CMP_STRIDE == 0 and CMP_LEN // CMP_STRIDE <= CMP_PAD
    nc = S // CMP_STRIDE
    n_sel = S // SEL_BLOCK
    per_b3 = lambda a: pl.BlockSpec((1,) + a.shape[1:], lambda b, i: (b, 0, 0))
    per_b4 = lambda a: pl.BlockSpec((1,) + a.shape[1:], lambda b, i: (b, 0, 0, 0))
    return pl.pallas_call(
        functools.partial(_attn_kernel, t=t, seq_len=S),
        grid=(B, S // t),
        in_specs=[pl.BlockSpec((1, MIX_W, t), lambda b, i: (b, 0, i)),
                  pl.BlockSpec((1, 16, t), lambda b, i: (b, 0, i)),
                  per_b3(kcmp), per_b3(vcmpt), per_b3(ks), per_b4(vst), per_b3(kw), per_b4(vwt)],
        out_specs=pl.BlockSpec((1, t, MIX_W), lambda b, i: (b, i, 0)),
        out_shape=jax.ShapeDtypeStruct((B, S, MIX_W), MXU_DTYPE),
        scratch_shapes=[
            pltpu.VMEM((N_KV_HEADS, n_sel, 2 * t), jnp.float32),
            pltpu.VMEM((t // LANES, CMP_PAD + nc, LANES), jnp.float32),
            pltpu.VMEM((N_KV_HEADS, S // t, t, 2 * t), jnp.float32),
            pltpu.VMEM((N_KV_HEADS, 2, t, 2 * t), jnp.float32),
        ],
        compiler_params=pltpu.CompilerParams(
            dimension_semantics=("arbitrary", "arbitrary"), vmem_limit_bytes=VMEM_LIMIT),
        name="nsa_attention",
    )(qt, ngt, kcmp, vcmpt, ks, vst, kw, vwt)


def _merge_mlp_kernel(x_ref, yacd_ref, yb_ref, g1_ref, wbr_ref, wg_ref, bg_ref, wo_ref,
                      g2_ref, w1_ref, w2_ref, o_ref):
    x = x_ref[0]
    xn = _rms_rows(x, g1_ref[0]).astype(MXU_DTYPE)
    ys = (yacd_ref[0, :, 0:MIX_W], yb_ref[0], yacd_ref[0, :, MIX_W:2 * MIX_W],
          yacd_ref[0, :, 2 * MIX_W:3 * MIX_W])
    mixed = jnp.zeros(x.shape, jnp.float32)
    for n, y in enumerate(ys):
        cols = slice(n * D_MODEL, (n + 1) * D_MODEL)
        proj = jnp.dot(y, wbr_ref[0, n], preferred_element_type=jnp.float32)
        logits = jnp.dot(xn, wg_ref[0, :, cols], preferred_element_type=jnp.float32) + bg_ref[0, :, cols]
        mixed = mixed + jax.nn.sigmoid(logits) * proj
    h = x + jnp.dot(mixed.astype(MXU_DTYPE), wo_ref[0], preferred_element_type=jnp.float32)

    hn = _rms_rows(h, g2_ref[0]).astype(MXU_DTYPE)
    acc = h
    for c in range(D_FF // D_MODEL):
        cols = slice(c * D_MODEL, (c + 1) * D_MODEL)
        hid = jnp.dot(hn, w1_ref[0, :, cols], preferred_element_type=jnp.float32)
        hid = jnp.square(jnp.maximum(hid, 0.0)).astype(MXU_DTYPE)
        acc = acc + jnp.dot(hid, w2_ref[0, cols, :], preferred_element_type=jnp.float32)
    o_ref[0] = acc


def _merge_mlp_call(x, l, yacd, yb, g1, wbr, wg, bg, wo, g2, w1, w2):
    B, S, d = x.shape
    tm = min(TM_MERGE, S)
    rows = lambda w: pl.BlockSpec((1, tm, w), lambda b, i: (b, i, 0))
    stacked = (g1, wbr, wg, bg, wo, g2, w1, w2)
    return pl.pallas_call(
        _merge_mlp_kernel,
        grid=(B, S // tm),
        in_specs=[rows(d), rows(3 * MIX_W), rows(MIX_W)] + [_layer_spec(a, l) for a in stacked],
        out_specs=rows(d),
        out_shape=jax.ShapeDtypeStruct((B, S, d), jnp.float32),
        compiler_params=pltpu.CompilerParams(
            dimension_semantics=("arbitrary", "arbitrary"), vmem_limit_bytes=VMEM_LIMIT),
        name="merge_mlp",
    )(x, yacd, yb, *stacked)


def _split_w_in_kernel(w_ref, wa_ref, wng_ref, wb_ref):
    w = w_ref[0]
    wa_ref[...] = w[:, :_NG_SRC].astype(wa_ref.dtype)
    ng = w[:, _NG_SRC:_NG_SRC + LANES]
    lane = lax.broadcasted_iota(jnp.int32, ng.shape, 1)
    wng_ref[...] = jnp.where(lane < NSA_GATE_W, ng, 0.0).astype(wng_ref.dtype)
    wb_ref[...] = w[:, _NG_SRC + NSA_GATE_W:].astype(wb_ref.dtype)


def _split_w_in(w_in, l):
    _, d, cols = w_in.shape
    tail = cols - _NG_SRC - NSA_GATE_W
    rows = 256
    blk = lambda width: pl.BlockSpec((rows, width), lambda r: (r, 0))
    return pl.pallas_call(
        _split_w_in_kernel,
        grid=(d // rows,),
        in_specs=[pl.BlockSpec((1, rows, cols), lambda r: (l, r, 0))],
        out_specs=(blk(_NG_SRC), blk(LANES), blk(tail)),
        out_shape=(jax.ShapeDtypeStruct((d, _NG_SRC), MXU_DTYPE),
                   jax.ShapeDtypeStruct((d, LANES), MXU_DTYPE),
                   jax.ShapeDtypeStruct((d, tail), MXU_DTYPE)),
        compiler_params=pltpu.CompilerParams(
            dimension_semantics=("arbitrary",), vmem_limit_bytes=VMEM_LIMIT),
        name="split_w_in",
    )(w_in)


def _compress_weights(pe, w1, w2):
    L = pe.shape[0]
    eye = jnp.eye(N_KV_HEADS, dtype=w1.dtype)
    w1r = w1.reshape(L, 2, CMP_LEN, HEAD_DIM, CMP_HIDDEN)

    def expand(w):
        return jnp.einsum('abtde,hg->abthdge', w, eye).reshape(
            L, 2, CMP_STRIDE * KV_W, N_KV_HEADS * CMP_HIDDEN).astype(MXU_DTYPE)

    def expand_pe(p):
        return jnp.tile(p[:, :, :, None, :], (1, 1, 1, N_KV_HEADS, 1)).reshape(
            L, 2, 1, CMP_STRIDE * KV_W)

    w2b = jnp.einsum('abed,hg->abhegd', w2, eye).reshape(
        L, 2, N_KV_HEADS * CMP_HIDDEN, KV_W).astype(MXU_DTYPE)
    return (expand_pe(pe[:, :, :CMP_STRIDE]), expand_pe(pe[:, :, CMP_STRIDE:]),
            expand(w1r[:, :, :CMP_STRIDE]), expand(w1r[:, :, CMP_STRIDE:]), w2b)


def _layer_mixers(x, l, p):
    (yacd, qt, kc, vc, ks, vst, kw, vwt, ngt) = _proj_call(
        x, l, *_split_w_in(p["w_in"], l), p["norm1_g"], p["gmlp_ln_g"], p["gmlp_ln_b"], p["gmlp_ws"],
        p["bst"], p["qg"], p["kg"], p["conf_conv_w"], p["conf_conv_b"], p["conf_ln_g"],
        p["conf_ln_b"], p["sconv_w"])
    kcmp, vcmpt = _compress_call(kc, vc, l, *p["cmp_w"], p["kg"])
    yb = _attn_call(qt, ngt, kcmp, vcmpt, ks, vst, kw, vwt)
    return yacd, yb


def _prepare(p):
    q = dict(p)
    q["bst"] = jnp.repeat(jnp.swapaxes(p["gmlp_bs"], 1, 2), MIX_W // GMLP_GROUPS, axis=2)
    qg = jnp.tile(p["nsa_q_norm_g"], (1, MIX_W // HEAD_DIM))
    q["qg"] = jnp.broadcast_to(qg[:, :, None], qg.shape + (LANES,))
    q["kg"] = jnp.tile(p["nsa_k_norm_g"], (1, 1, N_KV_HEADS))
    q["cmp_w"] = _compress_weights(p["nsa_cmp_pe"], p["nsa_cmp_w1"], p["nsa_cmp_w2"])
    for name in ("w_branch", "w_gate", "w_out", "w_mlp1", "w_mlp2"):
        q[name] = p[name].astype(MXU_DTYPE)
    for name in ("norm1_g", "gmlp_ln_g", "gmlp_ln_b", "conf_conv_b", "conf_ln_g", "conf_ln_b",
                 "b_gate", "norm2_g"):
        q[name] = p[name][:, None, :]
    return q


def kernel(x, norm1_g, w_in, gmlp_ln_g, gmlp_ln_b, gmlp_ws, gmlp_bs, nsa_q_norm_g, nsa_k_norm_g,
           nsa_cmp_pe, nsa_cmp_w1, nsa_cmp_w2, conf_conv_w, conf_conv_b, conf_ln_g, conf_ln_b,
           sconv_w, w_branch, w_gate, b_gate, w_out, norm2_g, w_mlp1, w_mlp2):
    B, S, D = x.shape
    assert D == D_MODEL and S % min(TM_PROJ, S) == 0 and S % min(T_ATT, S) == 0 and S % CH_ATT == 0
    p = _prepare(dict(
        norm1_g=norm1_g, w_in=w_in, gmlp_ln_g=gmlp_ln_g, gmlp_ln_b=gmlp_ln_b, gmlp_ws=gmlp_ws,
        gmlp_bs=gmlp_bs, nsa_q_norm_g=nsa_q_norm_g, nsa_k_norm_g=nsa_k_norm_g,
        nsa_cmp_pe=nsa_cmp_pe, nsa_cmp_w1=nsa_cmp_w1, nsa_cmp_w2=nsa_cmp_w2,
        conf_conv_w=conf_conv_w, conf_conv_b=conf_conv_b, conf_ln_g=conf_ln_g, conf_ln_b=conf_ln_b,
        sconv_w=sconv_w, w_branch=w_branch, w_gate=w_gate, b_gate=b_gate, w_out=w_out,
        norm2_g=norm2_g, w_mlp1=w_mlp1, w_mlp2=w_mlp2))
    for l in range(norm1_g.shape[0]):
        yacd, yb = _layer_mixers(x, l, p)
        x = _merge_mlp_call(x, l, yacd, yb, p["norm1_g"], p["w_branch"], p["w_gate"], p["b_gate"],
                            p["w_out"], p["norm2_g"], p["w_mlp1"], p["w_mlp2"])
    return x
```

```python
import functools

import jax
import jax.numpy as jnp
from jax import lax
from jax.experimental import pallas as pl
from jax.experimental.pallas import tpu as pltpu

D_MODEL = 1024
MIX_W = 256
HEAD_DIM = 64
N_KV_HEADS = 2
Q_PER_KV = 2
KV_W = N_KV_HEADS * HEAD_DIM
GMLP_GROUPS = 4
GMLP_CHUNK = 128
CMP_LEN = 32
CMP_STRIDE = 16
CMP_HIDDEN = 128
SEL_BLOCK = 64
SEL_TOPK = 8
N_LOCAL_BLOCKS = 2
WINDOW = 512
CONF_KERNEL = 31
SCONV_KERNEL = 3
D_FF = 4 * D_MODEL
NSA_GATE_W = 12
NEG_INF = -1e30

LANES = 128
SUBLANES = 8
MXU_DTYPE = jnp.bfloat16
VMEM_LIMIT = 56 * 1024 * 1024

_C_GU, _C_GV, _C_Q = 0, 256, 512
_C_KC, _C_VC, _C_KS, _C_VS, _C_KW, _C_VW = 768, 896, 1024, 1152, 1280, 1408
_C_NG = 1536
_C_CA, _C_CB, _C_SB, _C_SC, _C_SH = 1664, 1920, 2176, 2432, 2688
_NG_SRC = 1536

TM_PROJ = 512
TM_MERGE = 512
T_ATT = 256
CH_ATT = 256
LOG2E = 1.4426950408889634
V_AUG_ROWS = 16
CMP_PAD = 8
CONV_HALO = 32
SCONV_HALO = 8


def _rms_rows(x, g, eps=1e-6):
    return x * lax.rsqrt(jnp.mean(x * x, axis=-1, keepdims=True) + eps) * g


def _layernorm_rows(x, g, b, eps=1e-5):
    mu = jnp.mean(x, axis=-1, keepdims=True)
    xc = x - mu
    var = jnp.mean(xc * xc, axis=-1, keepdims=True)
    return xc * lax.rsqrt(var + eps) * g + b


def _head_rms_rows(x, g, eps=1e-6):
    n_heads = x.shape[-1] // HEAD_DIM
    head = lax.broadcasted_iota(jnp.int32, x.shape, 1) >> 6
    xx = x * x
    scale = jnp.zeros_like(x)
    for h in range(n_heads):
        ms = jnp.sum(jnp.where(head == h, xx, 0.0), axis=-1, keepdims=True) * (1.0 / HEAD_DIM)
        scale = jnp.where(head == h, lax.rsqrt(ms + eps), scale)
    return x * scale * g


def _layer_spec(a, l):
    return pl.BlockSpec((1,) + a.shape[1:], lambda *_: (l,) + (0,) * (a.ndim - 1))


def _proj_kernel(x_ref, g1_ref, wa_ref, wng_ref, wb_ref, lng_ref, lnb_ref, ws_ref, bst_ref, qg_ref,
                 kg_ref, cw_ref, cbias_ref, clg_ref, clb_ref, sw_ref,
                 yacd_ref, qt_ref, kc_ref, vc_ref, ks_ref, vst_ref, kw_ref, vwt_ref, ngt_ref,
                 zbuf, cbuf, *, tm, ch):
    @pl.when(pl.program_id(1) == 0)
    def _():
        zbuf[...] = jnp.zeros((CONV_HALO, MIX_W), jnp.float32)
        cbuf[...] = jnp.zeros((SCONV_HALO, MIX_W), jnp.float32)

    x = x_ref[0]
    xn = _rms_rows(x, g1_ref[0]).astype(MXU_DTYPE)

    def proj(lo, width):
        if lo < _C_NG:
            w = wa_ref[:, lo:lo + width]
        elif lo == _C_NG:
            w = wng_ref[...]
        else:
            w = wb_ref[:, lo - _C_CA:lo - _C_CA + width]
        return jnp.dot(xn, w, preferred_element_type=jnp.float32)

    def causal_conv(buf, w, width, halo):
        offsets = [halo - (width - 1) + j for j in range(width)]
        out = jnp.zeros((tm, MIX_W), jnp.float32)
        for b in range(SUBLANES):
            taps = [j for j in range(width) if offsets[j] % SUBLANES == b]
            if not taps:
                continue
            rows = tm if b == 0 else tm + SUBLANES
            part = jnp.zeros((rows, MIX_W), jnp.float32)
            for j in taps:
                part = part + w[j:j + 1] * buf[offsets[j] - b:offsets[j] - b + rows]
            out = out + part[b:b + tm]
            yield out


    z = proj(_C_CA, MIX_W) * jax.nn.sigmoid(proj(_C_CB, MIX_W))
    conv = causal_conv(jnp.concatenate([zbuf[...], z], axis=0), cw_ref[0], CONF_KERNEL, CONV_HALO)
    zbuf[...] = z[tm - CONV_HALO:tm]

    def emit_q():
        qt = proj(_C_Q, MIX_W).T
        gq = jnp.concatenate([qg_ref[0]] * (tm // LANES), axis=1)
        pieces = []
        for h in range(MIX_W // HEAD_DIM):
            blk = qt[h * HEAD_DIM:(h + 1) * HEAD_DIM]
            ms = jnp.sum(blk * blk, axis=0, keepdims=True) * (1.0 / HEAD_DIM)
            pieces.append(blk * lax.rsqrt(ms + 1e-6))
        qt_ref[0] = (jnp.concatenate(pieces, axis=0) * gq
                     * (HEAD_DIM ** -0.5 * LOG2E)).astype(qt_ref.dtype)

    def emit_raw(dst_ref, col):
        dst_ref[0] = proj(col, KV_W)

    def emit_key(dst_ref, col, branch):
        dst_ref[0] = _head_rms_rows(proj(col, KV_W), kg_ref[0, branch:branch + 1, :]).astype(dst_ref.dtype)

    def emit_value_t(dst_ref, col):
        vt = proj(col, KV_W).T
        for c in range(tm // ch):
            dst_ref[0, c] = vt[:, c * ch:(c + 1) * ch].astype(dst_ref.dtype)

    def emit_gates():
        ngt_ref[0] = proj(_C_NG, LANES).T[0:16, :]

    held = {}

    def hold(name, col):
        held[name] = proj(col, MIX_W)

    slices = [
        [lambda: emit_raw(kc_ref, _C_KC), lambda: emit_raw(vc_ref, _C_VC)],
        [lambda: emit_key(ks_ref, _C_KS, 1), lambda: emit_key(kw_ref, _C_KW, 2)],
        [lambda: emit_value_t(vst_ref, _C_VS), lambda: emit_value_t(vwt_ref, _C_VW)],
        [emit_q],
        [emit_gates, lambda: hold("sb", _C_SB)],
        [lambda: hold("sc", _C_SC), lambda: hold("sh", _C_SH)],
        [lambda: hold("gu", _C_GU)],
        [lambda: hold("gv", _C_GV)],
    ]
    acc = None
    for tasks in slices:
        for task in tasks:
            task()
        acc = next(conv, acc)
    for acc in conv:
        pass
    yacd_ref[0, :, MIX_W:2 * MIX_W] = jax.nn.silu(_layernorm_rows(
        acc + cbias_ref[0], clg_ref[0], clb_ref[0])).astype(yacd_ref.dtype)

    sc = held["sc"] * held["sh"]
    for acc in causal_conv(jnp.concatenate([cbuf[...], sc], axis=0), sw_ref[0], SCONV_KERNEL, SCONV_HALO):
        pass
    cbuf[...] = sc[tm - SCONV_HALO:tm]
    yacd_ref[0, :, 2 * MIX_W:3 * MIX_W] = (held["sb"] * acc).astype(yacd_ref.dtype)

    u = jax.nn.gelu(held["gu"])
    v = jax.nn.gelu(held["gv"])
    v = _layernorm_rows(v, lng_ref[0], lnb_ref[0]).astype(MXU_DTYPE)
    tri = (lax.broadcasted_iota(jnp.int32, (GMLP_CHUNK, GMLP_CHUNK), 0)
           >= lax.broadcasted_iota(jnp.int32, (GMLP_CHUNK, GMLP_CHUNK), 1))
    grp = lax.broadcasted_iota(jnp.int32, (GMLP_CHUNK, MIX_W), 1) >> 6
    wsm = [jnp.where(tri, ws_ref[0, g], 0.0).astype(MXU_DTYPE) for g in range(GMLP_GROUPS)]
    for c in range(tm // GMLP_CHUNK):
        rows = slice(c * GMLP_CHUNK, (c + 1) * GMLP_CHUNK)
        vch = v[rows]
        mixed = jnp.zeros((GMLP_CHUNK, MIX_W), jnp.float32)
        for g in range(GMLP_GROUPS):
            r = jnp.dot(wsm[g], vch, preferred_element_type=jnp.float32)
            mixed = jnp.where(grp == g, r, mixed)
        yacd_ref[0, rows, 0:MIX_W] = (u[rows] * (mixed + bst_ref[0])).astype(yacd_ref.dtype)


def _proj_call(x, l, wa, wng, wb, g1, lng, lnb, ws, bst, qg, kg, cw, cbias, clg, clb, sw):
    B, S, D = x.shape
    tm, ch = min(TM_PROJ, S), CH_ATT
    full = lambda a: pl.BlockSpec(a.shape, lambda b, i: (0,) * a.ndim)
    seq = lambda width: pl.BlockSpec((1, tm, width), lambda b, i: (b, i, 0))
    stacked = (lng, lnb, ws, bst, qg, kg, cw, cbias, clg, clb, sw)
    ins = (g1, wa, wng, wb) + stacked
    in_specs = ([seq(D), _layer_spec(g1, l), full(wa), full(wng), full(wb)]
                + [_layer_spec(a, l) for a in stacked])
    out_shape = (
        jax.ShapeDtypeStruct((B, S, 3 * MIX_W), MXU_DTYPE),
        jax.ShapeDtypeStruct((B, MIX_W, S), MXU_DTYPE),
        jax.ShapeDtypeStruct((B, S, KV_W), jnp.float32),
        jax.ShapeDtypeStruct((B, S, KV_W), jnp.float32),
        jax.ShapeDtypeStruct((B, S, KV_W), MXU_DTYPE),
        jax.ShapeDtypeStruct((B, S // ch, KV_W, ch), MXU_DTYPE),
        jax.ShapeDtypeStruct((B, S, KV_W), MXU_DTYPE),
        jax.ShapeDtypeStruct((B, S // ch, KV_W, ch), MXU_DTYPE),
        jax.ShapeDtypeStruct((B, 16, S), jnp.float32),
    )
    vt_spec = pl.BlockSpec((1, tm // ch, KV_W, ch), lambda b, i: (b, i, 0, 0))
    out_specs = (
        seq(3 * MIX_W),
        pl.BlockSpec((1, MIX_W, tm), lambda b, i: (b, 0, i)),
        seq(KV_W), seq(KV_W), seq(KV_W), vt_spec, seq(KV_W), vt_spec,
        pl.BlockSpec((1, 16, tm), lambda b, i: (b, 0, i)),
    )
    return pl.pallas_call(
        functools.partial(_proj_kernel, tm=tm, ch=ch),
        grid=(B, S // tm),
        in_specs=in_specs,
        out_specs=out_specs,
        out_shape=out_shape,
        scratch_shapes=[pltpu.VMEM((CONV_HALO, MIX_W), jnp.float32),
                        pltpu.VMEM((SCONV_HALO, MIX_W), jnp.float32)],
        compiler_params=pltpu.CompilerParams(
            dimension_semantics=("arbitrary", "arbitrary"), vmem_limit_bytes=VMEM_LIMIT),
        name="proj_mixers",
    )(x, *ins)


def _compress_kernel(kc_ref, vc_ref, pet_ref, peb_ref, wt_ref, wb_ref, w2_ref, kg_ref,
                     kcmp_ref, vcmpt_ref):
    nc = kc_ref.shape[1] // CMP_STRIDE

    def compress(src_ref, idx):
        chunks = jnp.concatenate(
            [src_ref[0, pl.ds(l, nc, stride=CMP_STRIDE), :] for l in range(CMP_STRIDE)], axis=1)
        a1 = jnp.dot((chunks + pet_ref[0, idx]).astype(MXU_DTYPE), wt_ref[0, idx],
                     preferred_element_type=jnp.float32)
        a2 = jnp.dot((chunks + peb_ref[0, idx]).astype(MXU_DTYPE), wb_ref[0, idx],
                     preferred_element_type=jnp.float32)
        hid = jax.nn.gelu(a1 + pltpu.roll(a2, shift=nc - 1, axis=0))
        return jnp.dot(hid.astype(MXU_DTYPE), w2_ref[0, idx], preferred_element_type=jnp.float32)

    kcmp_ref[0] = _head_rms_rows(compress(kc_ref, 0), kg_ref[0, 0:1, :]).astype(kcmp_ref.dtype)
    vcmpt_ref[0] = compress(vc_ref, 1).T.astype(vcmpt_ref.dtype)


def _compress_call(kc, vc, l, pet, peb, wt, wb, w2, kg):
    B, S, _ = kc.shape
    nc = S // CMP_STRIDE
    per_b = lambda r, c: pl.BlockSpec((1, r, c), lambda b: (b, 0, 0))
    ins = (pet, peb, wt, wb, w2, kg)
    return pl.pallas_call(
        _compress_kernel,
        grid=(B,),
        in_specs=[per_b(S, KV_W)] * 2 + [_layer_spec(a, l) for a in ins],
        out_specs=(per_b(nc, KV_W), per_b(KV_W, nc)),
        out_shape=(jax.ShapeDtypeStruct((B, nc, KV_W), MXU_DTYPE),
                   jax.ShapeDtypeStruct((B, KV_W, nc), MXU_DTYPE)),
        compiler_params=pltpu.CompilerParams(
            dimension_semantics=("arbitrary",), vmem_limit_bytes=VMEM_LIMIT),
        name="nsa_compress",
    )(kc, vc, *ins)


def _attn_kernel(qt_ref, ngt_ref, kcmp_ref, vcmpt_ref, ks_ref, vst_ref, kw_ref, vwt_ref,
                 y_ref, selb_scr, pcmp_scr, ssel_scr, stail_scr, *, t, seq_len):
    i = pl.program_id(1)
    nc = seq_len // CMP_STRIDE
    n_cmp = nc - 1
    n_sel = seq_len // SEL_BLOCK
    k_top = min(SEL_TOPK, n_sel)
    bpc = t // SEL_BLOCK
    cmp_per_sel = SEL_BLOCK // CMP_STRIDE
    nw = WINDOW // t
    t2 = 2 * t
    hq = t // 2
    f32 = jnp.float32

    def lanes_of(a0, a1):
        return jnp.concatenate([a0[:, :hq], a1[:, :hq], a0[:, hq:], a1[:, hq:]], axis=1)

    def both(a):
        return lanes_of(a, a)

    def keep_bias(cond):
        return jnp.where(cond, 0.0, NEG_INF).astype(f32)

    qt = qt_ref[0]
    qpads = []
    for h in range(N_KV_HEADS):
        qh = lanes_of(qt[(2 * h) * HEAD_DIM:(2 * h + 1) * HEAD_DIM],
                      qt[(2 * h + 1) * HEAD_DIM:(2 * h + 2) * HEAD_DIM])
        zero = jnp.zeros_like(qh)
        qpads.append(jnp.concatenate([qh, zero] if h == 0 else [zero, qh], axis=0))

    n_i = lax.broadcasted_iota(jnp.int32, (nc, t), 0)
    qpos_c = i * t + lax.broadcasted_iota(jnp.int32, (nc, t), 1)
    cmp_ok = (n_i * CMP_STRIDE + (CMP_LEN - 1) <= qpos_c) & (n_i < n_cmp)
    cmp_bias = both(keep_bias(cmp_ok))
    cmp_keep = both(jnp.where(cmp_ok, 1.0, 0.0).astype(f32))

    blk = lax.broadcasted_iota(jnp.int32, (n_sel, t), 0)
    cur = (i * t + lax.broadcasted_iota(jnp.int32, (n_sel, t), 1)) >> 6
    dist = cur - blk
    causal = dist >= 0
    forced = (blk == 0) | (causal & (dist < N_LOCAL_BLOCKS))
    blk8 = lax.broadcasted_iota(jnp.int32, (8, t), 0)

    kcmp = kcmp_ref[0]
    pcmp_scr[:, 0:CMP_PAD, :] = jnp.zeros((t // LANES, CMP_PAD, LANES), f32)
    o_cmp = []
    for h in range(N_KV_HEADS):
        sm = jnp.dot(kcmp, qpads[h], preferred_element_type=f32) + cmp_bias
        e = jnp.exp2(sm - jnp.max(sm, axis=0, keepdims=True)) * cmp_keep
        den = jnp.sum(e, axis=0, keepdims=True)
        p_cmp = e * (1.0 / jnp.where(den > 0.0, den, 1.0))
        o_cmp.append(jnp.dot(vcmpt_ref[0, h * HEAD_DIM:(h + 1) * HEAD_DIM, :],
                             p_cmp.astype(MXU_DTYPE), preferred_element_type=f32))

        p_pair = jnp.concatenate([p_cmp[:, 0:hq] + p_cmp[:, hq:t],
                                  p_cmp[:, t:t + hq] + p_cmp[:, t + hq:t2]], axis=1)
        slabs = []
        for lb in range(t // LANES):
            pcmp_scr[lb, CMP_PAD:CMP_PAD + nc, :] = p_pair[:, lb * LANES:(lb + 1) * LANES]
            slabs.append(functools.reduce(lambda a, b: a + b, [
                pcmp_scr[lb, pl.ds(CMP_PAD + off, n_sel, stride=cmp_per_sel), :]
                for off in range(1 - CMP_LEN // CMP_STRIDE, cmp_per_sel)]))
        p_slc = jnp.concatenate(slabs, axis=1)
        score = jnp.where(forced, jnp.inf, jnp.where(causal, p_slc, -jnp.inf))
        groups = [score[8 * g:8 * g + 8] for g in range(n_sel // 8)]
        rank = [jnp.zeros((8, t), f32) for _ in groups]
        for r in range(n_sel):
            row = score[r:r + 1, :]
            for g, sg in enumerate(groups):
                if 8 * g > r:
                    beats = row >= sg
                elif 8 * g + 7 <= r:
                    beats = row > sg
                else:
                    beats = (row > sg) | ((row == sg) & (blk8 > r - 8 * g))
                rank[g] = rank[g] + jnp.where(beats, 1.0, 0.0)
        selected = (jnp.concatenate(rank, axis=0) < k_top) & causal
        selb_scr[h] = both(keep_bias(selected))

    r_i = lax.broadcasted_iota(jnp.int32, (t, t), 0)
    q_i = lax.broadcasted_iota(jnp.int32, (t, t), 1)
    diag_bias = both(keep_bias(r_i <= q_i))
    ones_rows = jnp.where(lax.broadcasted_iota(jnp.int32, (V_AUG_ROWS, t), 0) == 0,
                          1.0, 0.0).astype(MXU_DTYPE)

    def k_chunk(k_ref, c):
        return k_ref[0, pl.ds(pl.multiple_of(c * t, t), t), :]

    def v_aug(vt_ref, c, h):
        return jnp.concatenate([vt_ref[0, c, h * HEAD_DIM:(h + 1) * HEAD_DIM, :], ones_rows], axis=0)

    def colmax8(s):
        return jnp.max(s.reshape(s.shape[0] // 8, 8, s.shape[1]), axis=0)

    def weighted_values(s, m, vaug):
        p = jnp.exp2(s - m).astype(MXU_DTYPE)
        return jnp.dot(vaug, p, preferred_element_type=f32)

    def normalise(acc):
        return acc[0:HEAD_DIM] * (1.0 / acc[HEAD_DIM:HEAD_DIM + 1])

    def on_lanes(full, part, lanes, op):
        pieces = [full[:, :lanes.start]] if lanes.start else []
        pieces.append(op(full[:, lanes], part))
        if lanes.stop < full.shape[1]:
            pieces.append(full[:, lanes.stop:])
        return pieces[0] if len(pieces) == 1 else jnp.concatenate(pieces, axis=1)

    top, bot = slice(0, hq), slice(hq, t)
    lo_q, hi_q, all_q = slice(0, t), slice(t, t2), slice(0, t2)
    full_pieces = ((slice(0, t), all_q),)
    diag_pieces = ((top, all_q), (bot, hi_q))
    first_pieces = ((top, lo_q), (bot, all_q))

    heads = range(N_KV_HEADS)
    neg8 = jnp.full((8, t2), NEG_INF, f32)

    def sel_scores(c, k, h, rows, lanes, extra):
        s = jnp.dot(k[rows], qpads[h][:, lanes], preferred_element_type=f32)
        parts = []
        for r in range((rows.stop - rows.start) // SEL_BLOCK):
            blk_rows = slice(r * SEL_BLOCK, (r + 1) * SEL_BLOCK)
            bias = selb_scr[h, pl.ds(c * bpc + rows.start // SEL_BLOCK + r, 1), lanes]
            if extra is not None:
                bias = bias + (extra[blk_rows] if extra.shape[0] > 1 else extra)
            parts.append(s[blk_rows] + bias)
        return jnp.concatenate(parts, axis=0)

    def sel_pass1(pair, pm):
        pm = list(pm)
        for u in range(2):
            c = 2 * pair + u
            k = k_chunk(ks_ref, c)
            for h in heads:
                s = sel_scores(c, k, h, slice(0, t), all_q, None)
                ssel_scr[h, c] = s
                pm[h] = jnp.maximum(pm[h], colmax8(s))
        return tuple(pm)

    n_pairs = i >> 1
    pm = lax.fori_loop(0, n_pairs, sel_pass1, (neg8, neg8))
    c_left = jnp.maximum(i - 1, 0)
    left_bias = jnp.full((1, t2), jnp.where((i & 1) == 1, 0.0, NEG_INF), f32)
    tail = ((0, c_left, full_pieces, (left_bias,)),
            (1, i, diag_pieces, tuple(diag_bias[rows, lanes] for rows, lanes in diag_pieces)))
    pm = list(pm)
    for u, c, pieces, extras in tail:
        k = k_chunk(ks_ref, c)
        for (rows, lanes), extra in zip(pieces, extras):
            for h in heads:
                s = sel_scores(c, k, h, rows, lanes, extra)
                stail_scr[h, u, rows, lanes] = s
                pm[h] = on_lanes(pm[h], colmax8(s), lanes, jnp.maximum)
    m_sel = [jnp.max(pm[h], axis=0, keepdims=True) for h in heads]

    def sel_pass2(pair, accs):
        accs = list(accs)
        for u in range(2):
            c = 2 * pair + u
            for h in heads:
                accs[h] = accs[h] + weighted_values(ssel_scr[h, c], m_sel[h], v_aug(vst_ref, c, h))
        return tuple(accs)

    zero_acc = jnp.zeros((HEAD_DIM + V_AUG_ROWS, t2), f32)
    accs = list(lax.fori_loop(0, n_pairs, sel_pass2, (zero_acc, zero_acc)))
    for u, c, pieces, _ in tail:
        for rows, lanes in pieces:
            for h in heads:
                part = weighted_values(stail_scr[h, u, rows, lanes], m_sel[h][:, lanes],
                                       v_aug(vst_ref, c, h)[:, rows])
                accs[h] = on_lanes(accs[h], part, lanes, lambda a, b: a + b)
    o_sel = [normalise(accs[h]) for h in heads]

    first_bias = both(keep_bias(r_i > q_i))
    work = []
    for cc in range(nw + 1):
        ci = i - nw + cc
        cic = jnp.maximum(ci, 0)
        exists = jnp.where(ci >= 0, 0.0, NEG_INF).astype(f32)
        k = k_chunk(kw_ref, cic)
        if cc == nw:
            work += [(cic, rows, lanes, diag_bias[rows, lanes], k) for rows, lanes in diag_pieces]
        elif cc == 0:
            work += [(cic, rows, lanes, first_bias[rows, lanes] + exists, k)
                     for rows, lanes in first_pieces]
        else:
            work += [(cic, rows, lanes, jnp.full((1, t2), exists, f32), k)
                     for rows, lanes in full_pieces]
    o_win = []
    for h in heads:
        ss = [jnp.dot(k[rows], qpads[h][:, lanes], preferred_element_type=f32) + b
              for _, rows, lanes, b, k in work]
        pm = neg8
        for (_, _, lanes, _, _), s in zip(work, ss):
            pm = on_lanes(pm, colmax8(s), lanes, jnp.maximum)
        m = jnp.max(pm, axis=0, keepdims=True)
        acc = zero_acc
        for (cic, rows, lanes, _, _), s in zip(work, ss):
            part = weighted_values(s, m[:, lanes], v_aug(vwt_ref, cic, h)[:, rows])
            acc = on_lanes(acc, part, lanes, lambda a, b: a + b)
        o_win.append(normalise(acc))

    sig = jax.nn.sigmoid(ngt_ref[0])
    out_rows = []
    for h in range(N_KV_HEADS):
        def gate(r):
            a = (2 * h) * 3 + r
            b = (2 * h + 1) * 3 + r
            return lanes_of(sig[a:a + 1, :], sig[b:b + 1, :])

        o = gate(0) * o_cmp[h] + gate(1) * o_sel[h] + gate(2) * o_win[h]
        out_rows += [jnp.concatenate([o[:, 0:hq], o[:, t:t + hq]], axis=1),
                     jnp.concatenate([o[:, hq:t], o[:, t + hq:t2]], axis=1)]
    y_ref[0] = jnp.concatenate(out_rows, axis=0).T.astype(y_ref.dtype)


def _attn_call(qt, ngt, kcmp, vcmpt, ks, vst, kw, vwt):
    B, _, S = qt.shape
    t = T_ATT
    assert CH_ATT == t and WINDOW % t == 0 and S % t == 0 and (S // SEL_BLOCK) % 8 == 0
    assert SEL_BLOCK % CMP_STRIDE == 0 and CMP_LEN % CMP_STRIDE == 0 and CMP_LEN // CMP_STRIDE <= CMP_PAD
    nc = S // CMP_STRIDE
    n_sel = S // SEL_BLOCK
    per_b3 = lambda a: pl.BlockSpec((1,) + a.shape[1:], lambda b, i: (b, 0, 0))
    per_b4 = lambda a: pl.BlockSpec((1,) + a.shape[1:], lambda b, i: (b, 0, 0, 0))
    return pl.pallas_call(
        functools.partial(_attn_kernel, t=t, seq_len=S),
        grid=(B, S // t),
        in_specs=[pl.BlockSpec((1, MIX_W, t), lambda b, i: (b, 0, i)),
                  pl.BlockSpec((1, 16, t), lambda b, i: (b, 0, i)),
                  per_b3(kcmp), per_b3(vcmpt), per_b3(ks), per_b4(vst), per_b3(kw), per_b4(vwt)],
        out_specs=pl.BlockSpec((1, t, MIX_W), lambda b, i: (b, i, 0)),
        out_shape=jax.ShapeDtypeStruct((B, S, MIX_W), MXU_DTYPE),
        scratch_shapes=[
            pltpu.VMEM((N_KV_HEADS, n_sel, 2 * t), jnp.float32),
            pltpu.VMEM((t // LANES, CMP_PAD + nc, LANES), jnp.float32),
            pltpu.VMEM((N_KV_HEADS, S // t, t, 2 * t), jnp.float32),
            pltpu.VMEM((N_KV_HEADS, 2, t, 2 * t), jnp.float32),
        ],
        compiler_params=pltpu.CompilerParams(
            dimension_semantics=("arbitrary", "arbitrary"), vmem_limit_bytes=VMEM_LIMIT),
        name="nsa_attention",
    )(qt, ngt, kcmp, vcmpt, ks, vst, kw, vwt)


def _merge_mlp_kernel(x_ref, yacd_ref, yb_ref, g1_ref, wbr_ref, wg_ref, bg_ref, wo_ref,
                      g2_ref, w1_ref, w2_ref, o_ref):
    x = x_ref[0]
    xn = _rms_rows(x, g1_ref[0]).astype(MXU_DTYPE)
    ys = (yacd_ref[0, :, 0:MIX_W], yb_ref[0], yacd_ref[0, :, MIX_W:2 * MIX_W],
          yacd_ref[0, :, 2 * MIX_W:3 * MIX_W])
    mixed = jnp.zeros(x.shape, jnp.float32)
    for n, y in enumerate(ys):
        cols = slice(n * D_MODEL, (n + 1) * D_MODEL)
        proj = jnp.dot(y, wbr_ref[0, n], preferred_element_type=jnp.float32)
        logits = jnp.dot(xn, wg_ref[0, :, cols], preferred_element_type=jnp.float32) + bg_ref[0, :, cols]
        mixed = mixed + jax.nn.sigmoid(logits) * proj
    h = x + jnp.dot(mixed.astype(MXU_DTYPE), wo_ref[0], preferred_element_type=jnp.float32)

    hn = _rms_rows(h, g2_ref[0]).astype(MXU_DTYPE)
    acc = h
    for c in range(D_FF // D_MODEL):
        cols = slice(c * D_MODEL, (c + 1) * D_MODEL)
        hid = jnp.dot(hn, w1_ref[0, :, cols], preferred_element_type=jnp.float32)
        hid = jnp.square(jnp.maximum(hid, 0.0)).astype(MXU_DTYPE)
        acc = acc + jnp.dot(hid, w2_ref[0, cols, :], preferred_element_type=jnp.float32)
    o_ref[0] = acc


def _merge_mlp_call(x, l, yacd, yb, g1, wbr, wg, bg, wo, g2, w1, w2):
    B, S, d = x.shape
    tm = min(TM_MERGE, S)
    rows = lambda w: pl.BlockSpec((1, tm, w), lambda b, i: (b, i, 0))
    stacked = (g1, wbr, wg, bg, wo, g2, w1, w2)
    return pl.pallas_call(
        _merge_mlp_kernel,
        grid=(B, S // tm),
        in_specs=[rows(d), rows(3 * MIX_W), rows(MIX_W)] + [_layer_spec(a, l) for a in stacked],
        out_specs=rows(d),
        out_shape=jax.ShapeDtypeStruct((B, S, d), jnp.float32),
        compiler_params=pltpu.CompilerParams(
            dimension_semantics=("arbitrary", "arbitrary"), vmem_limit_bytes=VMEM_LIMIT),
        name="merge_mlp",
    )(x, yacd, yb, *stacked)


def _split_w_in_kernel(w_ref, wa_ref, wng_ref, wb_ref):
    w = w_ref[0]
    wa_ref[...] = w[:, :_NG_SRC].astype(wa_ref.dtype)
    ng = w[:, _NG_SRC:_NG_SRC + LANES]
    lane = lax.broadcasted_iota(jnp.int32, ng.shape, 1)
    wng_ref[...] = jnp.where(lane < NSA_GATE_W, ng, 0.0).astype(wng_ref.dtype)
    wb_ref[...] = w[:, _NG_SRC + NSA_GATE_W:].astype(wb_ref.dtype)


def _split_w_in(w_in, l):
    _, d, cols = w_in.shape
    tail = cols - _NG_SRC - NSA_GATE_W
    rows = 256
    blk = lambda width: pl.BlockSpec((rows, width), lambda r: (r, 0))
    return pl.pallas_call(
        _split_w_in_kernel,
        grid=(d // rows,),
        in_specs=[pl.BlockSpec((1, rows, cols), lambda r: (l, r, 0))],
        out_specs=(blk(_NG_SRC), blk(LANES), blk(tail)),
        out_shape=(jax.ShapeDtypeStruct((d, _NG_SRC), MXU_DTYPE),
                   jax.ShapeDtypeStruct((d, LANES), MXU_DTYPE),
                   jax.ShapeDtypeStruct((d, tail), MXU_DTYPE)),
        compiler_params=pltpu.CompilerParams(
            dimension_semantics=("arbitrary",), vmem_limit_bytes=VMEM_LIMIT),
        name="split_w_in",
    )(w_in)


def _compress_weights(pe, w1, w2):
    assert N_KV_HEADS == 2
    L = pe.shape[0]
    w1r = w1.reshape(L, 2, CMP_LEN, HEAD_DIM, CMP_HIDDEN)

    def block_diag(w):
        z = jnp.zeros_like(w)
        return jnp.concatenate([jnp.concatenate([w, z], axis=-1),
                                jnp.concatenate([z, w], axis=-1)], axis=-2)

    def expand(w):
        return block_diag(w).reshape(
            L, 2, CMP_STRIDE * KV_W, N_KV_HEADS * CMP_HIDDEN).astype(MXU_DTYPE)

    def expand_pe(p):
        return jnp.concatenate([p, p], axis=-1).reshape(L, 2, 1, CMP_STRIDE * KV_W)

    w2b = block_diag(w2).astype(MXU_DTYPE)
    return (expand_pe(pe[:, :, :CMP_STRIDE]), expand_pe(pe[:, :, CMP_STRIDE:]),
            expand(w1r[:, :, :CMP_STRIDE]), expand(w1r[:, :, CMP_STRIDE:]), w2b)


def _layer_mixers(x, l, p):
    (yacd, qt, kc, vc, ks, vst, kw, vwt, ngt) = _proj_call(
        x, l, *_split_w_in(p["w_in"], l), p["norm1_g"], p["gmlp_ln_g"], p["gmlp_ln_b"], p["gmlp_ws"],
        p["bst"], p["qg"], p["kg"], p["conf_conv_w"], p["conf_conv_b"], p["conf_ln_g"],
        p["conf_ln_b"], p["sconv_w"])
    kcmp, vcmpt = _compress_call(kc, vc, l, *p["cmp_w"], p["kg"])
    yb = _attn_call(qt, ngt, kcmp, vcmpt, ks, vst, kw, vwt)
    return yacd, yb


def _prepare(p):
    q = dict(p)
    q["bst"] = jnp.repeat(jnp.swapaxes(p["gmlp_bs"], 1, 2), MIX_W // GMLP_GROUPS, axis=2)
    qg = jnp.tile(p["nsa_q_norm_g"], (1, MIX_W // HEAD_DIM))
    q["qg"] = jnp.broadcast_to(qg[:, :, None], qg.shape + (LANES,))
    q["kg"] = jnp.tile(p["nsa_k_norm_g"], (1, 1, N_KV_HEADS))
    q["cmp_w"] = _compress_weights(p["nsa_cmp_pe"], p["nsa_cmp_w1"], p["nsa_cmp_w2"])
    for name in ("w_branch", "w_gate", "w_out", "w_mlp1", "w_mlp2"):
        q[name] = p[name].astype(MXU_DTYPE)
    for name in ("norm1_g", "gmlp_ln_g", "gmlp_ln_b", "conf_conv_b", "conf_ln_g", "conf_ln_b",
                 "b_gate", "norm2_g"):
        q[name] = p[name][:, None, :]
    return q


def kernel(x, norm1_g, w_in, gmlp_ln_g, gmlp_ln_b, gmlp_ws, gmlp_bs, nsa_q_norm_g, nsa_k_norm_g,
           nsa_cmp_pe, nsa_cmp_w1, nsa_cmp_w2, conf_conv_w, conf_conv_b, conf_ln_g, conf_ln_b,
           sconv_w, w_branch, w_gate, b_gate, w_out, norm2_g, w_mlp1, w_mlp2):
    B, S, D = x.shape
    assert D == D_MODEL and S % min(TM_PROJ, S) == 0 and S % min(T_ATT, S) == 0 and S % CH_ATT == 0
    p = _prepare(dict(
        norm1_g=norm1_g, w_in=w_in, gmlp_ln_g=gmlp_ln_g, gmlp_ln_b=gmlp_ln_b, gmlp_ws=gmlp_ws,
        gmlp_bs=gmlp_bs, nsa_q_norm_g=nsa_q_norm_g, nsa_k_norm_g=nsa_k_norm_g,
        nsa_cmp_pe=nsa_cmp_pe, nsa_cmp_w1=nsa_cmp_w1, nsa_cmp_w2=nsa_cmp_w2,
        conf_conv_w=conf_conv_w, conf_conv_b=conf_conv_b, conf_ln_g=conf_ln_g, conf_ln_b=conf_ln_b,
        sconv_w=sconv_w, w_branch=w_branch, w_gate=w_gate, b_gate=b_gate, w_out=w_out,
        norm2_g=norm2_g, w_mlp1=w_mlp1, w_mlp2=w_mlp2))
    for l in range(norm1_g.shape[0]):
        yacd, yb = _layer_mixers(x, l, p)
        x = _merge_mlp_call(x, l, yacd, yb, p["norm1_g"], p["w_branch"], p["w_gate"], p["b_gate"],
                            p["w_out"], p["norm2_g"], p["w_mlp1"], p["w_mlp2"])
    return x
```

```python
import functools

import jax
import jax.numpy as jnp
from jax import lax
from jax.experimental import pallas as pl
from jax.experimental.pallas import tpu as pltpu

D_MODEL = 1024
MIX_W = 256
HEAD_DIM = 64
N_KV_HEADS = 2
Q_PER_KV = 2
KV_W = N_KV_HEADS * HEAD_DIM
GMLP_GROUPS = 4
GMLP_CHUNK = 128
CMP_LEN = 32
CMP_STRIDE = 16
CMP_HIDDEN = 128
SEL_BLOCK = 64
SEL_TOPK = 8
N_LOCAL_BLOCKS = 2
WINDOW = 512
CONF_KERNEL = 31
SCONV_KERNEL = 3
D_FF = 4 * D_MODEL
NSA_GATE_W = 12
NEG_INF = -1e30

LANES = 128
SUBLANES = 8
MXU_DTYPE = jnp.bfloat16
VMEM_LIMIT = 56 * 1024 * 1024

_C_GU, _C_GV, _C_Q = 0, 256, 512
_C_KC, _C_VC, _C_KS, _C_VS, _C_KW, _C_VW = 768, 896, 1024, 1152, 1280, 1408
_C_NG = 1536
_C_CA, _C_CB, _C_SB, _C_SC, _C_SH = 1664, 1920, 2176, 2432, 2688
_NG_SRC = 1536

TM_PROJ = 512
TM_MERGE = 512
T_ATT = 256
CH_ATT = 256
LOG2E = 1.4426950408889634
V_AUG_ROWS = 16
CMP_PAD = 8
CONV_HALO = 32
SCONV_HALO = 8


def _rms_rows(x, g, eps=1e-6):
    return x * lax.rsqrt(jnp.mean(x * x, axis=-1, keepdims=True) + eps) * g


def _layernorm_rows(x, g, b, eps=1e-5):
    mu = jnp.mean(x, axis=-1, keepdims=True)
    xc = x - mu
    var = jnp.mean(xc * xc, axis=-1, keepdims=True)
    return xc * lax.rsqrt(var + eps) * g + b


def _head_rms_rows(x, g, eps=1e-6):
    n_heads = x.shape[-1] // HEAD_DIM
    head = lax.broadcasted_iota(jnp.int32, x.shape, 1) >> 6
    xx = x * x
    scale = jnp.zeros_like(x)
    for h in range(n_heads):
        ms = jnp.sum(jnp.where(head == h, xx, 0.0), axis=-1, keepdims=True) * (1.0 / HEAD_DIM)
        scale = jnp.where(head == h, lax.rsqrt(ms + eps), scale)
    return x * scale * g


def _layer_spec(a, l):
    return pl.BlockSpec((1,) + a.shape[1:], lambda *_: (l,) + (0,) * (a.ndim - 1))


def _proj_kernel(x_ref, g1_ref, wa_ref, wng_ref, wb_ref, lng_ref, lnb_ref, ws_ref, bst_ref, qg_ref,
                 kg_ref, cw_ref, cbias_ref, clg_ref, clb_ref, sw_ref,
                 yacd_ref, qt_ref, kc_ref, vc_ref, ks_ref, vst_ref, kw_ref, vwt_ref, ngt_ref,
                 zbuf, cbuf, *, tm, ch):
    @pl.when(pl.program_id(1) == 0)
    def _():
        zbuf[...] = jnp.zeros((CONV_HALO, MIX_W), jnp.float32)
        cbuf[...] = jnp.zeros((SCONV_HALO, MIX_W), jnp.float32)

    x = x_ref[0]
    xn = _rms_rows(x, g1_ref[0]).astype(MXU_DTYPE)

    def proj(lo, width):
        if lo < _C_NG:
            w = wa_ref[lo:lo + width, :]
        elif lo == _C_NG:
            w = wng_ref[...]
        else:
            w = wb_ref[lo - _C_CA:lo - _C_CA + width, :]
        return lax.dot_general(xn, w, (((1,), (1,)), ((), ())), preferred_element_type=jnp.float32)

    def causal_conv(buf, w, width, halo):
        offsets = [halo - (width - 1) + j for j in range(width)]
        out = jnp.zeros((tm, MIX_W), jnp.float32)
        for b in range(SUBLANES):
            taps = [j for j in range(width) if offsets[j] % SUBLANES == b]
            if not taps:
                continue
            rows = tm if b == 0 else tm + SUBLANES
            part = jnp.zeros((rows, MIX_W), jnp.float32)
            for j in taps:
                part = part + w[j:j + 1] * buf[offsets[j] - b:offsets[j] - b + rows]
            out = out + part[b:b + tm]
            yield out


    z = proj(_C_CA, MIX_W) * jax.nn.sigmoid(proj(_C_CB, MIX_W))
    conv = causal_conv(jnp.concatenate([zbuf[...], z], axis=0), cw_ref[0], CONF_KERNEL, CONV_HALO)
    zbuf[...] = z[tm - CONV_HALO:tm]

    def emit_q():
        qt = proj(_C_Q, MIX_W).T
        gq = jnp.concatenate([qg_ref[0]] * (tm // LANES), axis=1)
        pieces = []
        for h in range(MIX_W // HEAD_DIM):
            blk = qt[h * HEAD_DIM:(h + 1) * HEAD_DIM]
            ms = jnp.sum(blk * blk, axis=0, keepdims=True) * (1.0 / HEAD_DIM)
            pieces.append(blk * lax.rsqrt(ms + 1e-6))
        qt_ref[0] = (jnp.concatenate(pieces, axis=0) * gq
                     * (HEAD_DIM ** -0.5 * LOG2E)).astype(qt_ref.dtype)

    def emit_raw(dst_ref, col):
        dst_ref[0] = proj(col, KV_W)

    def emit_key(dst_ref, col, branch):
        dst_ref[0] = _head_rms_rows(proj(col, KV_W), kg_ref[0, branch:branch + 1, :]).astype(dst_ref.dtype)

    def emit_value_t(dst_ref, col):
        vt = proj(col, KV_W).T
        for c in range(tm // ch):
            dst_ref[0, c] = vt[:, c * ch:(c + 1) * ch].astype(dst_ref.dtype)

    def emit_gates():
        ngt_ref[0] = proj(_C_NG, LANES).T[0:16, :]

    held = {}

    def hold(name, col):
        held[name] = proj(col, MIX_W)

    slices = [
        [lambda: emit_raw(kc_ref, _C_KC), lambda: emit_raw(vc_ref, _C_VC)],
        [lambda: emit_key(ks_ref, _C_KS, 1), lambda: emit_key(kw_ref, _C_KW, 2)],
        [lambda: emit_value_t(vst_ref, _C_VS), lambda: emit_value_t(vwt_ref, _C_VW)],
        [emit_q],
        [emit_gates, lambda: hold("sb", _C_SB)],
        [lambda: hold("sc", _C_SC), lambda: hold("sh", _C_SH)],
        [lambda: hold("gu", _C_GU)],
        [lambda: hold("gv", _C_GV)],
    ]
    acc = None
    for tasks in slices:
        for task in tasks:
            task()
        acc = next(conv, acc)
    for acc in conv:
        pass
    yacd_ref[0, :, MIX_W:2 * MIX_W] = jax.nn.silu(_layernorm_rows(
        acc + cbias_ref[0], clg_ref[0], clb_ref[0])).astype(yacd_ref.dtype)

    sc = held["sc"] * held["sh"]
    for acc in causal_conv(jnp.concatenate([cbuf[...], sc], axis=0), sw_ref[0], SCONV_KERNEL, SCONV_HALO):
        pass
    cbuf[...] = sc[tm - SCONV_HALO:tm]
    yacd_ref[0, :, 2 * MIX_W:3 * MIX_W] = (held["sb"] * acc).astype(yacd_ref.dtype)

    u = jax.nn.gelu(held["gu"])
    v = jax.nn.gelu(held["gv"])
    v = _layernorm_rows(v, lng_ref[0], lnb_ref[0]).astype(MXU_DTYPE)
    tri = (lax.broadcasted_iota(jnp.int32, (GMLP_CHUNK, GMLP_CHUNK), 0)
           >= lax.broadcasted_iota(jnp.int32, (GMLP_CHUNK, GMLP_CHUNK), 1))
    grp = lax.broadcasted_iota(jnp.int32, (GMLP_CHUNK, MIX_W), 1) >> 6
    wsm = [jnp.where(tri, ws_ref[0, g], 0.0).astype(MXU_DTYPE) for g in range(GMLP_GROUPS)]
    for c in range(tm // GMLP_CHUNK):
        rows = slice(c * GMLP_CHUNK, (c + 1) * GMLP_CHUNK)
        vch = v[rows]
        mixed = jnp.zeros((GMLP_CHUNK, MIX_W), jnp.float32)
        for g in range(GMLP_GROUPS):
            r = jnp.dot(wsm[g], vch, preferred_element_type=jnp.float32)
            mixed = jnp.where(grp == g, r, mixed)
        yacd_ref[0, rows, 0:MIX_W] = (u[rows] * (mixed + bst_ref[0])).astype(yacd_ref.dtype)


def _proj_call(x, l, wa, wng, wb, g1, lng, lnb, ws, bst, qg, kg, cw, cbias, clg, clb, sw):
    B, S, D = x.shape
    tm, ch = min(TM_PROJ, S), CH_ATT
    full = lambda a: pl.BlockSpec(a.shape, lambda b, i: (0,) * a.ndim)
    seq = lambda width: pl.BlockSpec((1, tm, width), lambda b, i: (b, i, 0))
    stacked = (lng, lnb, ws, bst, qg, kg, cw, cbias, clg, clb, sw)
    ins = (g1, wa, wng, wb) + stacked
    in_specs = ([seq(D), _layer_spec(g1, l), full(wa), full(wng), full(wb)]
                + [_layer_spec(a, l) for a in stacked])
    out_shape = (
        jax.ShapeDtypeStruct((B, S, 3 * MIX_W), MXU_DTYPE),
        jax.ShapeDtypeStruct((B, MIX_W, S), MXU_DTYPE),
        jax.ShapeDtypeStruct((B, S, KV_W), jnp.float32),
        jax.ShapeDtypeStruct((B, S, KV_W), jnp.float32),
        jax.ShapeDtypeStruct((B, S, KV_W), MXU_DTYPE),
        jax.ShapeDtypeStruct((B, S // ch, KV_W, ch), MXU_DTYPE),
        jax.ShapeDtypeStruct((B, S, KV_W), MXU_DTYPE),
        jax.ShapeDtypeStruct((B, S // ch, KV_W, ch), MXU_DTYPE),
        jax.ShapeDtypeStruct((B, 16, S), jnp.float32),
    )
    vt_spec = pl.BlockSpec((1, tm // ch, KV_W, ch), lambda b, i: (b, i, 0, 0))
    out_specs = (
        seq(3 * MIX_W),
        pl.BlockSpec((1, MIX_W, tm), lambda b, i: (b, 0, i)),
        seq(KV_W), seq(KV_W), seq(KV_W), vt_spec, seq(KV_W), vt_spec,
        pl.BlockSpec((1, 16, tm), lambda b, i: (b, 0, i)),
    )
    return pl.pallas_call(
        functools.partial(_proj_kernel, tm=tm, ch=ch),
        grid=(B, S // tm),
        in_specs=in_specs,
        out_specs=out_specs,
        out_shape=out_shape,
        scratch_shapes=[pltpu.VMEM((CONV_HALO, MIX_W), jnp.float32),
                        pltpu.VMEM((SCONV_HALO, MIX_W), jnp.float32)],
        compiler_params=pltpu.CompilerParams(
            dimension_semantics=("arbitrary", "arbitrary"), vmem_limit_bytes=VMEM_LIMIT),
        name="proj_mixers",
    )(x, *ins)


def _compress_kernel(kc_ref, vc_ref, pet_ref, peb_ref, wt_ref, wb_ref, w2_ref, kg_ref,
                     kcmp_ref, vcmpt_ref):
    nc = kc_ref.shape[1] // CMP_STRIDE

    def compress(src_ref, idx):
        chunks = jnp.concatenate(
            [src_ref[0, pl.ds(l, nc, stride=CMP_STRIDE), :] for l in range(CMP_STRIDE)], axis=1)
        a1 = jnp.dot((chunks + pet_ref[0, idx]).astype(MXU_DTYPE), wt_ref[0, idx],
                     preferred_element_type=jnp.float32)
        a2 = jnp.dot((chunks + peb_ref[0, idx]).astype(MXU_DTYPE), wb_ref[0, idx],
                     preferred_element_type=jnp.float32)
        hid = jax.nn.gelu(a1 + pltpu.roll(a2, shift=nc - 1, axis=0))
        return jnp.dot(hid.astype(MXU_DTYPE), w2_ref[0, idx], preferred_element_type=jnp.float32)

    kcmp_ref[0] = _head_rms_rows(compress(kc_ref, 0), kg_ref[0, 0:1, :]).astype(kcmp_ref.dtype)
    vcmpt_ref[0] = compress(vc_ref, 1).T.astype(vcmpt_ref.dtype)


def _compress_call(kc, vc, l, pet, peb, wt, wb, w2, kg):
    B, S, _ = kc.shape
    nc = S // CMP_STRIDE
    per_b = lambda r, c: pl.BlockSpec((1, r, c), lambda b: (b, 0, 0))
    ins = (pet, peb, wt, wb, w2, kg)
    return pl.pallas_call(
        _compress_kernel,
        grid=(B,),
        in_specs=[per_b(S, KV_W)] * 2 + [_layer_spec(a, l) for a in ins],
        out_specs=(per_b(nc, KV_W), per_b(KV_W, nc)),
        out_shape=(jax.ShapeDtypeStruct((B, nc, KV_W), MXU_DTYPE),
                   jax.ShapeDtypeStruct((B, KV_W, nc), MXU_DTYPE)),
        compiler_params=pltpu.CompilerParams(
            dimension_semantics=("arbitrary",), vmem_limit_bytes=VMEM_LIMIT),
        name="nsa_compress",
    )(kc, vc, *ins)


def _attn_kernel(qt_ref, ngt_ref, kcmp_ref, vcmpt_ref, ks_ref, vst_ref, kw_ref, vwt_ref,
                 y_ref, selb_scr, pcmp_scr, ssel_scr, stail_scr, *, t, seq_len):
    i = pl.program_id(1)
    nc = seq_len // CMP_STRIDE
    n_cmp = nc - 1
    n_sel = seq_len // SEL_BLOCK
    k_top = min(SEL_TOPK, n_sel)
    bpc = t // SEL_BLOCK
    cmp_per_sel = SEL_BLOCK // CMP_STRIDE
    nw = WINDOW // t
    t2 = 2 * t
    hq = t // 2
    f32 = jnp.float32

    def lanes_of(a0, a1):
        return jnp.concatenate([a0[:, :hq], a1[:, :hq], a0[:, hq:], a1[:, hq:]], axis=1)

    def both(a):
        return lanes_of(a, a)

    def keep_bias(cond):
        return jnp.where(cond, 0.0, NEG_INF).astype(f32)

    qt = qt_ref[0]
    qpads = []
    for h in range(N_KV_HEADS):
        qh = lanes_of(qt[(2 * h) * HEAD_DIM:(2 * h + 1) * HEAD_DIM],
                      qt[(2 * h + 1) * HEAD_DIM:(2 * h + 2) * HEAD_DIM])
        zero = jnp.zeros_like(qh)
        qpads.append(jnp.concatenate([qh, zero] if h == 0 else [zero, qh], axis=0))

    n_i = lax.broadcasted_iota(jnp.int32, (nc, t), 0)
    qpos_c = i * t + lax.broadcasted_iota(jnp.int32, (nc, t), 1)
    cmp_ok = (n_i * CMP_STRIDE + (CMP_LEN - 1) <= qpos_c) & (n_i < n_cmp)
    cmp_bias = both(keep_bias(cmp_ok))
    cmp_keep = both(jnp.where(cmp_ok, 1.0, 0.0).astype(f32))

    blk = lax.broadcasted_iota(jnp.int32, (n_sel, t), 0)
    cur = (i * t + lax.broadcasted_iota(jnp.int32, (n_sel, t), 1)) >> 6
    dist = cur - blk
    causal = dist >= 0
    forced = (blk == 0) | (causal & (dist < N_LOCAL_BLOCKS))
    blk8 = lax.broadcasted_iota(jnp.int32, (8, t), 0)

    kcmp = kcmp_ref[0]
    pcmp_scr[:, 0:CMP_PAD, :] = jnp.zeros((t // LANES, CMP_PAD, LANES), f32)
    o_cmp = []
    for h in range(N_KV_HEADS):
        sm = jnp.dot(kcmp, qpads[h], preferred_element_type=f32) + cmp_bias
        e = jnp.exp2(sm - jnp.max(sm, axis=0, keepdims=True)) * cmp_keep
        den = jnp.sum(e, axis=0, keepdims=True)
        p_cmp = e * (1.0 / jnp.where(den > 0.0, den, 1.0))
        o_cmp.append(jnp.dot(vcmpt_ref[0, h * HEAD_DIM:(h + 1) * HEAD_DIM, :],
                             p_cmp.astype(MXU_DTYPE), preferred_element_type=f32))

        p_pair = jnp.concatenate([p_cmp[:, 0:hq] + p_cmp[:, hq:t],
                                  p_cmp[:, t:t + hq] + p_cmp[:, t + hq:t2]], axis=1)
        slabs = []
        for lb in range(t // LANES):
            pcmp_scr[lb, CMP_PAD:CMP_PAD + nc, :] = p_pair[:, lb * LANES:(lb + 1) * LANES]
            slabs.append(functools.reduce(lambda a, b: a + b, [
                pcmp_scr[lb, pl.ds(CMP_PAD + off, n_sel, stride=cmp_per_sel), :]
                for off in range(1 - CMP_LEN // CMP_STRIDE, cmp_per_sel)]))
        p_slc = jnp.concatenate(slabs, axis=1)
        score = jnp.where(forced, jnp.inf, jnp.where(causal, p_slc, -jnp.inf))
        groups = [score[8 * g:8 * g + 8] for g in range(n_sel // 8)]
        rank = [jnp.zeros((8, t), f32) for _ in groups]
        for r in range(n_sel):
            row = score[r:r + 1, :]
            for g, sg in enumerate(groups):
                if 8 * g > r:
                    beats = row >= sg
                elif 8 * g + 7 <= r:
                    beats = row > sg
                else:
                    beats = (row > sg) | ((row == sg) & (blk8 > r - 8 * g))
                rank[g] = rank[g] + jnp.where(beats, 1.0, 0.0)
        selected = (jnp.concatenate(rank, axis=0) < k_top) & causal
        selb_scr[h] = both(keep_bias(selected))

    r_i = lax.broadcasted_iota(jnp.int32, (t, t), 0)
    q_i = lax.broadcasted_iota(jnp.int32, (t, t), 1)
    diag_bias = both(keep_bias(r_i <= q_i))
    ones_rows = jnp.where(lax.broadcasted_iota(jnp.int32, (V_AUG_ROWS, t), 0) == 0,
                          1.0, 0.0).astype(MXU_DTYPE)

    def k_chunk(k_ref, c):
        return k_ref[0, pl.ds(pl.multiple_of(c * t, t), t), :]

    def v_aug(vt_ref, c, h):
        return jnp.concatenate([vt_ref[0, c, h * HEAD_DIM:(h + 1) * HEAD_DIM, :], ones_rows], axis=0)

    def colmax8(s):
        return jnp.max(s.reshape(s.shape[0] // 8, 8, s.shape[1]), axis=0)

    def weighted_values(s, m, vaug):
        p = jnp.exp2(s - m).astype(MXU_DTYPE)
        return jnp.dot(vaug, p, preferred_element_type=f32)

    def normalise(acc):
        return acc[0:HEAD_DIM] * (1.0 / acc[HEAD_DIM:HEAD_DIM + 1])

    def on_lanes(full, part, lanes, op):
        pieces = [full[:, :lanes.start]] if lanes.start else []
        pieces.append(op(full[:, lanes], part))
        if lanes.stop < full.shape[1]:
            pieces.append(full[:, lanes.stop:])
        return pieces[0] if len(pieces) == 1 else jnp.concatenate(pieces, axis=1)

    top, bot = slice(0, hq), slice(hq, t)
    lo_q, hi_q, all_q = slice(0, t), slice(t, t2), slice(0, t2)
    full_pieces = ((slice(0, t), all_q),)
    diag_pieces = ((top, all_q), (bot, hi_q))
    first_pieces = ((top, lo_q), (bot, all_q))

    heads = range(N_KV_HEADS)
    neg8 = jnp.full((8, t2), NEG_INF, f32)

    def sel_scores(c, k, h, rows, lanes, extra):
        s = jnp.dot(k[rows], qpads[h][:, lanes], preferred_element_type=f32)
        parts = []
        for r in range((rows.stop - rows.start) // SEL_BLOCK):
            blk_rows = slice(r * SEL_BLOCK, (r + 1) * SEL_BLOCK)
            bias = selb_scr[h, pl.ds(c * bpc + rows.start // SEL_BLOCK + r, 1), lanes]
            if extra is not None:
                bias = bias + (extra[blk_rows] if extra.shape[0] > 1 else extra)
            parts.append(s[blk_rows] + bias)
        return jnp.concatenate(parts, axis=0)

    def sel_pass1(pair, pm):
        pm = list(pm)
        for u in range(2):
            c = 2 * pair + u
            k = k_chunk(ks_ref, c)
            for h in heads:
                s = sel_scores(c, k, h, slice(0, t), all_q, None)
                ssel_scr[h, c] = s
                pm[h] = jnp.maximum(pm[h], colmax8(s))
        return tuple(pm)

    n_pairs = i >> 1
    pm = lax.fori_loop(0, n_pairs, sel_pass1, (neg8, neg8))
    c_left = jnp.maximum(i - 1, 0)
    left_bias = jnp.full((1, t2), jnp.where((i & 1) == 1, 0.0, NEG_INF), f32)
    tail = ((0, c_left, full_pieces, (left_bias,)),
            (1, i, diag_pieces, tuple(diag_bias[rows, lanes] for rows, lanes in diag_pieces)))
    pm = list(pm)
    for u, c, pieces, extras in tail:
        k = k_chunk(ks_ref, c)
        for (rows, lanes), extra in zip(pieces, extras):
            for h in heads:
                s = sel_scores(c, k, h, rows, lanes, extra)
                stail_scr[h, u, rows, lanes] = s
                pm[h] = on_lanes(pm[h], colmax8(s), lanes, jnp.maximum)
    m_sel = [jnp.max(pm[h], axis=0, keepdims=True) for h in heads]

    def sel_pass2(pair, accs):
        accs = list(accs)
        for u in range(2):
            c = 2 * pair + u
            for h in heads:
                accs[h] = accs[h] + weighted_values(ssel_scr[h, c], m_sel[h], v_aug(vst_ref, c, h))
        return tuple(accs)

    zero_acc = jnp.zeros((HEAD_DIM + V_AUG_ROWS, t2), f32)
    accs = list(lax.fori_loop(0, n_pairs, sel_pass2, (zero_acc, zero_acc)))
    for u, c, pieces, _ in tail:
        for rows, lanes in pieces:
            for h in heads:
                part = weighted_values(stail_scr[h, u, rows, lanes], m_sel[h][:, lanes],
                                       v_aug(vst_ref, c, h)[:, rows])
                accs[h] = on_lanes(accs[h], part, lanes, lambda a, b: a + b)
    o_sel = [normalise(accs[h]) for h in heads]

    first_bias = both(keep_bias(r_i > q_i))
    work = []
    for cc in range(nw + 1):
        ci = i - nw + cc
        cic = jnp.maximum(ci, 0)
        exists = jnp.where(ci >= 0, 0.0, NEG_INF).astype(f32)
        k = k_chunk(kw_ref, cic)
        if cc == nw:
            work += [(cic, rows, lanes, diag_bias[rows, lanes], k) for rows, lanes in diag_pieces]
        elif cc == 0:
            work += [(cic, rows, lanes, first_bias[rows, lanes] + exists, k)
                     for rows, lanes in first_pieces]
        else:
            work += [(cic, rows, lanes, jnp.full((1, t2), exists, f32), k)
                     for rows, lanes in full_pieces]
    o_win = []
    for h in heads:
        ss = [jnp.dot(k[rows], qpads[h][:, lanes], preferred_element_type=f32) + b
              for _, rows, lanes, b, k in work]
        pm = neg8
        for (_, _, lanes, _, _), s in zip(work, ss):
            pm = on_lanes(pm, colmax8(s), lanes, jnp.maximum)
        m = jnp.max(pm, axis=0, keepdims=True)
        acc = zero_acc
        for (cic, rows, lanes, _, _), s in zip(work, ss):
            part = weighted_values(s, m[:, lanes], v_aug(vwt_ref, cic, h)[:, rows])
            acc = on_lanes(acc, part, lanes, lambda a, b: a + b)
        o_win.append(normalise(acc))

    sig = jax.nn.sigmoid(ngt_ref[0])
    out_rows = []
    for h in range(N_KV_HEADS):
        def gate(r):
            a = (2 * h) * 3 + r
            b = (2 * h + 1) * 3 + r
            return lanes_of(sig[a:a + 1, :], sig[b:b + 1, :])

        o = gate(0) * o_cmp[h] + gate(1) * o_sel[h] + gate(2) * o_win[h]
        out_rows += [jnp.concatenate([o[:, 0:hq], o[:, t:t + hq]], axis=1),
                     jnp.concatenate([o[:, hq:t], o[:, t + hq:t2]], axis=1)]
    y_ref[0] = jnp.concatenate(out_rows, axis=0).T.astype(y_ref.dtype)


def _attn_call(qt, ngt, kcmp, vcmpt, ks, vst, kw, vwt):
    B, _, S = qt.shape
    t = T_ATT
    assert CH_ATT == t and WINDOW % t == 0 and S % t == 0 and (S // SEL_BLOCK) % 8 == 0
    assert SEL_BLOCK % CMP_STRIDE == 0 and CMP_LEN % CMP_STRIDE == 0 and CMP_LEN // CMP_STRIDE <= CMP_PAD
    nc = S // CMP_STRIDE
    n_sel = S // SEL_BLOCK
    per_b3 = lambda a: pl.BlockSpec((1,) + a.shape[1:], lambda b, i: (b, 0, 0))
    per_b4 = lambda a: pl.BlockSpec((1,) + a.shape[1:], lambda b, i: (b, 0, 0, 0))
    return pl.pallas_call(
        functools.partial(_attn_kernel, t=t, seq_len=S),
        grid=(B, S // t),
        in_specs=[pl.BlockSpec((1, MIX_W, t), lambda b, i: (b, 0, i)),
                  pl.BlockSpec((1, 16, t), lambda b, i: (b, 0, i)),
                  per_b3(kcmp), per_b3(vcmpt), per_b3(ks), per_b4(vst), per_b3(kw), per_b4(vwt)],
        out_specs=pl.BlockSpec((1, t, MIX_W), lambda b, i: (b, i, 0)),
        out_shape=jax.ShapeDtypeStruct((B, S, MIX_W), MXU_DTYPE),
        scratch_shapes=[
            pltpu.VMEM((N_KV_HEADS, n_sel, 2 * t), jnp.float32),
            pltpu.VMEM((t // LANES, CMP_PAD + nc, LANES), jnp.float32),
            pltpu.VMEM((N_KV_HEADS, S // t, t, 2 * t), jnp.float32),
            pltpu.VMEM((N_KV_HEADS, 2, t, 2 * t), jnp.float32),
        ],
        compiler_params=pltpu.CompilerParams(
            dimension_semantics=("arbitrary", "arbitrary"), vmem_limit_bytes=VMEM_LIMIT),
        name="nsa_attention",
    )(qt, ngt, kcmp, vcmpt, ks, vst, kw, vwt)


def _merge_mlp_kernel(x_ref, yacd_ref, yb_ref, g1_ref, wbr_ref, wg_ref, bg_ref, wo_ref,
                      g2_ref, w1_ref, w2_ref, o_ref):
    x = x_ref[0]
    xn = _rms_rows(x, g1_ref[0]).astype(MXU_DTYPE)
    ys = (yacd_ref[0, :, 0:MIX_W], yb_ref[0], yacd_ref[0, :, MIX_W:2 * MIX_W],
          yacd_ref[0, :, 2 * MIX_W:3 * MIX_W])
    mixed = jnp.zeros(x.shape, jnp.float32)
    for n, y in enumerate(ys):
        cols = slice(n * D_MODEL, (n + 1) * D_MODEL)
        proj = jnp.dot(y, wbr_ref[0, n], preferred_element_type=jnp.float32)
        logits = jnp.dot(xn, wg_ref[0, :, cols], preferred_element_type=jnp.float32) + bg_ref[0, :, cols]
        mixed = mixed + jax.nn.sigmoid(logits) * proj
    h = x + jnp.dot(mixed.astype(MXU_DTYPE), wo_ref[0], preferred_element_type=jnp.float32)

    hn = _rms_rows(h, g2_ref[0]).astype(MXU_DTYPE)
    acc = h
    for c in range(D_FF // D_MODEL):
        cols = slice(c * D_MODEL, (c + 1) * D_MODEL)
        hid = jnp.dot(hn, w1_ref[0, :, cols], preferred_element_type=jnp.float32)
        hid = jnp.square(jnp.maximum(hid, 0.0)).astype(MXU_DTYPE)
        acc = acc + jnp.dot(hid, w2_ref[0, cols, :], preferred_element_type=jnp.float32)
    o_ref[0] = acc


def _merge_mlp_call(x, l, yacd, yb, g1, wbr, wg, bg, wo, g2, w1, w2):
    B, S, d = x.shape
    tm = min(TM_MERGE, S)
    rows = lambda w: pl.BlockSpec((1, tm, w), lambda b, i: (b, i, 0))
    stacked = (g1, wbr, wg, bg, wo, g2, w1, w2)
    return pl.pallas_call(
        _merge_mlp_kernel,
        grid=(B, S // tm),
        in_specs=[rows(d), rows(3 * MIX_W), rows(MIX_W)] + [_layer_spec(a, l) for a in stacked],
        out_specs=rows(d),
        out_shape=jax.ShapeDtypeStruct((B, S, d), jnp.float32),
        compiler_params=pltpu.CompilerParams(
            dimension_semantics=("arbitrary", "arbitrary"), vmem_limit_bytes=VMEM_LIMIT),
        name="merge_mlp",
    )(x, yacd, yb, *stacked)


def _split_w_in_kernel(w_ref, wa_ref, wng_ref, wb_ref):
    w = w_ref[0]
    wa_ref[...] = w[:, :_NG_SRC].astype(wa_ref.dtype)
    ng = w[:, _NG_SRC:_NG_SRC + LANES]
    lane = lax.broadcasted_iota(jnp.int32, ng.shape, 1)
    wng_ref[...] = jnp.where(lane < NSA_GATE_W, ng, 0.0).astype(wng_ref.dtype)
    wb_ref[...] = w[:, _NG_SRC + NSA_GATE_W:].astype(wb_ref.dtype)


def _split_w_in(w_in, l):
    _, d, cols = w_in.shape
    tail = cols - _NG_SRC - NSA_GATE_W
    rows = 256
    blk = lambda width: pl.BlockSpec((rows, width), lambda r: (r, 0))
    return pl.pallas_call(
        _split_w_in_kernel,
        grid=(d // rows,),
        in_specs=[pl.BlockSpec((1, rows, cols), lambda r: (l, r, 0))],
        out_specs=(blk(_NG_SRC), blk(LANES), blk(tail)),
        out_shape=(jax.ShapeDtypeStruct((d, _NG_SRC), MXU_DTYPE),
                   jax.ShapeDtypeStruct((d, LANES), MXU_DTYPE),
                   jax.ShapeDtypeStruct((d, tail), MXU_DTYPE)),
        compiler_params=pltpu.CompilerParams(
            dimension_semantics=("arbitrary",), vmem_limit_bytes=VMEM_LIMIT),
        name="split_w_in",
    )(w_in)


def _split_w_in_t(w_in, l):
    wt = jnp.swapaxes(w_in, 1, 2)[l]
    wng = jnp.pad(wt[_NG_SRC:_NG_SRC + NSA_GATE_W], ((0, LANES - NSA_GATE_W), (0, 0)))
    return (wt[:_NG_SRC].astype(MXU_DTYPE), wng.astype(MXU_DTYPE),
            wt[_NG_SRC + NSA_GATE_W:].astype(MXU_DTYPE))


def _compress_weights(pe, w1, w2):
    assert N_KV_HEADS == 2
    L = pe.shape[0]
    w1r = w1.reshape(L, 2, CMP_LEN, HEAD_DIM, CMP_HIDDEN)

    def block_diag(w):
        z = jnp.zeros_like(w)
        return jnp.concatenate([jnp.concatenate([w, z], axis=-1),
                                jnp.concatenate([z, w], axis=-1)], axis=-2)

    def expand(w):
        return block_diag(w).reshape(
            L, 2, CMP_STRIDE * KV_W, N_KV_HEADS * CMP_HIDDEN).astype(MXU_DTYPE)

    def expand_pe(p):
        return jnp.concatenate([p, p], axis=-1).reshape(L, 2, 1, CMP_STRIDE * KV_W)

    w2b = block_diag(w2).astype(MXU_DTYPE)
    return (expand_pe(pe[:, :, :CMP_STRIDE]), expand_pe(pe[:, :, CMP_STRIDE:]),
            expand(w1r[:, :, :CMP_STRIDE]), expand(w1r[:, :, CMP_STRIDE:]), w2b)


def _layer_mixers(x, l, p):
    (yacd, qt, kc, vc, ks, vst, kw, vwt, ngt) = _proj_call(
        x, l, *_split_w_in_t(p["w_in"], l), p["norm1_g"], p["gmlp_ln_g"], p["gmlp_ln_b"], p["gmlp_ws"],
        p["bst"], p["qg"], p["kg"], p["conf_conv_w"], p["conf_conv_b"], p["conf_ln_g"],
        p["conf_ln_b"], p["sconv_w"])
    kcmp, vcmpt = _compress_call(kc, vc, l, *p["cmp_w"], p["kg"])
    yb = _attn_call(qt, ngt, kcmp, vcmpt, ks, vst, kw, vwt)
    return yacd, yb


def _prepare(p):
    q = dict(p)
    q["bst"] = jnp.repeat(jnp.swapaxes(p["gmlp_bs"], 1, 2), MIX_W // GMLP_GROUPS, axis=2)
    qg = jnp.tile(p["nsa_q_norm_g"], (1, MIX_W // HEAD_DIM))
    q["qg"] = jnp.broadcast_to(qg[:, :, None], qg.shape + (LANES,))
    q["kg"] = jnp.tile(p["nsa_k_norm_g"], (1, 1, N_KV_HEADS))
    q["cmp_w"] = _compress_weights(p["nsa_cmp_pe"], p["nsa_cmp_w1"], p["nsa_cmp_w2"])
    for name in ("w_branch", "w_gate", "w_out", "w_mlp1", "w_mlp2"):
        q[name] = p[name].astype(MXU_DTYPE)
    for name in ("norm1_g", "gmlp_ln_g", "gmlp_ln_b", "conf_conv_b", "conf_ln_g", "conf_ln_b",
                 "b_gate", "norm2_g"):
        q[name] = p[name][:, None, :]
    return q


def kernel(x, norm1_g, w_in, gmlp_ln_g, gmlp_ln_b, gmlp_ws, gmlp_bs, nsa_q_norm_g, nsa_k_norm_g,
           nsa_cmp_pe, nsa_cmp_w1, nsa_cmp_w2, conf_conv_w, conf_conv_b, conf_ln_g, conf_ln_b,
           sconv_w, w_branch, w_gate, b_gate, w_out, norm2_g, w_mlp1, w_mlp2):
    B, S, D = x.shape
    assert D == D_MODEL and S % min(TM_PROJ, S) == 0 and S % min(T_ATT, S) == 0 and S % CH_ATT == 0
    p = _prepare(dict(
        norm1_g=norm1_g, w_in=w_in, gmlp_ln_g=gmlp_ln_g, gmlp_ln_b=gmlp_ln_b, gmlp_ws=gmlp_ws,
        gmlp_bs=gmlp_bs, nsa_q_norm_g=nsa_q_norm_g, nsa_k_norm_g=nsa_k_norm_g,
        nsa_cmp_pe=nsa_cmp_pe, nsa_cmp_w1=nsa_cmp_w1, nsa_cmp_w2=nsa_cmp_w2,
        conf_conv_w=conf_conv_w, conf_conv_b=conf_conv_b, conf_ln_g=conf_ln_g, conf_ln_b=conf_ln_b,
        sconv_w=sconv_w, w_branch=w_branch, w_gate=w_gate, b_gate=b_gate, w_out=w_out,
        norm2_g=norm2_g, w_mlp1=w_mlp1, w_mlp2=w_mlp2))
    for l in range(norm1_g.shape[0]):
        yacd, yb = _layer_mixers(x, l, p)
        x = _merge_mlp_call(x, l, yacd, yb, p["norm1_g"], p["w_branch"], p["w_gate"], p["b_gate"],
                            p["w_out"], p["norm2_g"], p["w_mlp1"], p["w_mlp2"])
    return x
```

```python
import functools

import jax
import jax.numpy as jnp
from jax import lax
from jax.experimental import pallas as pl
from jax.experimental.pallas import tpu as pltpu

D_MODEL = 1024
MIX_W = 256
HEAD_DIM = 64
N_KV_HEADS = 2
Q_PER_KV = 2
KV_W = N_KV_HEADS * HEAD_DIM
GMLP_GROUPS = 4
GMLP_CHUNK = 128
CMP_LEN = 32
CMP_STRIDE = 16
CMP_HIDDEN = 128
SEL_BLOCK = 64
SEL_TOPK = 8
N_LOCAL_BLOCKS = 2
WINDOW = 512
CONF_KERNEL = 31
SCONV_KERNEL = 3
D_FF = 4 * D_MODEL
NSA_GATE_W = 12
NEG_INF = -1e30

LANES = 128
SUBLANES = 8
MXU_DTYPE = jnp.bfloat16
VMEM_LIMIT = 56 * 1024 * 1024

_C_GU, _C_GV, _C_Q = 0, 256, 512
_C_KC, _C_VC, _C_KS, _C_VS, _C_KW, _C_VW = 768, 896, 1024, 1152, 1280, 1408
_C_NG = 1536
_C_CA, _C_CB, _C_SB, _C_SC, _C_SH = 1664, 1920, 2176, 2432, 2688
_NG_SRC = 1536

TM_PROJ = 1024
TM_MERGE = 512
T_ATT = 256
CH_ATT = 256
LOG2E = 1.4426950408889634
V_AUG_ROWS = 16
CMP_PAD = 8
CONV_HALO = 32
SCONV_HALO = 8


def _rms_rows(x, g, eps=1e-6):
    return x * lax.rsqrt(jnp.mean(x * x, axis=-1, keepdims=True) + eps) * g


def _layernorm_rows(x, g, b, eps=1e-5):
    mu = jnp.mean(x, axis=-1, keepdims=True)
    xc = x - mu
    var = jnp.mean(xc * xc, axis=-1, keepdims=True)
    return xc * lax.rsqrt(var + eps) * g + b


def _head_rms_rows(x, g, eps=1e-6):
    n_heads = x.shape[-1] // HEAD_DIM
    head = lax.broadcasted_iota(jnp.int32, x.shape, 1) >> 6
    xx = x * x
    scale = jnp.zeros_like(x)
    for h in range(n_heads):
        ms = jnp.sum(jnp.where(head == h, xx, 0.0), axis=-1, keepdims=True) * (1.0 / HEAD_DIM)
        scale = jnp.where(head == h, lax.rsqrt(ms + eps), scale)
    return x * scale * g


def _layer_spec(a, l):
    return pl.BlockSpec((1,) + a.shape[1:], lambda *_: (l,) + (0,) * (a.ndim - 1))


def _proj_kernel(x_ref, g1_ref, wa_ref, wng_ref, wb_ref, lng_ref, lnb_ref, ws_ref, bst_ref, qg_ref,
                 kg_ref, cw_ref, cbias_ref, clg_ref, clb_ref, sw_ref,
                 yacd_ref, qt_ref, kc_ref, vc_ref, ks_ref, vst_ref, kw_ref, vwt_ref, ngt_ref,
                 zbuf, cbuf, *, tm, ch):
    @pl.when(pl.program_id(1) == 0)
    def _():
        zbuf[...] = jnp.zeros((CONV_HALO, MIX_W), jnp.float32)
        cbuf[...] = jnp.zeros((SCONV_HALO, MIX_W), jnp.float32)

    x = x_ref[0]
    xn = _rms_rows(x, g1_ref[0]).astype(MXU_DTYPE)

    def proj(lo, width):
        if lo < _C_NG:
            w = wa_ref[lo:lo + width, :]
        elif lo == _C_NG:
            w = wng_ref[...]
        else:
            w = wb_ref[lo - _C_CA:lo - _C_CA + width, :]
        return lax.dot_general(xn, w, (((1,), (1,)), ((), ())), preferred_element_type=jnp.float32)

    def causal_conv(buf, w, width, halo):
        offsets = [halo - (width - 1) + j for j in range(width)]
        out = jnp.zeros((tm, MIX_W), jnp.float32)
        for b in range(SUBLANES):
            taps = [j for j in range(width) if offsets[j] % SUBLANES == b]
            if not taps:
                continue
            rows = tm if b == 0 else tm + SUBLANES
            part = jnp.zeros((rows, MIX_W), jnp.float32)
            for j in taps:
                part = part + w[j:j + 1] * buf[offsets[j] - b:offsets[j] - b + rows]
            out = out + part[b:b + tm]
            yield out


    z = proj(_C_CA, MIX_W) * jax.nn.sigmoid(proj(_C_CB, MIX_W))
    conv = causal_conv(jnp.concatenate([zbuf[...], z], axis=0), cw_ref[0], CONF_KERNEL, CONV_HALO)
    zbuf[...] = z[tm - CONV_HALO:tm]

    def emit_q():
        qt = proj(_C_Q, MIX_W).T
        gq = jnp.concatenate([qg_ref[0]] * (tm // LANES), axis=1)
        pieces = []
        for h in range(MIX_W // HEAD_DIM):
            blk = qt[h * HEAD_DIM:(h + 1) * HEAD_DIM]
            ms = jnp.sum(blk * blk, axis=0, keepdims=True) * (1.0 / HEAD_DIM)
            pieces.append(blk * lax.rsqrt(ms + 1e-6))
        qn = (jnp.concatenate(pieces, axis=0) * gq * (HEAD_DIM ** -0.5 * LOG2E)).astype(qt_ref.dtype)
        for c in range(tm // ch):
            qt_ref[0, c] = qn[:, c * ch:(c + 1) * ch]

    def emit_raw(dst_ref, col):
        dst_ref[0] = proj(col, KV_W)

    def emit_key(dst_ref, col, branch):
        dst_ref[0] = _head_rms_rows(proj(col, KV_W), kg_ref[0, branch:branch + 1, :]).astype(dst_ref.dtype)

    def emit_value_t(dst_ref, col):
        vt = proj(col, KV_W).T
        for c in range(tm // ch):
            dst_ref[0, c] = vt[:, c * ch:(c + 1) * ch].astype(dst_ref.dtype)

    def emit_gates():
        ngt = proj(_C_NG, LANES).T[0:16, :]
        for c in range(tm // ch):
            ngt_ref[0, c] = ngt[:, c * ch:(c + 1) * ch]

    held = {}

    def hold(name, col):
        held[name] = proj(col, MIX_W)

    slices = [
        [lambda: emit_raw(kc_ref, _C_KC), lambda: emit_raw(vc_ref, _C_VC)],
        [lambda: emit_key(ks_ref, _C_KS, 1), lambda: emit_key(kw_ref, _C_KW, 2)],
        [lambda: emit_value_t(vst_ref, _C_VS), lambda: emit_value_t(vwt_ref, _C_VW)],
        [emit_q],
        [emit_gates, lambda: hold("sb", _C_SB)],
        [lambda: hold("sc", _C_SC), lambda: hold("sh", _C_SH)],
        [lambda: hold("gu", _C_GU)],
        [lambda: hold("gv", _C_GV)],
    ]
    acc = None
    for tasks in slices:
        for task in tasks:
            task()
        acc = next(conv, acc)
    for acc in conv:
        pass
    yacd_ref[0, :, MIX_W:2 * MIX_W] = jax.nn.silu(_layernorm_rows(
        acc + cbias_ref[0], clg_ref[0], clb_ref[0])).astype(yacd_ref.dtype)

    sc = held["sc"] * held["sh"]
    for acc in causal_conv(jnp.concatenate([cbuf[...], sc], axis=0), sw_ref[0], SCONV_KERNEL, SCONV_HALO):
        pass
    cbuf[...] = sc[tm - SCONV_HALO:tm]
    yacd_ref[0, :, 2 * MIX_W:3 * MIX_W] = (held["sb"] * acc).astype(yacd_ref.dtype)

    u = jax.nn.gelu(held["gu"])
    v = jax.nn.gelu(held["gv"])
    v = _layernorm_rows(v, lng_ref[0], lnb_ref[0]).astype(MXU_DTYPE)
    tri = (lax.broadcasted_iota(jnp.int32, (GMLP_CHUNK, GMLP_CHUNK), 0)
           >= lax.broadcasted_iota(jnp.int32, (GMLP_CHUNK, GMLP_CHUNK), 1))
    grp = lax.broadcasted_iota(jnp.int32, (GMLP_CHUNK, MIX_W), 1) >> 6
    wsm = [jnp.where(tri, ws_ref[0, g], 0.0).astype(MXU_DTYPE) for g in range(GMLP_GROUPS)]
    for c in range(tm // GMLP_CHUNK):
        rows = slice(c * GMLP_CHUNK, (c + 1) * GMLP_CHUNK)
        vch = v[rows]
        mixed = jnp.zeros((GMLP_CHUNK, MIX_W), jnp.float32)
        for g in range(GMLP_GROUPS):
            r = jnp.dot(wsm[g], vch, preferred_element_type=jnp.float32)
            mixed = jnp.where(grp == g, r, mixed)
        yacd_ref[0, rows, 0:MIX_W] = (u[rows] * (mixed + bst_ref[0])).astype(yacd_ref.dtype)


def _proj_call(x, l, wa, wng, wb, g1, lng, lnb, ws, bst, qg, kg, cw, cbias, clg, clb, sw):
    B, S, D = x.shape
    tm, ch = min(TM_PROJ, S), CH_ATT
    full = lambda a: pl.BlockSpec(a.shape, lambda b, i: (0,) * a.ndim)
    seq = lambda width: pl.BlockSpec((1, tm, width), lambda b, i: (b, i, 0))
    stacked = (lng, lnb, ws, bst, qg, kg, cw, cbias, clg, clb, sw)
    ins = (g1, wa, wng, wb) + stacked
    in_specs = ([seq(D), _layer_spec(g1, l), full(wa), full(wng), full(wb)]
                + [_layer_spec(a, l) for a in stacked])
    out_shape = (
        jax.ShapeDtypeStruct((B, S, 3 * MIX_W), MXU_DTYPE),
        jax.ShapeDtypeStruct((B, S // ch, MIX_W, ch), MXU_DTYPE),
        jax.ShapeDtypeStruct((B, S, KV_W), jnp.float32),
        jax.ShapeDtypeStruct((B, S, KV_W), jnp.float32),
        jax.ShapeDtypeStruct((B, S, KV_W), MXU_DTYPE),
        jax.ShapeDtypeStruct((B, S // ch, KV_W, ch), MXU_DTYPE),
        jax.ShapeDtypeStruct((B, S, KV_W), MXU_DTYPE),
        jax.ShapeDtypeStruct((B, S // ch, KV_W, ch), MXU_DTYPE),
        jax.ShapeDtypeStruct((B, S // ch, 16, ch), jnp.float32),
    )
    vt_spec = pl.BlockSpec((1, tm // ch, KV_W, ch), lambda b, i: (b, i, 0, 0))
    out_specs = (
        seq(3 * MIX_W),
        pl.BlockSpec((1, tm // ch, MIX_W, ch), lambda b, i: (b, i, 0, 0)),
        seq(KV_W), seq(KV_W), seq(KV_W), vt_spec, seq(KV_W), vt_spec,
        pl.BlockSpec((1, tm // ch, 16, ch), lambda b, i: (b, i, 0, 0)),
    )
    return pl.pallas_call(
        functools.partial(_proj_kernel, tm=tm, ch=ch),
        grid=(B, S // tm),
        in_specs=in_specs,
        out_specs=out_specs,
        out_shape=out_shape,
        scratch_shapes=[pltpu.VMEM((CONV_HALO, MIX_W), jnp.float32),
                        pltpu.VMEM((SCONV_HALO, MIX_W), jnp.float32)],
        compiler_params=pltpu.CompilerParams(
            dimension_semantics=("arbitrary", "arbitrary"), vmem_limit_bytes=VMEM_LIMIT),
        name="proj_mixers",
    )(x, *ins)


def _compress_kernel(kc_ref, vc_ref, pet_ref, peb_ref, wt_ref, wb_ref, w2_ref, kg_ref,
                     kcmp_ref, vcmpt_ref):
    nc = kc_ref.shape[1] // CMP_STRIDE

    def compress(src_ref, idx):
        chunks = jnp.concatenate(
            [src_ref[0, pl.ds(l, nc, stride=CMP_STRIDE), :] for l in range(CMP_STRIDE)], axis=1)
        a1 = jnp.dot((chunks + pet_ref[0, idx]).astype(MXU_DTYPE), wt_ref[0, idx],
                     preferred_element_type=jnp.float32)
        a2 = jnp.dot((chunks + peb_ref[0, idx]).astype(MXU_DTYPE), wb_ref[0, idx],
                     preferred_element_type=jnp.float32)
        hid = jax.nn.gelu(a1 + pltpu.roll(a2, shift=nc - 1, axis=0))
        return jnp.dot(hid.astype(MXU_DTYPE), w2_ref[0, idx], preferred_element_type=jnp.float32)

    kcmp_ref[0] = _head_rms_rows(compress(kc_ref, 0), kg_ref[0, 0:1, :]).astype(kcmp_ref.dtype)
    vcmpt_ref[0] = compress(vc_ref, 1).T.astype(vcmpt_ref.dtype)


def _compress_call(kc, vc, l, pet, peb, wt, wb, w2, kg):
    B, S, _ = kc.shape
    nc = S // CMP_STRIDE
    per_b = lambda r, c: pl.BlockSpec((1, r, c), lambda b: (b, 0, 0))
    ins = (pet, peb, wt, wb, w2, kg)
    return pl.pallas_call(
        _compress_kernel,
        grid=(B,),
        in_specs=[per_b(S, KV_W)] * 2 + [_layer_spec(a, l) for a in ins],
        out_specs=(per_b(nc, KV_W), per_b(KV_W, nc)),
        out_shape=(jax.ShapeDtypeStruct((B, nc, KV_W), MXU_DTYPE),
                   jax.ShapeDtypeStruct((B, KV_W, nc), MXU_DTYPE)),
        compiler_params=pltpu.CompilerParams(
            dimension_semantics=("arbitrary",), vmem_limit_bytes=VMEM_LIMIT),
        name="nsa_compress",
    )(kc, vc, *ins)


def _attn_kernel(*refs, t, seq_len):
    def tile(i, carry):
        _attn_tile(i, *refs, t=t, seq_len=seq_len)
        return carry

    lax.fori_loop(0, seq_len // t, tile, 0)


def _attn_tile(i, qt_ref, ngt_ref, kcmp_ref, vcmpt_ref, ks_ref, vst_ref, kw_ref, vwt_ref,
               y_ref, selb_scr, pcmp_scr, ssel_scr, stail_scr, *, t, seq_len):
    nc = seq_len // CMP_STRIDE
    n_cmp = nc - 1
    n_sel = seq_len // SEL_BLOCK
    k_top = min(SEL_TOPK, n_sel)
    bpc = t // SEL_BLOCK
    cmp_per_sel = SEL_BLOCK // CMP_STRIDE
    nw = WINDOW // t
    t2 = 2 * t
    hq = t // 2
    f32 = jnp.float32

    def lanes_of(a0, a1):
        return jnp.concatenate([a0[:, :hq], a1[:, :hq], a0[:, hq:], a1[:, hq:]], axis=1)

    def both(a):
        return lanes_of(a, a)

    def keep_bias(cond):
        return jnp.where(cond, 0.0, NEG_INF).astype(f32)

    qt = qt_ref[0, i]
    qpads = []
    for h in range(N_KV_HEADS):
        qh = lanes_of(qt[(2 * h) * HEAD_DIM:(2 * h + 1) * HEAD_DIM],
                      qt[(2 * h + 1) * HEAD_DIM:(2 * h + 2) * HEAD_DIM])
        zero = jnp.zeros_like(qh)
        qpads.append(jnp.concatenate([qh, zero] if h == 0 else [zero, qh], axis=0))

    n_i = lax.broadcasted_iota(jnp.int32, (nc, t), 0)
    qpos_c = i * t + lax.broadcasted_iota(jnp.int32, (nc, t), 1)
    cmp_ok = (n_i * CMP_STRIDE + (CMP_LEN - 1) <= qpos_c) & (n_i < n_cmp)
    cmp_bias = both(keep_bias(cmp_ok))
    cmp_keep = both(jnp.where(cmp_ok, 1.0, 0.0).astype(f32))

    blk = lax.broadcasted_iota(jnp.int32, (n_sel, t), 0)
    cur = (i * t + lax.broadcasted_iota(jnp.int32, (n_sel, t), 1)) >> 6
    dist = cur - blk
    causal = dist >= 0
    forced = (blk == 0) | (causal & (dist < N_LOCAL_BLOCKS))
    blk8 = lax.broadcasted_iota(jnp.int32, (8, t), 0)

    kcmp = kcmp_ref[0]
    pcmp_scr[:, 0:CMP_PAD, :] = jnp.zeros((t // LANES, CMP_PAD, LANES), f32)
    o_cmp = []
    for h in range(N_KV_HEADS):
        sm = jnp.dot(kcmp, qpads[h], preferred_element_type=f32) + cmp_bias
        e = jnp.exp2(sm - jnp.max(sm, axis=0, keepdims=True)) * cmp_keep
        den = jnp.sum(e, axis=0, keepdims=True)
        p_cmp = e * (1.0 / jnp.where(den > 0.0, den, 1.0))
        o_cmp.append(jnp.dot(vcmpt_ref[0, h * HEAD_DIM:(h + 1) * HEAD_DIM, :],
                             p_cmp.astype(MXU_DTYPE), preferred_element_type=f32))

        p_pair = jnp.concatenate([p_cmp[:, 0:hq] + p_cmp[:, hq:t],
                                  p_cmp[:, t:t + hq] + p_cmp[:, t + hq:t2]], axis=1)
        slabs = []
        for lb in range(t // LANES):
            pcmp_scr[lb, CMP_PAD:CMP_PAD + nc, :] = p_pair[:, lb * LANES:(lb + 1) * LANES]
            slabs.append(functools.reduce(lambda a, b: a + b, [
                pcmp_scr[lb, pl.ds(CMP_PAD + off, n_sel, stride=cmp_per_sel), :]
                for off in range(1 - CMP_LEN // CMP_STRIDE, cmp_per_sel)]))
        p_slc = jnp.concatenate(slabs, axis=1)
        score = jnp.where(forced, jnp.inf, jnp.where(causal, p_slc, -jnp.inf))
        groups = [score[8 * g:8 * g + 8] for g in range(n_sel // 8)]
        rank = [jnp.zeros((8, t), f32) for _ in groups]
        for r in range(n_sel):
            row = score[r:r + 1, :]
            for g, sg in enumerate(groups):
                if 8 * g > r:
                    beats = row >= sg
                elif 8 * g + 7 <= r:
                    beats = row > sg
                else:
                    beats = (row > sg) | ((row == sg) & (blk8 > r - 8 * g))
                rank[g] = rank[g] + jnp.where(beats, 1.0, 0.0)
        selected = (jnp.concatenate(rank, axis=0) < k_top) & causal
        selb_scr[h] = both(keep_bias(selected))

    r_i = lax.broadcasted_iota(jnp.int32, (t, t), 0)
    q_i = lax.broadcasted_iota(jnp.int32, (t, t), 1)
    diag_bias = both(keep_bias(r_i <= q_i))
    ones_rows = jnp.where(lax.broadcasted_iota(jnp.int32, (V_AUG_ROWS, t), 0) == 0,
                          1.0, 0.0).astype(MXU_DTYPE)

    def k_chunk(k_ref, c):
        return k_ref[0, pl.ds(pl.multiple_of(c * t, t), t), :]

    def v_aug(vt_ref, c, h):
        return jnp.concatenate([vt_ref[0, c, h * HEAD_DIM:(h + 1) * HEAD_DIM, :], ones_rows], axis=0)

    def colmax8(s):
        return jnp.max(s.reshape(s.shape[0] // 8, 8, s.shape[1]), axis=0)

    def weighted_values(s, m, vaug):
        p = jnp.exp2(s - m).astype(MXU_DTYPE)
        return jnp.dot(vaug, p, preferred_element_type=f32)

    def normalise(acc):
        return acc[0:HEAD_DIM] * (1.0 / acc[HEAD_DIM:HEAD_DIM + 1])

    def on_lanes(full, part, lanes, op):
        pieces = [full[:, :lanes.start]] if lanes.start else []
        pieces.append(op(full[:, lanes], part))
        if lanes.stop < full.shape[1]:
            pieces.append(full[:, lanes.stop:])
        return pieces[0] if len(pieces) == 1 else jnp.concatenate(pieces, axis=1)

    top, bot = slice(0, hq), slice(hq, t)
    lo_q, hi_q, all_q = slice(0, t), slice(t, t2), slice(0, t2)
    full_pieces = ((slice(0, t), all_q),)
    diag_pieces = ((top, all_q), (bot, hi_q))
    first_pieces = ((top, lo_q), (bot, all_q))

    heads = range(N_KV_HEADS)
    neg8 = jnp.full((8, t2), NEG_INF, f32)

    def sel_scores(c, k, h, rows, lanes, extra):
        s = jnp.dot(k[rows], qpads[h][:, lanes], preferred_element_type=f32)
        parts = []
        for r in range((rows.stop - rows.start) // SEL_BLOCK):
            blk_rows = slice(r * SEL_BLOCK, (r + 1) * SEL_BLOCK)
            bias = selb_scr[h, pl.ds(c * bpc + rows.start // SEL_BLOCK + r, 1), lanes]
            if extra is not None:
                bias = bias + (extra[blk_rows] if extra.shape[0] > 1 else extra)
            parts.append(s[blk_rows] + bias)
        return jnp.concatenate(parts, axis=0)

    def sel_pass1(pair, pm):
        pm = list(pm)
        for u in range(2):
            c = 2 * pair + u
            k = k_chunk(ks_ref, c)
            for h in heads:
                s = sel_scores(c, k, h, slice(0, t), all_q, None)
                ssel_scr[h, c] = s
                pm[h] = jnp.maximum(pm[h], colmax8(s))
        return tuple(pm)

    n_pairs = i >> 1
    pm = lax.fori_loop(0, n_pairs, sel_pass1, (neg8, neg8))
    c_left = jnp.maximum(i - 1, 0)
    left_bias = jnp.full((1, t2), jnp.where((i & 1) == 1, 0.0, NEG_INF), f32)
    tail = ((0, c_left, full_pieces, (left_bias,)),
            (1, i, diag_pieces, tuple(diag_bias[rows, lanes] for rows, lanes in diag_pieces)))
    pm = list(pm)
    for u, c, pieces, extras in tail:
        k = k_chunk(ks_ref, c)
        for (rows, lanes), extra in zip(pieces, extras):
            for h in heads:
                s = sel_scores(c, k, h, rows, lanes, extra)
                stail_scr[h, u, rows, lanes] = s
                pm[h] = on_lanes(pm[h], colmax8(s), lanes, jnp.maximum)
    m_sel = [jnp.max(pm[h], axis=0, keepdims=True) for h in heads]

    def sel_pass2(pair, accs):
        accs = list(accs)
        for u in range(2):
            c = 2 * pair + u
            for h in heads:
                accs[h] = accs[h] + weighted_values(ssel_scr[h, c], m_sel[h], v_aug(vst_ref, c, h))
        return tuple(accs)

    zero_acc = jnp.zeros((HEAD_DIM + V_AUG_ROWS, t2), f32)
    accs = list(lax.fori_loop(0, n_pairs, sel_pass2, (zero_acc, zero_acc)))
    for u, c, pieces, _ in tail:
        for rows, lanes in pieces:
            for h in heads:
                part = weighted_values(stail_scr[h, u, rows, lanes], m_sel[h][:, lanes],
                                       v_aug(vst_ref, c, h)[:, rows])
                accs[h] = on_lanes(accs[h], part, lanes, lambda a, b: a + b)
    o_sel = [normalise(accs[h]) for h in heads]

    first_bias = both(keep_bias(r_i > q_i))
    work = []
    for cc in range(nw + 1):
        ci = i - nw + cc
        cic = jnp.maximum(ci, 0)
        exists = jnp.where(ci >= 0, 0.0, NEG_INF).astype(f32)
        k = k_chunk(kw_ref, cic)
        if cc == nw:
            work += [(cic, rows, lanes, diag_bias[rows, lanes], k) for rows, lanes in diag_pieces]
        elif cc == 0:
            work += [(cic, rows, lanes, first_bias[rows, lanes] + exists, k)
                     for rows, lanes in first_pieces]
        else:
            work += [(cic, rows, lanes, jnp.full((1, t2), exists, f32), k)
                     for rows, lanes in full_pieces]
    o_win = []
    for h in heads:
        ss = [jnp.dot(k[rows], qpads[h][:, lanes], preferred_element_type=f32) + b
              for _, rows, lanes, b, k in work]
        pm = neg8
        for (_, _, lanes, _, _), s in zip(work, ss):
            pm = on_lanes(pm, colmax8(s), lanes, jnp.maximum)
        m = jnp.max(pm, axis=0, keepdims=True)
        acc = zero_acc
        for (cic, rows, lanes, _, _), s in zip(work, ss):
            part = weighted_values(s, m[:, lanes], v_aug(vwt_ref, cic, h)[:, rows])
            acc = on_lanes(acc, part, lanes, lambda a, b: a + b)
        o_win.append(normalise(acc))

    sig = jax.nn.sigmoid(ngt_ref[0, i])
    out_rows = []
    for h in range(N_KV_HEADS):
        def gate(r):
            a = (2 * h) * 3 + r
            b = (2 * h + 1) * 3 + r
            return lanes_of(sig[a:a + 1, :], sig[b:b + 1, :])

        o = gate(0) * o_cmp[h] + gate(1) * o_sel[h] + gate(2) * o_win[h]
        out_rows += [jnp.concatenate([o[:, 0:hq], o[:, t:t + hq]], axis=1),
                     jnp.concatenate([o[:, hq:t], o[:, t + hq:t2]], axis=1)]
    y_ref[0, pl.ds(pl.multiple_of(i * t, t), t), :] = (
        jnp.concatenate(out_rows, axis=0).T.astype(y_ref.dtype))


def _attn_call(qt, ngt, kcmp, vcmpt, ks, vst, kw, vwt):
    B, n_tiles, _, t = qt.shape
    S = n_tiles * t
    assert t == T_ATT == CH_ATT and WINDOW % t == 0 and (S // SEL_BLOCK) % 8 == 0
    assert SEL_BLOCK % CMP_STRIDE == 0 and CMP_LEN % CMP_STRIDE == 0 and CMP_LEN // CMP_STRIDE <= CMP_PAD
    nc = S // CMP_STRIDE
    n_sel = S // SEL_BLOCK
    per_b = lambda a: pl.BlockSpec((1,) + a.shape[1:], lambda b: (b,) + (0,) * (a.ndim - 1))
    ins = (qt, ngt, kcmp, vcmpt, ks, vst, kw, vwt)
    return pl.pallas_call(
        functools.partial(_attn_kernel, t=t, seq_len=S),
        grid=(B,),
        in_specs=[per_b(a) for a in ins],
        out_specs=pl.BlockSpec((1, S, MIX_W), lambda b: (b, 0, 0)),
        out_shape=jax.ShapeDtypeStruct((B, S, MIX_W), MXU_DTYPE),
        scratch_shapes=[
            pltpu.VMEM((N_KV_HEADS, n_sel, 2 * t), jnp.float32),
            pltpu.VMEM((t // LANES, CMP_PAD + nc, LANES), jnp.float32),
            pltpu.VMEM((N_KV_HEADS, S // t, t, 2 * t), jnp.float32),
            pltpu.VMEM((N_KV_HEADS, 2, t, 2 * t), jnp.float32),
        ],
        compiler_params=pltpu.CompilerParams(
            dimension_semantics=("arbitrary",), vmem_limit_bytes=VMEM_LIMIT),
        name="nsa_attention",
    )(*ins)


def _merge_mlp_kernel(x_ref, yacd_ref, yb_ref, g1_ref, wbr_ref, wg_ref, bg_ref, wo_ref,
                      g2_ref, w1_ref, w2_ref, o_ref):
    x = x_ref[0]
    xn = _rms_rows(x, g1_ref[0]).astype(MXU_DTYPE)
    ys = (yacd_ref[0, :, 0:MIX_W], yb_ref[0], yacd_ref[0, :, MIX_W:2 * MIX_W],
          yacd_ref[0, :, 2 * MIX_W:3 * MIX_W])
    mixed = jnp.zeros(x.shape, jnp.float32)
    for n, y in enumerate(ys):
        cols = slice(n * D_MODEL, (n + 1) * D_MODEL)
        proj = jnp.dot(y, wbr_ref[0, n], preferred_element_type=jnp.float32)
        logits = jnp.dot(xn, wg_ref[0, :, cols], preferred_element_type=jnp.float32) + bg_ref[0, :, cols]
        mixed = mixed + jax.nn.sigmoid(logits) * proj
    h = x + jnp.dot(mixed.astype(MXU_DTYPE), wo_ref[0], preferred_element_type=jnp.float32)

    hn = _rms_rows(h, g2_ref[0]).astype(MXU_DTYPE)
    acc = h
    for c in range(D_FF // D_MODEL):
        cols = slice(c * D_MODEL, (c + 1) * D_MODEL)
        hid = jnp.dot(hn, w1_ref[0, :, cols], preferred_element_type=jnp.float32)
        hid = jnp.square(jnp.maximum(hid, 0.0)).astype(MXU_DTYPE)
        acc = acc + jnp.dot(hid, w2_ref[0, cols, :], preferred_element_type=jnp.float32)
    o_ref[0] = acc


def _merge_mlp_call(x, l, yacd, yb, g1, wbr, wg, bg, wo, g2, w1, w2):
    B, S, d = x.shape
    tm = min(TM_MERGE, S)
    rows = lambda w: pl.BlockSpec((1, tm, w), lambda b, i: (b, i, 0))
    stacked = (g1, wbr, wg, bg, wo, g2, w1, w2)
    return pl.pallas_call(
        _merge_mlp_kernel,
        grid=(B, S // tm),
        in_specs=[rows(d), rows(3 * MIX_W), rows(MIX_W)] + [_layer_spec(a, l) for a in stacked],
        out_specs=rows(d),
        out_shape=jax.ShapeDtypeStruct((B, S, d), jnp.float32),
        compiler_params=pltpu.CompilerParams(
            dimension_semantics=("arbitrary", "arbitrary"), vmem_limit_bytes=VMEM_LIMIT),
        name="merge_mlp",
    )(x, yacd, yb, *stacked)


def _split_w_in_t(w_in, l):
    wt = jnp.swapaxes(w_in, 1, 2)[l]
    wng = jnp.pad(wt[_NG_SRC:_NG_SRC + NSA_GATE_W], ((0, LANES - NSA_GATE_W), (0, 0)))
    return (wt[:_NG_SRC].astype(MXU_DTYPE), wng.astype(MXU_DTYPE),
            wt[_NG_SRC + NSA_GATE_W:].astype(MXU_DTYPE))


def _compress_weights(pe, w1, w2):
    assert N_KV_HEADS == 2
    L = pe.shape[0]
    w1r = w1.reshape(L, 2, CMP_LEN, HEAD_DIM, CMP_HIDDEN)

    def block_diag(w):
        z = jnp.zeros_like(w)
        return jnp.concatenate([jnp.concatenate([w, z], axis=-1),
                                jnp.concatenate([z, w], axis=-1)], axis=-2)

    def expand(w):
        return block_diag(w).reshape(
            L, 2, CMP_STRIDE * KV_W, N_KV_HEADS * CMP_HIDDEN).astype(MXU_DTYPE)

    def expand_pe(p):
        return jnp.concatenate([p, p], axis=-1).reshape(L, 2, 1, CMP_STRIDE * KV_W)

    w2b = block_diag(w2).astype(MXU_DTYPE)
    return (expand_pe(pe[:, :, :CMP_STRIDE]), expand_pe(pe[:, :, CMP_STRIDE:]),
            expand(w1r[:, :, :CMP_STRIDE]), expand(w1r[:, :, CMP_STRIDE:]), w2b)


def _layer_mixers(x, l, p):
    (yacd, qt, kc, vc, ks, vst, kw, vwt, ngt) = _proj_call(
        x, l, *_split_w_in_t(p["w_in"], l), p["norm1_g"], p["gmlp_ln_g"], p["gmlp_ln_b"], p["gmlp_ws"],
        p["bst"], p["qg"], p["kg"], p["conf_conv_w"], p["conf_conv_b"], p["conf_ln_g"],
        p["conf_ln_b"], p["sconv_w"])
    kcmp, vcmpt = _compress_call(kc, vc, l, *p["cmp_w"], p["kg"])
    yb = _attn_call(qt, ngt, kcmp, vcmpt, ks, vst, kw, vwt)
    return yacd, yb


def _prepare(p):
    q = dict(p)
    q["bst"] = jnp.repeat(jnp.swapaxes(p["gmlp_bs"], 1, 2), MIX_W // GMLP_GROUPS, axis=2)
    qg = jnp.tile(p["nsa_q_norm_g"], (1, MIX_W // HEAD_DIM))
    q["qg"] = jnp.broadcast_to(qg[:, :, None], qg.shape + (LANES,))
    q["kg"] = jnp.tile(p["nsa_k_norm_g"], (1, 1, N_KV_HEADS))
    q["cmp_w"] = _compress_weights(p["nsa_cmp_pe"], p["nsa_cmp_w1"], p["nsa_cmp_w2"])
    for name in ("w_branch", "w_gate", "w_out", "w_mlp1", "w_mlp2"):
        q[name] = p[name].astype(MXU_DTYPE)
    for name in ("norm1_g", "gmlp_ln_g", "gmlp_ln_b", "conf_conv_b", "conf_ln_g", "conf_ln_b",
                 "b_gate", "norm2_g"):
        q[name] = p[name][:, None, :]
    return q


def kernel(x, norm1_g, w_in, gmlp_ln_g, gmlp_ln_b, gmlp_ws, gmlp_bs, nsa_q_norm_g, nsa_k_norm_g,
           nsa_cmp_pe, nsa_cmp_w1, nsa_cmp_w2, conf_conv_w, conf_conv_b, conf_ln_g, conf_ln_b,
           sconv_w, w_branch, w_gate, b_gate, w_out, norm2_g, w_mlp1, w_mlp2):
    B, S, D = x.shape
    assert D == D_MODEL and S % min(TM_PROJ, S) == 0 and S % min(T_ATT, S) == 0 and S % CH_ATT == 0
    p = _prepare(dict(
        norm1_g=norm1_g, w_in=w_in, gmlp_ln_g=gmlp_ln_g, gmlp_ln_b=gmlp_ln_b, gmlp_ws=gmlp_ws,
        gmlp_bs=gmlp_bs, nsa_q_norm_g=nsa_q_norm_g, nsa_k_norm_g=nsa_k_norm_g,
        nsa_cmp_pe=nsa_cmp_pe, nsa_cmp_w1=nsa_cmp_w1, nsa_cmp_w2=nsa_cmp_w2,
        conf_conv_w=conf_conv_w, conf_conv_b=conf_conv_b, conf_ln_g=conf_ln_g, conf_ln_b=conf_ln_b,
        sconv_w=sconv_w, w_branch=w_branch, w_gate=w_gate, b_gate=b_gate, w_out=w_out,
        norm2_g=norm2_g, w_mlp1=w_mlp1, w_mlp2=w_mlp2))
    for l in range(norm1_g.shape[0]):
        yacd, yb = _layer_mixers(x, l, p)
        x = _merge_mlp_call(x, l, yacd, yb, p["norm1_g"], p["w_branch"], p["w_gate"], p["b_gate"],
                            p["w_out"], p["norm2_g"], p["w_mlp1"], p["w_mlp2"])
    return x
```

```python
import functools

import jax
import jax.numpy as jnp
from jax import lax
from jax.experimental import pallas as pl
from jax.experimental.pallas import tpu as pltpu

D_MODEL = 1024
MIX_W = 256
HEAD_DIM = 64
N_KV_HEADS = 2
Q_PER_KV = 2
KV_W = N_KV_HEADS * HEAD_DIM
GMLP_GROUPS = 4
GMLP_CHUNK = 128
CMP_LEN = 32
CMP_STRIDE = 16
CMP_HIDDEN = 128
SEL_BLOCK = 64
SEL_TOPK = 8
N_LOCAL_BLOCKS = 2
WINDOW = 512
CONF_KERNEL = 31
SCONV_KERNEL = 3
D_FF = 4 * D_MODEL
NSA_GATE_W = 12
NEG_INF = -1e30

LANES = 128
SUBLANES = 8
MXU_DTYPE = jnp.bfloat16
VMEM_LIMIT = 56 * 1024 * 1024

_C_GU, _C_GV, _C_Q = 0, 256, 512
_C_KC, _C_VC, _C_KS, _C_VS, _C_KW, _C_VW = 768, 896, 1024, 1152, 1280, 1408
_C_NG = 1536
_C_CA, _C_CB, _C_SB, _C_SC, _C_SH = 1664, 1920, 2176, 2432, 2688
_NG_SRC = 1536

TM_PROJ = 1024
TM_MERGE = 512
T_ATT = 256
CH_ATT = 256
LOG2E = 1.4426950408889634
V_AUG_ROWS = 16
CMP_PAD = 8
CONV_HALO = 32
SCONV_HALO = 8


def _rms_rows(x, g, eps=1e-6):
    return x * lax.rsqrt(jnp.mean(x * x, axis=-1, keepdims=True) + eps) * g


def _layernorm_rows(x, g, b, eps=1e-5):
    mu = jnp.mean(x, axis=-1, keepdims=True)
    xc = x - mu
    var = jnp.mean(xc * xc, axis=-1, keepdims=True)
    return xc * lax.rsqrt(var + eps) * g + b


def _head_rms_rows(x, g, eps=1e-6):
    n_heads = x.shape[-1] // HEAD_DIM
    head = lax.broadcasted_iota(jnp.int32, x.shape, 1) >> 6
    xx = x * x
    scale = jnp.zeros_like(x)
    for h in range(n_heads):
        ms = jnp.sum(jnp.where(head == h, xx, 0.0), axis=-1, keepdims=True) * (1.0 / HEAD_DIM)
        scale = jnp.where(head == h, lax.rsqrt(ms + eps), scale)
    return x * scale * g


def _layer_spec(a, l):
    return pl.BlockSpec((1,) + a.shape[1:], lambda *_: (l,) + (0,) * (a.ndim - 1))


def _proj_kernel(x_ref, g1_ref, wa_ref, wng_ref, wb_ref, lng_ref, lnb_ref, ws_ref, bst_ref, qg_ref,
                 kg_ref, cw_ref, cbias_ref, clg_ref, clb_ref, sw_ref,
                 yacd_ref, qt_ref, kc_ref, vc_ref, ks_ref, vst_ref, kw_ref, vwt_ref, ngt_ref,
                 zbuf, cbuf, *, tm, ch):
    @pl.when(pl.program_id(1) == 0)
    def _():
        zbuf[...] = jnp.zeros((CONV_HALO, MIX_W), jnp.float32)
        cbuf[...] = jnp.zeros((SCONV_HALO, MIX_W), jnp.float32)

    x = x_ref[0]
    xn = _rms_rows(x, g1_ref[0]).astype(MXU_DTYPE)

    def proj(lo, width):
        if lo < _C_NG:
            w = wa_ref[lo:lo + width, :]
        elif lo == _C_NG:
            w = wng_ref[...]
        else:
            w = wb_ref[lo - _C_CA:lo - _C_CA + width, :]
        return lax.dot_general(xn, w, (((1,), (1,)), ((), ())), preferred_element_type=jnp.float32)

    def causal_conv(buf, w, width, halo):
        offsets = [halo - (width - 1) + j for j in range(width)]
        out = jnp.zeros((tm, MIX_W), jnp.float32)
        for b in range(SUBLANES):
            taps = [j for j in range(width) if offsets[j] % SUBLANES == b]
            if not taps:
                continue
            rows = tm if b == 0 else tm + SUBLANES
            part = jnp.zeros((rows, MIX_W), jnp.float32)
            for j in taps:
                part = part + w[j:j + 1] * buf[offsets[j] - b:offsets[j] - b + rows]
            out = out + part[b:b + tm]
            yield out


    z = proj(_C_CA, MIX_W) * jax.nn.sigmoid(proj(_C_CB, MIX_W))
    conv = causal_conv(jnp.concatenate([zbuf[...], z], axis=0), cw_ref[0], CONF_KERNEL, CONV_HALO)
    zbuf[...] = z[tm - CONV_HALO:tm]

    def emit_q():
        qt = proj(_C_Q, MIX_W).T
        gq = jnp.concatenate([qg_ref[0]] * (tm // LANES), axis=1)
        pieces = []
        for h in range(MIX_W // HEAD_DIM):
            blk = qt[h * HEAD_DIM:(h + 1) * HEAD_DIM]
            ms = jnp.sum(blk * blk, axis=0, keepdims=True) * (1.0 / HEAD_DIM)
            pieces.append(blk * lax.rsqrt(ms + 1e-6))
        qn = (jnp.concatenate(pieces, axis=0) * gq * (HEAD_DIM ** -0.5 * LOG2E)).astype(qt_ref.dtype)
        for c in range(tm // ch):
            qt_ref[0, c] = qn[:, c * ch:(c + 1) * ch]

    def emit_raw(dst_ref, col):
        dst_ref[0] = proj(col, KV_W)

    def emit_key(dst_ref, col, branch):
        dst_ref[0] = _head_rms_rows(proj(col, KV_W), kg_ref[0, branch:branch + 1, :]).astype(dst_ref.dtype)

    def emit_value_t(dst_ref, col):
        vt = proj(col, KV_W).T
        for c in range(tm // ch):
            dst_ref[0, c] = vt[:, c * ch:(c + 1) * ch].astype(dst_ref.dtype)

    def emit_gates():
        ngt = proj(_C_NG, LANES).T[0:16, :]
        for c in range(tm // ch):
            ngt_ref[0, c] = ngt[:, c * ch:(c + 1) * ch]

    held = {}

    def hold(name, col):
        held[name] = proj(col, MIX_W)

    slices = [
        [lambda: emit_raw(kc_ref, _C_KC), lambda: emit_raw(vc_ref, _C_VC)],
        [lambda: emit_key(ks_ref, _C_KS, 1), lambda: emit_key(kw_ref, _C_KW, 2)],
        [lambda: emit_value_t(vst_ref, _C_VS), lambda: emit_value_t(vwt_ref, _C_VW)],
        [emit_q],
        [emit_gates, lambda: hold("sb", _C_SB)],
        [lambda: hold("sc", _C_SC), lambda: hold("sh", _C_SH)],
        [lambda: hold("gu", _C_GU)],
        [lambda: hold("gv", _C_GV)],
    ]
    acc = None
    for tasks in slices:
        for task in tasks:
            task()
        acc = next(conv, acc)
    for acc in conv:
        pass
    yacd_ref[0, :, MIX_W:2 * MIX_W] = jax.nn.silu(_layernorm_rows(
        acc + cbias_ref[0], clg_ref[0], clb_ref[0])).astype(yacd_ref.dtype)

    sc = held["sc"] * held["sh"]
    for acc in causal_conv(jnp.concatenate([cbuf[...], sc], axis=0), sw_ref[0], SCONV_KERNEL, SCONV_HALO):
        pass
    cbuf[...] = sc[tm - SCONV_HALO:tm]
    yacd_ref[0, :, 2 * MIX_W:3 * MIX_W] = (held["sb"] * acc).astype(yacd_ref.dtype)

    u = jax.nn.gelu(held["gu"])
    v = jax.nn.gelu(held["gv"])
    v = _layernorm_rows(v, lng_ref[0], lnb_ref[0]).astype(MXU_DTYPE)
    tri = (lax.broadcasted_iota(jnp.int32, (GMLP_CHUNK, GMLP_CHUNK), 0)
           >= lax.broadcasted_iota(jnp.int32, (GMLP_CHUNK, GMLP_CHUNK), 1))
    grp = lax.broadcasted_iota(jnp.int32, (GMLP_CHUNK, MIX_W), 1) >> 6
    wsm = [jnp.where(tri, ws_ref[0, g], 0.0).astype(MXU_DTYPE) for g in range(GMLP_GROUPS)]
    for c in range(tm // GMLP_CHUNK):
        rows = slice(c * GMLP_CHUNK, (c + 1) * GMLP_CHUNK)
        vch = v[rows]
        mixed = jnp.zeros((GMLP_CHUNK, MIX_W), jnp.float32)
        for g in range(GMLP_GROUPS):
            r = jnp.dot(wsm[g], vch, preferred_element_type=jnp.float32)
            mixed = jnp.where(grp == g, r, mixed)
        yacd_ref[0, rows, 0:MIX_W] = (u[rows] * (mixed + bst_ref[0])).astype(yacd_ref.dtype)


def _proj_call(x, l, wa, wng, wb, g1, lng, lnb, ws, bst, qg, kg, cw, cbias, clg, clb, sw):
    B, S, D = x.shape
    tm, ch = min(TM_PROJ, S), CH_ATT
    full = lambda a: pl.BlockSpec(a.shape, lambda b, i: (0,) * a.ndim)
    seq = lambda width: pl.BlockSpec((1, tm, width), lambda b, i: (b, i, 0))
    stacked = (lng, lnb, ws, bst, qg, kg, cw, cbias, clg, clb, sw)
    ins = (g1, wa, wng, wb) + stacked
    in_specs = ([seq(D), _layer_spec(g1, l), full(wa), full(wng), full(wb)]
                + [_layer_spec(a, l) for a in stacked])
    out_shape = (
        jax.ShapeDtypeStruct((B, S, 3 * MIX_W), MXU_DTYPE),
        jax.ShapeDtypeStruct((B, S // ch, MIX_W, ch), MXU_DTYPE),
        jax.ShapeDtypeStruct((B, S, KV_W), jnp.float32),
        jax.ShapeDtypeStruct((B, S, KV_W), jnp.float32),
        jax.ShapeDtypeStruct((B, S, KV_W), MXU_DTYPE),
        jax.ShapeDtypeStruct((B, S // ch, KV_W, ch), MXU_DTYPE),
        jax.ShapeDtypeStruct((B, S, KV_W), MXU_DTYPE),
        jax.ShapeDtypeStruct((B, S // ch, KV_W, ch), MXU_DTYPE),
        jax.ShapeDtypeStruct((B, S // ch, 16, ch), jnp.float32),
    )
    vt_spec = pl.BlockSpec((1, tm // ch, KV_W, ch), lambda b, i: (b, i, 0, 0))
    out_specs = (
        seq(3 * MIX_W),
        pl.BlockSpec((1, tm // ch, MIX_W, ch), lambda b, i: (b, i, 0, 0)),
        seq(KV_W), seq(KV_W), seq(KV_W), vt_spec, seq(KV_W), vt_spec,
        pl.BlockSpec((1, tm // ch, 16, ch), lambda b, i: (b, i, 0, 0)),
    )
    return pl.pallas_call(
        functools.partial(_proj_kernel, tm=tm, ch=ch),
        grid=(B, S // tm),
        in_specs=in_specs,
        out_specs=out_specs,
        out_shape=out_shape,
        scratch_shapes=[pltpu.VMEM((CONV_HALO, MIX_W), jnp.float32),
                        pltpu.VMEM((SCONV_HALO, MIX_W), jnp.float32)],
        compiler_params=pltpu.CompilerParams(
            dimension_semantics=("arbitrary", "arbitrary"), vmem_limit_bytes=VMEM_LIMIT),
        name="proj_mixers",
    )(x, *ins)


def _compress_kernel(kc_ref, vc_ref, pet_ref, peb_ref, wt_ref, wb_ref, w2_ref, kg_ref,
                     kcmp_ref, vcmpt_ref):
    nc = kc_ref.shape[1] // CMP_STRIDE

    def compress(src_ref, idx):
        chunks = jnp.concatenate(
            [src_ref[0, pl.ds(l, nc, stride=CMP_STRIDE), :] for l in range(CMP_STRIDE)], axis=1)
        a1 = jnp.dot((chunks + pet_ref[0, idx]).astype(MXU_DTYPE), wt_ref[0, idx],
                     preferred_element_type=jnp.float32)
        a2 = jnp.dot((chunks + peb_ref[0, idx]).astype(MXU_DTYPE), wb_ref[0, idx],
                     preferred_element_type=jnp.float32)
        hid = jax.nn.gelu(a1 + pltpu.roll(a2, shift=nc - 1, axis=0))
        return jnp.dot(hid.astype(MXU_DTYPE), w2_ref[0, idx], preferred_element_type=jnp.float32)

    kcmp_ref[0] = _head_rms_rows(compress(kc_ref, 0), kg_ref[0, 0:1, :]).astype(kcmp_ref.dtype)
    vcmpt_ref[0] = compress(vc_ref, 1).T.astype(vcmpt_ref.dtype)


def _compress_call(kc, vc, l, pet, peb, wt, wb, w2, kg):
    B, S, _ = kc.shape
    nc = S // CMP_STRIDE
    per_b = lambda r, c: pl.BlockSpec((1, r, c), lambda b: (b, 0, 0))
    ins = (pet, peb, wt, wb, w2, kg)
    return pl.pallas_call(
        _compress_kernel,
        grid=(B,),
        in_specs=[per_b(S, KV_W)] * 2 + [_layer_spec(a, l) for a in ins],
        out_specs=(per_b(nc, KV_W), per_b(KV_W, nc)),
        out_shape=(jax.ShapeDtypeStruct((B, nc, KV_W), MXU_DTYPE),
                   jax.ShapeDtypeStruct((B, KV_W, nc), MXU_DTYPE)),
        compiler_params=pltpu.CompilerParams(
            dimension_semantics=("arbitrary",), vmem_limit_bytes=VMEM_LIMIT),
        name="nsa_compress",
    )(kc, vc, *ins)


def _attn_kernel(*refs, t, seq_len):
    for i in range(seq_len // t):
        _attn_tile(i, *refs, t=t, seq_len=seq_len)


def _attn_tile(i, qt_ref, ngt_ref, kcmp_ref, vcmpt_ref, ks_ref, vst_ref, kw_ref, vwt_ref,
               y_ref, selb_scr, pcmp_scr, ssel_scr, stail_scr, *, t, seq_len):
    nc = seq_len // CMP_STRIDE
    n_cmp = nc - 1
    n_sel = seq_len // SEL_BLOCK
    k_top = min(SEL_TOPK, n_sel)
    bpc = t // SEL_BLOCK
    cmp_per_sel = SEL_BLOCK // CMP_STRIDE
    nw = WINDOW // t
    t2 = 2 * t
    hq = t // 2
    f32 = jnp.float32
    n_cmp_live = min(nc, (i + 1) * t // CMP_STRIDE)
    n_sel_live = (i + 1) * bpc
    n_groups = -(-n_sel_live // SUBLANES)

    def lanes_of(a0, a1):
        return jnp.concatenate([a0[:, :hq], a1[:, :hq], a0[:, hq:], a1[:, hq:]], axis=1)

    def both(a):
        return lanes_of(a, a)

    def keep_bias(cond):
        return jnp.where(cond, 0.0, NEG_INF).astype(f32)

    qt = qt_ref[0, i]
    qpads = []
    for h in range(N_KV_HEADS):
        qh = lanes_of(qt[(2 * h) * HEAD_DIM:(2 * h + 1) * HEAD_DIM],
                      qt[(2 * h + 1) * HEAD_DIM:(2 * h + 2) * HEAD_DIM])
        zero = jnp.zeros_like(qh)
        qpads.append(jnp.concatenate([qh, zero] if h == 0 else [zero, qh], axis=0))

    n_i = lax.broadcasted_iota(jnp.int32, (n_cmp_live, t), 0)
    qpos_c = i * t + lax.broadcasted_iota(jnp.int32, (n_cmp_live, t), 1)
    cmp_ok = (n_i * CMP_STRIDE + (CMP_LEN - 1) <= qpos_c) & (n_i < n_cmp)
    cmp_bias = both(keep_bias(cmp_ok))
    cmp_keep = both(jnp.where(cmp_ok, 1.0, 0.0).astype(f32))

    blk = lax.broadcasted_iota(jnp.int32, (n_groups * SUBLANES, t), 0)
    cur = (i * t + lax.broadcasted_iota(jnp.int32, (n_groups * SUBLANES, t), 1)) >> 6
    dist = cur - blk
    causal = dist >= 0
    forced = (blk == 0) | (causal & (dist < N_LOCAL_BLOCKS))
    blk8 = lax.broadcasted_iota(jnp.int32, (8, t), 0)

    kcmp = kcmp_ref[0, 0:n_cmp_live, :]
    pcmp_scr[:, 0:CMP_PAD, :] = jnp.zeros((t // LANES, CMP_PAD, LANES), f32)
    o_cmp = []
    for h in range(N_KV_HEADS):
        sm = jnp.dot(kcmp, qpads[h], preferred_element_type=f32) + cmp_bias
        e = jnp.exp2(sm - jnp.max(sm, axis=0, keepdims=True)) * cmp_keep
        den = jnp.sum(e, axis=0, keepdims=True)
        p_cmp = e * (1.0 / jnp.where(den > 0.0, den, 1.0))
        p_all = p_cmp.astype(MXU_DTYPE)
        if n_cmp_live < nc:
            p_all = jnp.concatenate([p_all, jnp.zeros((nc - n_cmp_live, t2), MXU_DTYPE)], axis=0)
        o_cmp.append(jnp.dot(vcmpt_ref[0, h * HEAD_DIM:(h + 1) * HEAD_DIM, :], p_all,
                             preferred_element_type=f32))

        p_pair = jnp.concatenate([p_cmp[:, 0:hq] + p_cmp[:, hq:t],
                                  p_cmp[:, t:t + hq] + p_cmp[:, t + hq:t2]], axis=1)
        slabs = []
        for lb in range(t // LANES):
            pcmp_scr[lb, CMP_PAD:CMP_PAD + n_cmp_live, :] = p_pair[:, lb * LANES:(lb + 1) * LANES]
            slabs.append(functools.reduce(lambda a, b: a + b, [
                pcmp_scr[lb, pl.ds(CMP_PAD + off, n_sel_live, stride=cmp_per_sel), :]
                for off in range(1 - CMP_LEN // CMP_STRIDE, cmp_per_sel)]))
        p_slc = jnp.concatenate(slabs, axis=1)
        if n_sel_live < n_groups * SUBLANES:
            p_slc = jnp.concatenate(
                [p_slc, jnp.zeros((n_groups * SUBLANES - n_sel_live, t), f32)], axis=0)
        score = jnp.where(forced, jnp.inf, jnp.where(causal, p_slc, -jnp.inf))
        groups = [score[8 * g:8 * g + 8] for g in range(n_groups)]
        rank = [jnp.zeros((8, t), f32) for _ in groups]
        for r in range(n_sel_live):
            row = score[r:r + 1, :]
            for g, sg in enumerate(groups):
                if 8 * g > r:
                    beats = row >= sg
                elif 8 * g + 7 <= r:
                    beats = row > sg
                else:
                    beats = (row > sg) | ((row == sg) & (blk8 > r - 8 * g))
                rank[g] = rank[g] + jnp.where(beats, 1.0, 0.0)
        selected = (jnp.concatenate(rank, axis=0) < k_top) & causal
        selb_scr[h, 0:n_groups * SUBLANES, :] = both(keep_bias(selected))

    r_i = lax.broadcasted_iota(jnp.int32, (t, t), 0)
    q_i = lax.broadcasted_iota(jnp.int32, (t, t), 1)
    diag_bias = both(keep_bias(r_i <= q_i))
    ones_rows = jnp.where(lax.broadcasted_iota(jnp.int32, (V_AUG_ROWS, t), 0) == 0,
                          1.0, 0.0).astype(MXU_DTYPE)

    def k_chunk(k_ref, c):
        start = c * t if isinstance(c, int) else pl.multiple_of(c * t, t)
        return k_ref[0, pl.ds(start, t), :]

    def v_aug(vt_ref, c, h):
        return jnp.concatenate([vt_ref[0, c, h * HEAD_DIM:(h + 1) * HEAD_DIM, :], ones_rows], axis=0)

    def colmax8(s):
        return jnp.max(s.reshape(s.shape[0] // 8, 8, s.shape[1]), axis=0)

    def weighted_values(s, m, vaug):
        p = jnp.exp2(s - m).astype(MXU_DTYPE)
        return jnp.dot(vaug, p, preferred_element_type=f32)

    def normalise(acc):
        return acc[0:HEAD_DIM] * (1.0 / acc[HEAD_DIM:HEAD_DIM + 1])

    def on_lanes(full, part, lanes, op):
        pieces = [full[:, :lanes.start]] if lanes.start else []
        pieces.append(op(full[:, lanes], part))
        if lanes.stop < full.shape[1]:
            pieces.append(full[:, lanes.stop:])
        return pieces[0] if len(pieces) == 1 else jnp.concatenate(pieces, axis=1)

    top, bot = slice(0, hq), slice(hq, t)
    lo_q, hi_q, all_q = slice(0, t), slice(t, t2), slice(0, t2)
    full_pieces = ((slice(0, t), all_q),)
    diag_pieces = ((top, all_q), (bot, hi_q))
    first_pieces = ((top, lo_q), (bot, all_q))

    heads = range(N_KV_HEADS)
    neg8 = jnp.full((8, t2), NEG_INF, f32)

    def sel_scores(c, k, h, rows, lanes, extra):
        s = jnp.dot(k[rows], qpads[h][:, lanes], preferred_element_type=f32)
        parts = []
        for r in range((rows.stop - rows.start) // SEL_BLOCK):
            blk_rows = slice(r * SEL_BLOCK, (r + 1) * SEL_BLOCK)
            bias = selb_scr[h, pl.ds(c * bpc + rows.start // SEL_BLOCK + r, 1), lanes]
            if extra is not None:
                bias = bias + (extra[blk_rows] if extra.shape[0] > 1 else extra)
            parts.append(s[blk_rows] + bias)
        return jnp.concatenate(parts, axis=0)

    def sel_pass1(pair, pm):
        pm = list(pm)
        for u in range(2):
            c = 2 * pair + u
            k = k_chunk(ks_ref, c)
            for h in heads:
                s = sel_scores(c, k, h, slice(0, t), all_q, None)
                ssel_scr[h, c] = s
                pm[h] = jnp.maximum(pm[h], colmax8(s))
        return tuple(pm)

    n_pairs = i // 2
    pm = lax.fori_loop(0, n_pairs, sel_pass1, (neg8, neg8)) if n_pairs else (neg8, neg8)
    tail = ((1, i, diag_pieces, tuple(diag_bias[rows, lanes] for rows, lanes in diag_pieces)),)
    if i % 2:
        tail = ((0, i - 1, full_pieces, (None,)),) + tail
    pm = list(pm)
    for u, c, pieces, extras in tail:
        k = k_chunk(ks_ref, c)
        for (rows, lanes), extra in zip(pieces, extras):
            for h in heads:
                s = sel_scores(c, k, h, rows, lanes, extra)
                stail_scr[h, u, rows, lanes] = s
                pm[h] = on_lanes(pm[h], colmax8(s), lanes, jnp.maximum)
    m_sel = [jnp.max(pm[h], axis=0, keepdims=True) for h in heads]

    def sel_pass2(pair, accs):
        accs = list(accs)
        for u in range(2):
            c = 2 * pair + u
            for h in heads:
                accs[h] = accs[h] + weighted_values(ssel_scr[h, c], m_sel[h], v_aug(vst_ref, c, h))
        return tuple(accs)

    zero_acc = jnp.zeros((HEAD_DIM + V_AUG_ROWS, t2), f32)
    accs = (zero_acc, zero_acc)
    accs = list(lax.fori_loop(0, n_pairs, sel_pass2, accs) if n_pairs else accs)
    for u, c, pieces, _ in tail:
        for rows, lanes in pieces:
            for h in heads:
                part = weighted_values(stail_scr[h, u, rows, lanes], m_sel[h][:, lanes],
                                       v_aug(vst_ref, c, h)[:, rows])
                accs[h] = on_lanes(accs[h], part, lanes, lambda a, b: a + b)
    o_sel = [normalise(accs[h]) for h in heads]

    first_bias = both(keep_bias(r_i > q_i))
    work = []
    for cc in range(nw + 1):
        ci = i - nw + cc
        if ci < 0:
            continue
        k = k_chunk(kw_ref, ci)
        if cc == nw:
            work += [(ci, rows, lanes, diag_bias[rows, lanes], k) for rows, lanes in diag_pieces]
        elif cc == 0:
            work += [(ci, rows, lanes, first_bias[rows, lanes], k) for rows, lanes in first_pieces]
        else:
            work += [(ci, rows, lanes, None, k) for rows, lanes in full_pieces]
    o_win = []
    for h in heads:
        ss = []
        for _, rows, lanes, b, k in work:
            s = jnp.dot(k[rows], qpads[h][:, lanes], preferred_element_type=f32)
            ss.append(s if b is None else s + b)
        pm = neg8
        for (_, _, lanes, _, _), s in zip(work, ss):
            pm = on_lanes(pm, colmax8(s), lanes, jnp.maximum)
        m = jnp.max(pm, axis=0, keepdims=True)
        acc = zero_acc
        for (cic, rows, lanes, _, _), s in zip(work, ss):
            part = weighted_values(s, m[:, lanes], v_aug(vwt_ref, cic, h)[:, rows])
            acc = on_lanes(acc, part, lanes, lambda a, b: a + b)
        o_win.append(normalise(acc))

    sig = jax.nn.sigmoid(ngt_ref[0, i])
    out_rows = []
    for h in range(N_KV_HEADS):
        def gate(r):
            a = (2 * h) * 3 + r
            b = (2 * h + 1) * 3 + r
            return lanes_of(sig[a:a + 1, :], sig[b:b + 1, :])

        o = gate(0) * o_cmp[h] + gate(1) * o_sel[h] + gate(2) * o_win[h]
        out_rows += [jnp.concatenate([o[:, 0:hq], o[:, t:t + hq]], axis=1),
                     jnp.concatenate([o[:, hq:t], o[:, t + hq:t2]], axis=1)]
    y_ref[0, i * t:(i + 1) * t, :] = jnp.concatenate(out_rows, axis=0).T.astype(y_ref.dtype)


def _attn_call(qt, ngt, kcmp, vcmpt, ks, vst, kw, vwt):
    B, n_tiles, _, t = qt.shape
    S = n_tiles * t
    assert t == T_ATT == CH_ATT and WINDOW % t == 0 and (S // SEL_BLOCK) % 8 == 0
    assert SEL_BLOCK % CMP_STRIDE == 0 and CMP_LEN % CMP_STRIDE == 0 and CMP_LEN // CMP_STRIDE <= CMP_PAD
    nc = S // CMP_STRIDE
    n_sel = S // SEL_BLOCK
    per_b = lambda a: pl.BlockSpec((1,) + a.shape[1:], lambda b: (b,) + (0,) * (a.ndim - 1))
    ins = (qt, ngt, kcmp, vcmpt, ks, vst, kw, vwt)
    return pl.pallas_call(
        functools.partial(_attn_kernel, t=t, seq_len=S),
        grid=(B,),
        in_specs=[per_b(a) for a in ins],
        out_specs=pl.BlockSpec((1, S, MIX_W), lambda b: (b, 0, 0)),
        out_shape=jax.ShapeDtypeStruct((B, S, MIX_W), MXU_DTYPE),
        scratch_shapes=[
            pltpu.VMEM((N_KV_HEADS, n_sel, 2 * t), jnp.float32),
            pltpu.VMEM((t // LANES, CMP_PAD + nc, LANES), jnp.float32),
            pltpu.VMEM((N_KV_HEADS, S // t, t, 2 * t), jnp.float32),
            pltpu.VMEM((N_KV_HEADS, 2, t, 2 * t), jnp.float32),
        ],
        compiler_params=pltpu.CompilerParams(
            dimension_semantics=("arbitrary",), vmem_limit_bytes=VMEM_LIMIT),
        name="nsa_attention",
    )(*ins)


def _merge_mlp_kernel(x_ref, yacd_ref, yb_ref, g1_ref, wbr_ref, wg_ref, bg_ref, wo_ref,
                      g2_ref, w1_ref, w2_ref, o_ref):
    x = x_ref[0]
    xn = _rms_rows(x, g1_ref[0]).astype(MXU_DTYPE)
    ys = (yacd_ref[0, :, 0:MIX_W], yb_ref[0], yacd_ref[0, :, MIX_W:2 * MIX_W],
          yacd_ref[0, :, 2 * MIX_W:3 * MIX_W])
    mixed = jnp.zeros(x.shape, jnp.float32)
    for n, y in enumerate(ys):
        cols = slice(n * D_MODEL, (n + 1) * D_MODEL)
        proj = jnp.dot(y, wbr_ref[0, n], preferred_element_type=jnp.float32)
        logits = jnp.dot(xn, wg_ref[0, :, cols], preferred_element_type=jnp.float32) + bg_ref[0, :, cols]
        mixed = mixed + jax.nn.sigmoid(logits) * proj
    h = x + jnp.dot(mixed.astype(MXU_DTYPE), wo_ref[0], preferred_element_type=jnp.float32)

    hn = _rms_rows(h, g2_ref[0]).astype(MXU_DTYPE)
    acc = h
    for c in range(D_FF // D_MODEL):
        cols = slice(c * D_MODEL, (c + 1) * D_MODEL)
        hid = jnp.dot(hn, w1_ref[0, :, cols], preferred_element_type=jnp.float32)
        hid = jnp.square(jnp.maximum(hid, 0.0)).astype(MXU_DTYPE)
        acc = acc + jnp.dot(hid, w2_ref[0, cols, :], preferred_element_type=jnp.float32)
    o_ref[0] = acc


def _merge_mlp_call(x, l, yacd, yb, g1, wbr, wg, bg, wo, g2, w1, w2):
    B, S, d = x.shape
    tm = min(TM_MERGE, S)
    rows = lambda w: pl.BlockSpec((1, tm, w), lambda b, i: (b, i, 0))
    stacked = (g1, wbr, wg, bg, wo, g2, w1, w2)
    return pl.pallas_call(
        _merge_mlp_kernel,
        grid=(B, S // tm),
        in_specs=[rows(d), rows(3 * MIX_W), rows(MIX_W)] + [_layer_spec(a, l) for a in stacked],
        out_specs=rows(d),
        out_shape=jax.ShapeDtypeStruct((B, S, d), jnp.float32),
        compiler_params=pltpu.CompilerParams(
            dimension_semantics=("arbitrary", "arbitrary"), vmem_limit_bytes=VMEM_LIMIT),
        name="merge_mlp",
    )(x, yacd, yb, *stacked)


def _split_w_in_t(w_in, l):
    wt = jnp.swapaxes(w_in, 1, 2)[l]
    wng = jnp.pad(wt[_NG_SRC:_NG_SRC + NSA_GATE_W], ((0, LANES - NSA_GATE_W), (0, 0)))
    return (wt[:_NG_SRC].astype(MXU_DTYPE), wng.astype(MXU_DTYPE),
            wt[_NG_SRC + NSA_GATE_W:].astype(MXU_DTYPE))


def _compress_weights(pe, w1, w2):
    assert N_KV_HEADS == 2
    L = pe.shape[0]
    w1r = w1.reshape(L, 2, CMP_LEN, HEAD_DIM, CMP_HIDDEN)

    def block_diag(w):
        z = jnp.zeros_like(w)
        return jnp.concatenate([jnp.concatenate([w, z], axis=-1),
                                jnp.concatenate([z, w], axis=-1)], axis=-2)

    def expand(w):
        return block_diag(w).reshape(
            L, 2, CMP_STRIDE * KV_W, N_KV_HEADS * CMP_HIDDEN).astype(MXU_DTYPE)

    def expand_pe(p):
        return jnp.concatenate([p, p], axis=-1).reshape(L, 2, 1, CMP_STRIDE * KV_W)

    w2b = block_diag(w2).astype(MXU_DTYPE)
    return (expand_pe(pe[:, :, :CMP_STRIDE]), expand_pe(pe[:, :, CMP_STRIDE:]),
            expand(w1r[:, :, :CMP_STRIDE]), expand(w1r[:, :, CMP_STRIDE:]), w2b)


def _layer_mixers(x, l, p):
    (yacd, qt, kc, vc, ks, vst, kw, vwt, ngt) = _proj_call(
        x, l, *_split_w_in_t(p["w_in"], l), p["norm1_g"], p["gmlp_ln_g"], p["gmlp_ln_b"], p["gmlp_ws"],
        p["bst"], p["qg"], p["kg"], p["conf_conv_w"], p["conf_conv_b"], p["conf_ln_g"],
        p["conf_ln_b"], p["sconv_w"])
    kcmp, vcmpt = _compress_call(kc, vc, l, *p["cmp_w"], p["kg"])
    yb = _attn_call(qt, ngt, kcmp, vcmpt, ks, vst, kw, vwt)
    return yacd, yb


def _prepare(p):
    q = dict(p)
    q["bst"] = jnp.repeat(jnp.swapaxes(p["gmlp_bs"], 1, 2), MIX_W // GMLP_GROUPS, axis=2)
    qg = jnp.tile(p["nsa_q_norm_g"], (1, MIX_W // HEAD_DIM))
    q["qg"] = jnp.broadcast_to(qg[:, :, None], qg.shape + (LANES,))
    q["kg"] = jnp.tile(p["nsa_k_norm_g"], (1, 1, N_KV_HEADS))
    q["cmp_w"] = _compress_weights(p["nsa_cmp_pe"], p["nsa_cmp_w1"], p["nsa_cmp_w2"])
    for name in ("w_branch", "w_gate", "w_out", "w_mlp1", "w_mlp2"):
        q[name] = p[name].astype(MXU_DTYPE)
    for name in ("norm1_g", "gmlp_ln_g", "gmlp_ln_b", "conf_conv_b", "conf_ln_g", "conf_ln_b",
                 "b_gate", "norm2_g"):
        q[name] = p[name][:, None, :]
    return q


def kernel(x, norm1_g, w_in, gmlp_ln_g, gmlp_ln_b, gmlp_ws, gmlp_bs, nsa_q_norm_g, nsa_k_norm_g,
           nsa_cmp_pe, nsa_cmp_w1, nsa_cmp_w2, conf_conv_w, conf_conv_b, conf_ln_g, conf_ln_b,
           sconv_w, w_branch, w_gate, b_gate, w_out, norm2_g, w_mlp1, w_mlp2):
    B, S, D = x.shape
    assert D == D_MODEL and S % min(TM_PROJ, S) == 0 and S % min(T_ATT, S) == 0 and S % CH_ATT == 0
    p = _prepare(dict(
        norm1_g=norm1_g, w_in=w_in, gmlp_ln_g=gmlp_ln_g, gmlp_ln_b=gmlp_ln_b, gmlp_ws=gmlp_ws,
        gmlp_bs=gmlp_bs, nsa_q_norm_g=nsa_q_norm_g, nsa_k_norm_g=nsa_k_norm_g,
        nsa_cmp_pe=nsa_cmp_pe, nsa_cmp_w1=nsa_cmp_w1, nsa_cmp_w2=nsa_cmp_w2,
        conf_conv_w=conf_conv_w, conf_conv_b=conf_conv_b, conf_ln_g=conf_ln_g, conf_ln_b=conf_ln_b,
        sconv_w=sconv_w, w_branch=w_branch, w_gate=w_gate, b_gate=b_gate, w_out=w_out,
        norm2_g=norm2_g, w_mlp1=w_mlp1, w_mlp2=w_mlp2))
    for l in range(norm1_g.shape[0]):
        yacd, yb = _layer_mixers(x, l, p)
        x = _merge_mlp_call(x, l, yacd, yb, p["norm1_g"], p["w_branch"], p["w_gate"], p["b_gate"],
                            p["w_out"], p["norm2_g"], p["w_mlp1"], p["w_mlp2"])
    return x
```

```python
import functools

import jax
import jax.numpy as jnp
from jax import lax
from jax.experimental import pallas as pl
from jax.experimental.pallas import tpu as pltpu

D_MODEL = 1024
MIX_W = 256
HEAD_DIM = 64
N_KV_HEADS = 2
Q_PER_KV = 2
KV_W = N_KV_HEADS * HEAD_DIM
GMLP_GROUPS = 4
GMLP_CHUNK = 128
CMP_LEN = 32
CMP_STRIDE = 16
CMP_HIDDEN = 128
SEL_BLOCK = 64
SEL_TOPK = 8
N_LOCAL_BLOCKS = 2
WINDOW = 512
CONF_KERNEL = 31
SCONV_KERNEL = 3
D_FF = 4 * D_MODEL
NSA_GATE_W = 12
NEG_INF = -1e30

LANES = 128
SUBLANES = 8
MXU_DTYPE = jnp.bfloat16
VMEM_LIMIT = 56 * 1024 * 1024

_C_GU, _C_GV, _C_Q = 0, 256, 512
_C_KC, _C_VC, _C_KS, _C_VS, _C_KW, _C_VW = 768, 896, 1024, 1152, 1280, 1408
_C_NG = 1536
_C_CA, _C_CB, _C_SB, _C_SC, _C_SH = 1664, 1920, 2176, 2432, 2688
_NG_SRC = 1536

TM_PROJ = 1024
TM_MERGE = 512
T_ATT = 256
CH_ATT = 256
LOG2E = 1.4426950408889634
V_AUG_ROWS = 16
CMP_PAD = 8
CONV_HALO = 32
SCONV_HALO = 8


def _rms_rows(x, g, eps=1e-6):
    return x * lax.rsqrt(jnp.mean(x * x, axis=-1, keepdims=True) + eps) * g


def _layernorm_rows(x, g, b, eps=1e-5):
    mu = jnp.mean(x, axis=-1, keepdims=True)
    xc = x - mu
    var = jnp.mean(xc * xc, axis=-1, keepdims=True)
    return xc * lax.rsqrt(var + eps) * g + b


def _head_rms_rows(x, g, eps=1e-6):
    n_heads = x.shape[-1] // HEAD_DIM
    head = lax.broadcasted_iota(jnp.int32, x.shape, 1) >> 6
    xx = x * x
    scale = jnp.zeros_like(x)
    for h in range(n_heads):
        ms = jnp.sum(jnp.where(head == h, xx, 0.0), axis=-1, keepdims=True) * (1.0 / HEAD_DIM)
        scale = jnp.where(head == h, lax.rsqrt(ms + eps), scale)
    return x * scale * g


def _layer_spec(a, l):
    return pl.BlockSpec((1,) + a.shape[1:], lambda *_: (l,) + (0,) * (a.ndim - 1))


def _proj_kernel(x_ref, g1_ref, wa_ref, wng_ref, wb_ref, lng_ref, lnb_ref, ws_ref, bst_ref, qg_ref,
                 kg_ref, cw_ref, cbias_ref, clg_ref, clb_ref, sw_ref,
                 yacd_ref, qt_ref, kc_ref, vc_ref, ks_ref, vst_ref, kw_ref, vwt_ref, ngt_ref,
                 zbuf, cbuf, *, tm, ch):
    @pl.when(pl.program_id(1) == 0)
    def _():
        zbuf[...] = jnp.zeros((CONV_HALO, MIX_W), jnp.float32)
        cbuf[...] = jnp.zeros((SCONV_HALO, MIX_W), jnp.float32)

    x = x_ref[0]
    xn = _rms_rows(x, g1_ref[0]).astype(MXU_DTYPE)

    def proj(lo, width):
        if lo < _C_NG:
            w = wa_ref[lo:lo + width, :]
        elif lo == _C_NG:
            w = wng_ref[...]
        else:
            w = wb_ref[lo - _C_CA:lo - _C_CA + width, :]
        return lax.dot_general(xn, w, (((1,), (1,)), ((), ())), preferred_element_type=jnp.float32)

    def causal_conv(buf, w, width, halo):
        offsets = [halo - (width - 1) + j for j in range(width)]
        out = jnp.zeros((tm, MIX_W), jnp.float32)
        for b in range(SUBLANES):
            taps = [j for j in range(width) if offsets[j] % SUBLANES == b]
            if not taps:
                continue
            rows = tm if b == 0 else tm + SUBLANES
            part = jnp.zeros((rows, MIX_W), jnp.float32)
            for j in taps:
                part = part + w[j:j + 1] * buf[offsets[j] - b:offsets[j] - b + rows]
            out = out + part[b:b + tm]
            yield out


    z = proj(_C_CA, MIX_W) * jax.nn.sigmoid(proj(_C_CB, MIX_W))
    conv = causal_conv(jnp.concatenate([zbuf[...], z], axis=0), cw_ref[0], CONF_KERNEL, CONV_HALO)
    zbuf[...] = z[tm - CONV_HALO:tm]

    def emit_q():
        qt = proj(_C_Q, MIX_W).T
        gq = jnp.concatenate([qg_ref[0]] * (tm // LANES), axis=1)
        pieces = []
        for h in range(MIX_W // HEAD_DIM):
            blk = qt[h * HEAD_DIM:(h + 1) * HEAD_DIM]
            ms = jnp.sum(blk * blk, axis=0, keepdims=True) * (1.0 / HEAD_DIM)
            pieces.append(blk * lax.rsqrt(ms + 1e-6))
        qn = (jnp.concatenate(pieces, axis=0) * gq * (HEAD_DIM ** -0.5 * LOG2E)).astype(qt_ref.dtype)
        for c in range(tm // ch):
            qt_ref[0, c] = qn[:, c * ch:(c + 1) * ch]

    def emit_raw(dst_ref, col):
        dst_ref[0] = proj(col, KV_W)

    def emit_key(dst_ref, col, branch):
        dst_ref[0] = _head_rms_rows(proj(col, KV_W), kg_ref[0, branch:branch + 1, :]).astype(dst_ref.dtype)

    def emit_value_t(dst_ref, col):
        vt = proj(col, KV_W).T
        for c in range(tm // ch):
            dst_ref[0, c] = vt[:, c * ch:(c + 1) * ch].astype(dst_ref.dtype)

    def emit_gates():
        ngt = proj(_C_NG, LANES).T[0:16, :]
        for c in range(tm // ch):
            ngt_ref[0, c] = ngt[:, c * ch:(c + 1) * ch]

    held = {}

    def hold(name, col):
        held[name] = proj(col, MIX_W)

    slices = [
        [lambda: emit_raw(kc_ref, _C_KC), lambda: emit_raw(vc_ref, _C_VC)],
        [lambda: emit_key(ks_ref, _C_KS, 1), lambda: emit_key(kw_ref, _C_KW, 2)],
        [lambda: emit_value_t(vst_ref, _C_VS), lambda: emit_value_t(vwt_ref, _C_VW)],
        [emit_q],
        [emit_gates, lambda: hold("sb", _C_SB)],
        [lambda: hold("sc", _C_SC), lambda: hold("sh", _C_SH)],
        [lambda: hold("gu", _C_GU)],
        [lambda: hold("gv", _C_GV)],
    ]
    acc = None
    for tasks in slices:
        for task in tasks:
            task()
        acc = next(conv, acc)
    for acc in conv:
        pass
    yacd_ref[0, :, MIX_W:2 * MIX_W] = jax.nn.silu(_layernorm_rows(
        acc + cbias_ref[0], clg_ref[0], clb_ref[0])).astype(yacd_ref.dtype)

    sc = held["sc"] * held["sh"]
    for acc in causal_conv(jnp.concatenate([cbuf[...], sc], axis=0), sw_ref[0], SCONV_KERNEL, SCONV_HALO):
        pass
    cbuf[...] = sc[tm - SCONV_HALO:tm]
    yacd_ref[0, :, 2 * MIX_W:3 * MIX_W] = (held["sb"] * acc).astype(yacd_ref.dtype)

    u = jax.nn.gelu(held["gu"])
    v = jax.nn.gelu(held["gv"])
    v = _layernorm_rows(v, lng_ref[0], lnb_ref[0]).astype(MXU_DTYPE)
    tri = (lax.broadcasted_iota(jnp.int32, (GMLP_CHUNK, GMLP_CHUNK), 0)
           >= lax.broadcasted_iota(jnp.int32, (GMLP_CHUNK, GMLP_CHUNK), 1))
    grp = lax.broadcasted_iota(jnp.int32, (GMLP_CHUNK, MIX_W), 1) >> 6
    wsm = [jnp.where(tri, ws_ref[0, g], 0.0).astype(MXU_DTYPE) for g in range(GMLP_GROUPS)]
    for c in range(tm // GMLP_CHUNK):
        rows = slice(c * GMLP_CHUNK, (c + 1) * GMLP_CHUNK)
        vch = v[rows]
        mixed = jnp.zeros((GMLP_CHUNK, MIX_W), jnp.float32)
        for g in range(GMLP_GROUPS):
            r = jnp.dot(wsm[g], vch, preferred_element_type=jnp.float32)
            mixed = jnp.where(grp == g, r, mixed)
        yacd_ref[0, rows, 0:MIX_W] = (u[rows] * (mixed + bst_ref[0])).astype(yacd_ref.dtype)


def _proj_call(x, l, wa, wng, wb, g1, lng, lnb, ws, bst, qg, kg, cw, cbias, clg, clb, sw):
    B, S, D = x.shape
    tm, ch = min(TM_PROJ, S), CH_ATT
    full = lambda a: pl.BlockSpec(a.shape, lambda b, i: (0,) * a.ndim)
    seq = lambda width: pl.BlockSpec((1, tm, width), lambda b, i: (b, i, 0))
    stacked = (lng, lnb, ws, bst, qg, kg, cw, cbias, clg, clb, sw)
    ins = (g1, wa, wng, wb) + stacked
    in_specs = ([seq(D), _layer_spec(g1, l), full(wa), full(wng), full(wb)]
                + [_layer_spec(a, l) for a in stacked])
    out_shape = (
        jax.ShapeDtypeStruct((B, S, 3 * MIX_W), MXU_DTYPE),
        jax.ShapeDtypeStruct((B, S // ch, MIX_W, ch), MXU_DTYPE),
        jax.ShapeDtypeStruct((B, S, KV_W), jnp.float32),
        jax.ShapeDtypeStruct((B, S, KV_W), jnp.float32),
        jax.ShapeDtypeStruct((B, S, KV_W), MXU_DTYPE),
        jax.ShapeDtypeStruct((B, S // ch, KV_W, ch), MXU_DTYPE),
        jax.ShapeDtypeStruct((B, S, KV_W), MXU_DTYPE),
        jax.ShapeDtypeStruct((B, S // ch, KV_W, ch), MXU_DTYPE),
        jax.ShapeDtypeStruct((B, S // ch, 16, ch), jnp.float32),
    )
    vt_spec = pl.BlockSpec((1, tm // ch, KV_W, ch), lambda b, i: (b, i, 0, 0))
    out_specs = (
        seq(3 * MIX_W),
        pl.BlockSpec((1, tm // ch, MIX_W, ch), lambda b, i: (b, i, 0, 0)),
        seq(KV_W), seq(KV_W), seq(KV_W), vt_spec, seq(KV_W), vt_spec,
        pl.BlockSpec((1, tm // ch, 16, ch), lambda b, i: (b, i, 0, 0)),
    )
    return pl.pallas_call(
        functools.partial(_proj_kernel, tm=tm, ch=ch),
        grid=(B, S // tm),
        in_specs=in_specs,
        out_specs=out_specs,
        out_shape=out_shape,
        scratch_shapes=[pltpu.VMEM((CONV_HALO, MIX_W), jnp.float32),
                        pltpu.VMEM((SCONV_HALO, MIX_W), jnp.float32)],
        compiler_params=pltpu.CompilerParams(
            dimension_semantics=("arbitrary", "arbitrary"), vmem_limit_bytes=VMEM_LIMIT),
        name="proj_mixers",
    )(x, *ins)


def _compress_kernel(kc_ref, vc_ref, pet_ref, peb_ref, wt_ref, wb_ref, w2_ref, kg_ref,
                     kcmp_ref, vcmpt_ref):
    nc = kc_ref.shape[1] // CMP_STRIDE

    def compress(src_ref, idx):
        chunks = jnp.concatenate(
            [src_ref[0, pl.ds(l, nc, stride=CMP_STRIDE), :] for l in range(CMP_STRIDE)], axis=1)
        a1 = jnp.dot((chunks + pet_ref[0, idx]).astype(MXU_DTYPE), wt_ref[0, idx],
                     preferred_element_type=jnp.float32)
        a2 = jnp.dot((chunks + peb_ref[0, idx]).astype(MXU_DTYPE), wb_ref[0, idx],
                     preferred_element_type=jnp.float32)
        hid = jax.nn.gelu(a1 + pltpu.roll(a2, shift=nc - 1, axis=0))
        return jnp.dot(hid.astype(MXU_DTYPE), w2_ref[0, idx], preferred_element_type=jnp.float32)

    kcmp_ref[0] = _head_rms_rows(compress(kc_ref, 0), kg_ref[0, 0:1, :]).astype(kcmp_ref.dtype)
    vcmpt_ref[0] = compress(vc_ref, 1).T.astype(vcmpt_ref.dtype)


def _compress_call(kc, vc, l, pet, peb, wt, wb, w2, kg):
    B, S, _ = kc.shape
    nc = S // CMP_STRIDE
    per_b = lambda r, c: pl.BlockSpec((1, r, c), lambda b: (b, 0, 0))
    ins = (pet, peb, wt, wb, w2, kg)
    return pl.pallas_call(
        _compress_kernel,
        grid=(B,),
        in_specs=[per_b(S, KV_W)] * 2 + [_layer_spec(a, l) for a in ins],
        out_specs=(per_b(nc, KV_W), per_b(KV_W, nc)),
        out_shape=(jax.ShapeDtypeStruct((B, nc, KV_W), MXU_DTYPE),
                   jax.ShapeDtypeStruct((B, KV_W, nc), MXU_DTYPE)),
        compiler_params=pltpu.CompilerParams(
            dimension_semantics=("arbitrary",), vmem_limit_bytes=VMEM_LIMIT),
        name="nsa_compress",
    )(kc, vc, *ins)


def _attn_kernel(*refs, t, seq_len):
    for i in range(seq_len // t):
        _attn_tile(i, *refs, t=t, seq_len=seq_len)


def _attn_tile(i, qt_ref, ngt_ref, kcmp_ref, vcmpt_ref, ks_ref, vst_ref, kw_ref, vwt_ref,
               y_ref, selb_scr, pcmp_scr, *, t, seq_len):
    nc = seq_len // CMP_STRIDE
    n_cmp = nc - 1
    n_sel = seq_len // SEL_BLOCK
    k_top = min(SEL_TOPK, n_sel)
    bpc = t // SEL_BLOCK
    cmp_per_sel = SEL_BLOCK // CMP_STRIDE
    nw = WINDOW // t
    t2 = 2 * t
    hq = t // 2
    f32 = jnp.float32
    n_cmp_live = min(nc, (i + 1) * t // CMP_STRIDE)
    n_sel_live = (i + 1) * bpc
    n_groups = -(-n_sel_live // SUBLANES)

    def lanes_of(a0, a1):
        return jnp.concatenate([a0[:, :hq], a1[:, :hq], a0[:, hq:], a1[:, hq:]], axis=1)

    def both(a):
        return lanes_of(a, a)

    def keep_bias(cond):
        return jnp.where(cond, 0.0, NEG_INF).astype(f32)

    qt = qt_ref[0, i]
    qpads = []
    for h in range(N_KV_HEADS):
        qh = lanes_of(qt[(2 * h) * HEAD_DIM:(2 * h + 1) * HEAD_DIM],
                      qt[(2 * h + 1) * HEAD_DIM:(2 * h + 2) * HEAD_DIM])
        zero = jnp.zeros_like(qh)
        qpads.append(jnp.concatenate([qh, zero] if h == 0 else [zero, qh], axis=0))

    n_i = lax.broadcasted_iota(jnp.int32, (n_cmp_live, t), 0)
    qpos_c = i * t + lax.broadcasted_iota(jnp.int32, (n_cmp_live, t), 1)
    cmp_ok = (n_i * CMP_STRIDE + (CMP_LEN - 1) <= qpos_c) & (n_i < n_cmp)
    cmp_bias = both(keep_bias(cmp_ok))
    cmp_keep = both(jnp.where(cmp_ok, 1.0, 0.0).astype(f32))

    blk = lax.broadcasted_iota(jnp.int32, (n_groups * SUBLANES, t), 0)
    cur = (i * t + lax.broadcasted_iota(jnp.int32, (n_groups * SUBLANES, t), 1)) >> 6
    dist = cur - blk
    causal = dist >= 0
    forced = (blk == 0) | (causal & (dist < N_LOCAL_BLOCKS))
    blk8 = lax.broadcasted_iota(jnp.int32, (8, t), 0)

    kcmp = kcmp_ref[0, 0:n_cmp_live, :]
    pcmp_scr[:, 0:CMP_PAD, :] = jnp.zeros((t // LANES, CMP_PAD, LANES), f32)
    o_cmp = [None] * N_KV_HEADS

    def compressed_and_select(h):
        sm = jnp.dot(kcmp, qpads[h], preferred_element_type=f32) + cmp_bias
        e = jnp.exp2(sm - jnp.max(sm, axis=0, keepdims=True)) * cmp_keep
        den = jnp.sum(e, axis=0, keepdims=True)
        p_cmp = e * (1.0 / jnp.where(den > 0.0, den, 1.0))
        p_all = p_cmp.astype(MXU_DTYPE)
        if n_cmp_live < nc:
            p_all = jnp.concatenate([p_all, jnp.zeros((nc - n_cmp_live, t2), MXU_DTYPE)], axis=0)
        o_cmp[h] = jnp.dot(vcmpt_ref[0, h * HEAD_DIM:(h + 1) * HEAD_DIM, :], p_all,
                           preferred_element_type=f32)
        yield

        p_pair = jnp.concatenate([p_cmp[:, 0:hq] + p_cmp[:, hq:t],
                                  p_cmp[:, t:t + hq] + p_cmp[:, t + hq:t2]], axis=1)
        slabs = []
        for lb in range(t // LANES):
            pcmp_scr[lb, CMP_PAD:CMP_PAD + n_cmp_live, :] = p_pair[:, lb * LANES:(lb + 1) * LANES]
            slabs.append(functools.reduce(lambda a, b: a + b, [
                pcmp_scr[lb, pl.ds(CMP_PAD + off, n_sel_live, stride=cmp_per_sel), :]
                for off in range(1 - CMP_LEN // CMP_STRIDE, cmp_per_sel)]))
        p_slc = jnp.concatenate(slabs, axis=1)
        if n_sel_live < n_groups * SUBLANES:
            p_slc = jnp.concatenate(
                [p_slc, jnp.zeros((n_groups * SUBLANES - n_sel_live, t), f32)], axis=0)
        score = jnp.where(forced, jnp.inf, jnp.where(causal, p_slc, -jnp.inf))
        groups = [score[8 * g:8 * g + 8] for g in range(n_groups)]
        rank = [jnp.zeros((8, t), f32) for _ in groups]
        for r in range(n_sel_live):
            row = score[r:r + 1, :]
            for g, sg in enumerate(groups):
                if 8 * g > r:
                    beats = row >= sg
                elif 8 * g + 7 <= r:
                    beats = row > sg
                else:
                    beats = (row > sg) | ((row == sg) & (blk8 > r - 8 * g))
                rank[g] = rank[g] + jnp.where(beats, 1.0, 0.0)
            if r % 4 == 3:
                yield
        selected = (jnp.concatenate(rank, axis=0) < k_top) & causal
        selb_scr[h, 0:n_groups * SUBLANES, :] = both(keep_bias(selected))

    r_i = lax.broadcasted_iota(jnp.int32, (t, t), 0)
    q_i = lax.broadcasted_iota(jnp.int32, (t, t), 1)
    diag_bias = both(keep_bias(r_i <= q_i))
    ones_rows = jnp.where(lax.broadcasted_iota(jnp.int32, (V_AUG_ROWS, t), 0) == 0,
                          1.0, 0.0).astype(MXU_DTYPE)

    def k_chunk(k_ref, c):
        start = c * t if isinstance(c, int) else pl.multiple_of(c * t, t)
        return k_ref[0, pl.ds(start, t), :]

    def v_aug(vt_ref, c, h):
        return jnp.concatenate([vt_ref[0, c, h * HEAD_DIM:(h + 1) * HEAD_DIM, :], ones_rows], axis=0)

    def colmax8(s):
        return jnp.max(s.reshape(s.shape[0] // 8, 8, s.shape[1]), axis=0)

    def weighted_values(s, m, vaug):
        p = jnp.exp2(s - m).astype(MXU_DTYPE)
        return jnp.dot(vaug, p, preferred_element_type=f32)

    def normalise(acc):
        return acc[0:HEAD_DIM] * (1.0 / acc[HEAD_DIM:HEAD_DIM + 1])

    def on_lanes(full, part, lanes, op):
        pieces = [full[:, :lanes.start]] if lanes.start else []
        pieces.append(op(full[:, lanes], part))
        if lanes.stop < full.shape[1]:
            pieces.append(full[:, lanes.stop:])
        return pieces[0] if len(pieces) == 1 else jnp.concatenate(pieces, axis=1)

    top, bot = slice(0, hq), slice(hq, t)
    lo_q, hi_q, all_q = slice(0, t), slice(t, t2), slice(0, t2)
    full_pieces = ((slice(0, t), all_q),)
    diag_pieces = ((top, all_q), (bot, hi_q))
    first_pieces = ((top, lo_q), (bot, all_q))

    heads = range(N_KV_HEADS)
    neg8 = jnp.full((8, t2), NEG_INF, f32)

    def sel_scores(c, k, h, rows, lanes, extra):
        s = jnp.dot(k[rows], qpads[h][:, lanes], preferred_element_type=f32)
        parts = []
        for r in range((rows.stop - rows.start) // SEL_BLOCK):
            blk_rows = slice(r * SEL_BLOCK, (r + 1) * SEL_BLOCK)
            bias = selb_scr[h, pl.ds(c * bpc + rows.start // SEL_BLOCK + r, 1), lanes]
            if extra is not None:
                bias = bias + (extra[blk_rows] if extra.shape[0] > 1 else extra)
            parts.append(s[blk_rows] + bias)
        return jnp.concatenate(parts, axis=0)

    def win_scores(c, k, h, rows, lanes, extra):
        s = jnp.dot(k[rows], qpads[h][:, lanes], preferred_element_type=f32)
        return s if extra is None else s + extra

    def two_pass_branch(work, k_ref, vt_ref, score_fn):
        keys = {c: k_chunk(k_ref, c) for c in sorted({w[0] for w in work})}
        ss = [[] for _ in heads]
        pm = [neg8 for _ in heads]
        m = [None for _ in heads]
        accs = [jnp.zeros((HEAD_DIM + V_AUG_ROWS, t2), f32) for _ in heads]

        def pass1(item):
            c, rows, lanes, extra = item
            for h in heads:
                s = score_fn(c, keys[c], h, rows, lanes, extra)
                ss[h].append(s)
                pm[h] = on_lanes(pm[h], colmax8(s), lanes, jnp.maximum)

        def fix_max():
            for h in heads:
                m[h] = jnp.max(pm[h], axis=0, keepdims=True)

        def pass2(n):
            c, rows, lanes, _ = work[n]
            for h in heads:
                part = weighted_values(ss[h][n], m[h][:, lanes], v_aug(vt_ref, c, h)[:, rows])
                accs[h] = on_lanes(accs[h], part, lanes, lambda a, b: a + b)

        steps1 = [functools.partial(pass1, item) for item in work]
        steps2 = [functools.partial(pass2, n) for n in range(len(work))]
        return steps1, fix_max, steps2, lambda: [normalise(accs[h]) for h in heads]

    sel_work = [(c, rows, lanes, None) for c in range(i) for rows, lanes in full_pieces]
    sel_work += [(i, rows, lanes, diag_bias[rows, lanes]) for rows, lanes in diag_pieces]

    first_bias = both(keep_bias(r_i > q_i))
    win_work = []
    for cc in range(nw + 1):
        ci = i - nw + cc
        if ci < 0:
            continue
        if cc == nw:
            win_work += [(ci, rows, lanes, diag_bias[rows, lanes]) for rows, lanes in diag_pieces]
        elif cc == 0:
            win_work += [(ci, rows, lanes, first_bias[rows, lanes]) for rows, lanes in first_pieces]
        else:
            win_work += [(ci, rows, lanes, None) for rows, lanes in full_pieces]

    sel1, sel_fix, sel2, sel_out = two_pass_branch(sel_work, ks_ref, vst_ref, sel_scores)
    win1, win_fix, win2, win_out = two_pass_branch(win_work, kw_ref, vwt_ref, win_scores)
    for h in heads:
        for _ in compressed_and_select(h):
            pass
    for step in win1:
        step()
    win_fix()
    for n in range(max(len(sel1), len(win2))):
        for steps in (sel1, win2):
            if n < len(steps):
                steps[n]()
    sel_fix()
    for step in sel2:
        step()
    o_sel, o_win = sel_out(), win_out()

    sig = jax.nn.sigmoid(ngt_ref[0, i])
    out_rows = []
    for h in range(N_KV_HEADS):
        def gate(r):
            a = (2 * h) * 3 + r
            b = (2 * h + 1) * 3 + r
            return lanes_of(sig[a:a + 1, :], sig[b:b + 1, :])

        o = gate(0) * o_cmp[h] + gate(1) * o_sel[h] + gate(2) * o_win[h]
        out_rows += [jnp.concatenate([o[:, 0:hq], o[:, t:t + hq]], axis=1),
                     jnp.concatenate([o[:, hq:t], o[:, t + hq:t2]], axis=1)]
    y_ref[0, i * t:(i + 1) * t, :] = jnp.concatenate(out_rows, axis=0).T.astype(y_ref.dtype)


def _attn_call(qt, ngt, kcmp, vcmpt, ks, vst, kw, vwt):
    B, n_tiles, _, t = qt.shape
    S = n_tiles * t
    assert t == T_ATT == CH_ATT and WINDOW % t == 0 and (S // SEL_BLOCK) % 8 == 0
    assert SEL_BLOCK % CMP_STRIDE == 0 and CMP_LEN % CMP_STRIDE == 0 and CMP_LEN // CMP_STRIDE <= CMP_PAD
    nc = S // CMP_STRIDE
    n_sel = S // SEL_BLOCK
    per_b = lambda a: pl.BlockSpec((1,) + a.shape[1:], lambda b: (b,) + (0,) * (a.ndim - 1))
    ins = (qt, ngt, kcmp, vcmpt, ks, vst, kw, vwt)
    return pl.pallas_call(
        functools.partial(_attn_kernel, t=t, seq_len=S),
        grid=(B,),
        in_specs=[per_b(a) for a in ins],
        out_specs=pl.BlockSpec((1, S, MIX_W), lambda b: (b, 0, 0)),
        out_shape=jax.ShapeDtypeStruct((B, S, MIX_W), MXU_DTYPE),
        scratch_shapes=[
            pltpu.VMEM((N_KV_HEADS, n_sel, 2 * t), jnp.float32),
            pltpu.VMEM((t // LANES, CMP_PAD + nc, LANES), jnp.float32),
        ],
        compiler_params=pltpu.CompilerParams(
            dimension_semantics=("arbitrary",), vmem_limit_bytes=VMEM_LIMIT),
        name="nsa_attention",
    )(*ins)


def _merge_mlp_kernel(x_ref, yacd_ref, yb_ref, g1_ref, wbr_ref, wg_ref, bg_ref, wo_ref,
                      g2_ref, w1_ref, w2_ref, o_ref):
    x = x_ref[0]
    xn = _rms_rows(x, g1_ref[0]).astype(MXU_DTYPE)
    ys = (yacd_ref[0, :, 0:MIX_W], yb_ref[0], yacd_ref[0, :, MIX_W:2 * MIX_W],
          yacd_ref[0, :, 2 * MIX_W:3 * MIX_W])
    mixed = jnp.zeros(x.shape, jnp.float32)
    for n, y in enumerate(ys):
        cols = slice(n * D_MODEL, (n + 1) * D_MODEL)
        proj = jnp.dot(y, wbr_ref[0, n], preferred_element_type=jnp.float32)
        logits = jnp.dot(xn, wg_ref[0, :, cols], preferred_element_type=jnp.float32) + bg_ref[0, :, cols]
        mixed = mixed + jax.nn.sigmoid(logits) * proj
    h = x + jnp.dot(mixed.astype(MXU_DTYPE), wo_ref[0], preferred_element_type=jnp.float32)

    hn = _rms_rows(h, g2_ref[0]).astype(MXU_DTYPE)
    acc = h
    for c in range(D_FF // D_MODEL):
        cols = slice(c * D_MODEL, (c + 1) * D_MODEL)
        hid = jnp.dot(hn, w1_ref[0, :, cols], preferred_element_type=jnp.float32)
        hid = jnp.square(jnp.maximum(hid, 0.0)).astype(MXU_DTYPE)
        acc = acc + jnp.dot(hid, w2_ref[0, cols, :], preferred_element_type=jnp.float32)
    o_ref[0] = acc


def _merge_mlp_call(x, l, yacd, yb, g1, wbr, wg, bg, wo, g2, w1, w2):
    B, S, d = x.shape
    tm = min(TM_MERGE, S)
    rows = lambda w: pl.BlockSpec((1, tm, w), lambda b, i: (b, i, 0))
    stacked = (g1, wbr, wg, bg, wo, g2, w1, w2)
    return pl.pallas_call(
        _merge_mlp_kernel,
        grid=(B, S // tm),
        in_specs=[rows(d), rows(3 * MIX_W), rows(MIX_W)] + [_layer_spec(a, l) for a in stacked],
        out_specs=rows(d),
        out_shape=jax.ShapeDtypeStruct((B, S, d), jnp.float32),
        compiler_params=pltpu.CompilerParams(
            dimension_semantics=("arbitrary", "arbitrary"), vmem_limit_bytes=VMEM_LIMIT),
        name="merge_mlp",
    )(x, yacd, yb, *stacked)


def _split_w_in_t(w_in, l):
    wt = jnp.swapaxes(w_in, 1, 2)[l]
    wng = jnp.pad(wt[_NG_SRC:_NG_SRC + NSA_GATE_W], ((0, LANES - NSA_GATE_W), (0, 0)))
    return (wt[:_NG_SRC].astype(MXU_DTYPE), wng.astype(MXU_DTYPE),
            wt[_NG_SRC + NSA_GATE_W:].astype(MXU_DTYPE))


def _compress_weights(pe, w1, w2):
    assert N_KV_HEADS == 2
    L = pe.shape[0]
    w1r = w1.reshape(L, 2, CMP_LEN, HEAD_DIM, CMP_HIDDEN)

    def block_diag(w):
        z = jnp.zeros_like(w)
        return jnp.concatenate([jnp.concatenate([w, z], axis=-1),
                                jnp.concatenate([z, w], axis=-1)], axis=-2)

    def expand(w):
        return block_diag(w).reshape(
            L, 2, CMP_STRIDE * KV_W, N_KV_HEADS * CMP_HIDDEN).astype(MXU_DTYPE)

    def expand_pe(p):
        return jnp.concatenate([p, p], axis=-1).reshape(L, 2, 1, CMP_STRIDE * KV_W)

    w2b = block_diag(w2).astype(MXU_DTYPE)
    return (expand_pe(pe[:, :, :CMP_STRIDE]), expand_pe(pe[:, :, CMP_STRIDE:]),
            expand(w1r[:, :, :CMP_STRIDE]), expand(w1r[:, :, CMP_STRIDE:]), w2b)


def _layer_mixers(x, l, p):
    (yacd, qt, kc, vc, ks, vst, kw, vwt, ngt) = _proj_call(
        x, l, *_split_w_in_t(p["w_in"], l), p["norm1_g"], p["gmlp_ln_g"], p["gmlp_ln_b"], p["gmlp_ws"],
        p["bst"], p["qg"], p["kg"], p["conf_conv_w"], p["conf_conv_b"], p["conf_ln_g"],
        p["conf_ln_b"], p["sconv_w"])
    kcmp, vcmpt = _compress_call(kc, vc, l, *p["cmp_w"], p["kg"])
    yb = _attn_call(qt, ngt, kcmp, vcmpt, ks, vst, kw, vwt)
    return yacd, yb


def _prepare(p):
    q = dict(p)
    q["bst"] = jnp.repeat(jnp.swapaxes(p["gmlp_bs"], 1, 2), MIX_W // GMLP_GROUPS, axis=2)
    qg = jnp.tile(p["nsa_q_norm_g"], (1, MIX_W // HEAD_DIM))
    q["qg"] = jnp.broadcast_to(qg[:, :, None], qg.shape + (LANES,))
    q["kg"] = jnp.tile(p["nsa_k_norm_g"], (1, 1, N_KV_HEADS))
    q["cmp_w"] = _compress_weights(p["nsa_cmp_pe"], p["nsa_cmp_w1"], p["nsa_cmp_w2"])
    for name in ("w_branch", "w_gate", "w_out", "w_mlp1", "w_mlp2"):
        q[name] = p[name].astype(MXU_DTYPE)
    for name in ("norm1_g", "gmlp_ln_g", "gmlp_ln_b", "conf_conv_b", "conf_ln_g", "conf_ln_b",
                 "b_gate", "norm2_g"):
        q[name] = p[name][:, None, :]
    return q


def kernel(x, norm1_g, w_in, gmlp_ln_g, gmlp_ln_b, gmlp_ws, gmlp_bs, nsa_q_norm_g, nsa_k_norm_g,
           nsa_cmp_pe, nsa_cmp_w1, nsa_cmp_w2, conf_conv_w, conf_conv_b, conf_ln_g, conf_ln_b,
           sconv_w, w_branch, w_gate, b_gate, w_out, norm2_g, w_mlp1, w_mlp2):
    B, S, D = x.shape
    assert D == D_MODEL and S % min(TM_PROJ, S) == 0 and S % min(T_ATT, S) == 0 and S % CH_ATT == 0
    p = _prepare(dict(
        norm1_g=norm1_g, w_in=w_in, gmlp_ln_g=gmlp_ln_g, gmlp_ln_b=gmlp_ln_b, gmlp_ws=gmlp_ws,
        gmlp_bs=gmlp_bs, nsa_q_norm_g=nsa_q_norm_g, nsa_k_norm_g=nsa_k_norm_g,
        nsa_cmp_pe=nsa_cmp_pe, nsa_cmp_w1=nsa_cmp_w1, nsa_cmp_w2=nsa_cmp_w2,
        conf_conv_w=conf_conv_w, conf_conv_b=conf_conv_b, conf_ln_g=conf_ln_g, conf_ln_b=conf_ln_b,
        sconv_w=sconv_w, w_branch=w_branch, w_gate=w_gate, b_gate=b_gate, w_out=w_out,
        norm2_g=norm2_g, w_mlp1=w_mlp1, w_mlp2=w_mlp2))
    for l in range(norm1_g.shape[0]):
        yacd, yb = _layer_mixers(x, l, p)
        x = _merge_mlp_call(x, l, yacd, yb, p["norm1_g"], p["w_branch"], p["w_gate"], p["b_gate"],
                            p["w_out"], p["norm2_g"], p["w_mlp1"], p["w_mlp2"])
    return x
```

```python
import functools

import jax
import jax.numpy as jnp
from jax import lax
from jax.experimental import pallas as pl
from jax.experimental.pallas import tpu as pltpu

D_MODEL = 1024
MIX_W = 256
HEAD_DIM = 64
N_KV_HEADS = 2
Q_PER_KV = 2
KV_W = N_KV_HEADS * HEAD_DIM
GMLP_GROUPS = 4
GMLP_CHUNK = 128
CMP_LEN = 32
CMP_STRIDE = 16
CMP_HIDDEN = 128
SEL_BLOCK = 64
SEL_TOPK = 8
N_LOCAL_BLOCKS = 2
WINDOW = 512
CONF_KERNEL = 31
SCONV_KERNEL = 3
D_FF = 4 * D_MODEL
NSA_GATE_W = 12
NEG_INF = -1e30

LANES = 128
SUBLANES = 8
MXU_DTYPE = jnp.bfloat16
VMEM_LIMIT = 56 * 1024 * 1024

_C_GU, _C_GV, _C_Q = 0, 256, 512
_C_KC, _C_VC, _C_KS, _C_VS, _C_KW, _C_VW = 768, 896, 1024, 1152, 1280, 1408
_C_NG = 1536
_C_CA, _C_CB, _C_SB, _C_SC, _C_SH = 1664, 1920, 2176, 2432, 2688
_NG_SRC = 1536

TM_PROJ = 1024
TM_MERGE = 512
T_ATT = 256
CH_ATT = 256
LOG2E = 1.4426950408889634
V_AUG_ROWS = 16
CMP_PAD = 8
SEL_SEGMENT = 1
CONV_HALO = 32
SCONV_HALO = 8


def _rms_rows(x, g, eps=1e-6):
    return x * lax.rsqrt(jnp.mean(x * x, axis=-1, keepdims=True) + eps) * g


def _layernorm_rows(x, g, b, eps=1e-5):
    mu = jnp.mean(x, axis=-1, keepdims=True)
    xc = x - mu
    var = jnp.mean(xc * xc, axis=-1, keepdims=True)
    return xc * lax.rsqrt(var + eps) * g + b


def _head_rms_rows(x, g, eps=1e-6):
    n_heads = x.shape[-1] // HEAD_DIM
    head = lax.broadcasted_iota(jnp.int32, x.shape, 1) >> 6
    xx = x * x
    scale = jnp.zeros_like(x)
    for h in range(n_heads):
        ms = jnp.sum(jnp.where(head == h, xx, 0.0), axis=-1, keepdims=True) * (1.0 / HEAD_DIM)
        scale = jnp.where(head == h, lax.rsqrt(ms + eps), scale)
    return x * scale * g


def _layer_spec(a, l):
    return pl.BlockSpec((1,) + a.shape[1:], lambda *_: (l,) + (0,) * (a.ndim - 1))


def _proj_kernel(x_ref, g1_ref, wa_ref, wng_ref, wb_ref, lng_ref, lnb_ref, ws_ref, bst_ref, qg_ref,
                 kg_ref, cw_ref, cbias_ref, clg_ref, clb_ref, sw_ref,
                 yacd_ref, qt_ref, kc_ref, vc_ref, ks_ref, vst_ref, kw_ref, vwt_ref, ngt_ref,
                 zbuf, cbuf, *, tm, ch):
    @pl.when(pl.program_id(1) == 0)
    def _():
        zbuf[...] = jnp.zeros((CONV_HALO, MIX_W), jnp.float32)
        cbuf[...] = jnp.zeros((SCONV_HALO, MIX_W), jnp.float32)

    x = x_ref[0]
    xn = _rms_rows(x, g1_ref[0]).astype(MXU_DTYPE)

    def proj(lo, width):
        if lo < _C_NG:
            w = wa_ref[lo:lo + width, :]
        elif lo == _C_NG:
            w = wng_ref[...]
        else:
            w = wb_ref[lo - _C_CA:lo - _C_CA + width, :]
        return lax.dot_general(xn, w, (((1,), (1,)), ((), ())), preferred_element_type=jnp.float32)

    def causal_conv(buf, w, width, halo):
        offsets = [halo - (width - 1) + j for j in range(width)]
        out = jnp.zeros((tm, MIX_W), jnp.float32)
        for b in range(SUBLANES):
            taps = [j for j in range(width) if offsets[j] % SUBLANES == b]
            if not taps:
                continue
            rows = tm if b == 0 else tm + SUBLANES
            part = jnp.zeros((rows, MIX_W), jnp.float32)
            for j in taps:
                part = part + w[j:j + 1] * buf[offsets[j] - b:offsets[j] - b + rows]
            out = out + part[b:b + tm]
            yield out


    z = proj(_C_CA, MIX_W) * jax.nn.sigmoid(proj(_C_CB, MIX_W))
    conv = causal_conv(jnp.concatenate([zbuf[...], z], axis=0), cw_ref[0], CONF_KERNEL, CONV_HALO)
    zbuf[...] = z[tm - CONV_HALO:tm]

    def emit_q():
        qt = proj(_C_Q, MIX_W).T
        gq = jnp.concatenate([qg_ref[0]] * (tm // LANES), axis=1)
        pieces = []
        for h in range(MIX_W // HEAD_DIM):
            blk = qt[h * HEAD_DIM:(h + 1) * HEAD_DIM]
            ms = jnp.sum(blk * blk, axis=0, keepdims=True) * (1.0 / HEAD_DIM)
            pieces.append(blk * lax.rsqrt(ms + 1e-6))
        qn = (jnp.concatenate(pieces, axis=0) * gq * (HEAD_DIM ** -0.5 * LOG2E)).astype(qt_ref.dtype)
        for c in range(tm // ch):
            qt_ref[0, c] = qn[:, c * ch:(c + 1) * ch]

    def emit_raw(dst_ref, col):
        dst_ref[0] = proj(col, KV_W)

    def emit_key(dst_ref, col, branch):
        dst_ref[0] = _head_rms_rows(proj(col, KV_W), kg_ref[0, branch:branch + 1, :]).astype(dst_ref.dtype)

    def emit_value_t(dst_ref, col):
        vt = proj(col, KV_W).T
        for c in range(tm // ch):
            dst_ref[0, c] = vt[:, c * ch:(c + 1) * ch].astype(dst_ref.dtype)

    def emit_gates():
        ngt = proj(_C_NG, LANES).T[0:16, :]
        for c in range(tm // ch):
            ngt_ref[0, c] = ngt[:, c * ch:(c + 1) * ch]

    held = {}

    def hold(name, col):
        held[name] = proj(col, MIX_W)

    slices = [
        [lambda: emit_raw(kc_ref, _C_KC), lambda: emit_raw(vc_ref, _C_VC)],
        [lambda: emit_key(ks_ref, _C_KS, 1), lambda: emit_key(kw_ref, _C_KW, 2)],
        [lambda: emit_value_t(vst_ref, _C_VS), lambda: emit_value_t(vwt_ref, _C_VW)],
        [emit_q],
        [emit_gates, lambda: hold("sb", _C_SB)],
        [lambda: hold("sc", _C_SC), lambda: hold("sh", _C_SH)],
        [lambda: hold("gu", _C_GU)],
        [lambda: hold("gv", _C_GV)],
    ]
    acc = None
    for tasks in slices:
        for task in tasks:
            task()
        acc = next(conv, acc)
    for acc in conv:
        pass
    yacd_ref[0, :, MIX_W:2 * MIX_W] = jax.nn.silu(_layernorm_rows(
        acc + cbias_ref[0], clg_ref[0], clb_ref[0])).astype(yacd_ref.dtype)

    sc = held["sc"] * held["sh"]
    for acc in causal_conv(jnp.concatenate([cbuf[...], sc], axis=0), sw_ref[0], SCONV_KERNEL, SCONV_HALO):
        pass
    cbuf[...] = sc[tm - SCONV_HALO:tm]
    yacd_ref[0, :, 2 * MIX_W:3 * MIX_W] = (held["sb"] * acc).astype(yacd_ref.dtype)

    u = jax.nn.gelu(held["gu"])
    v = jax.nn.gelu(held["gv"])
    v = _layernorm_rows(v, lng_ref[0], lnb_ref[0]).astype(MXU_DTYPE)
    tri = (lax.broadcasted_iota(jnp.int32, (GMLP_CHUNK, GMLP_CHUNK), 0)
           >= lax.broadcasted_iota(jnp.int32, (GMLP_CHUNK, GMLP_CHUNK), 1))
    grp = lax.broadcasted_iota(jnp.int32, (GMLP_CHUNK, MIX_W), 1) >> 6
    wsm = [jnp.where(tri, ws_ref[0, g], 0.0).astype(MXU_DTYPE) for g in range(GMLP_GROUPS)]
    for c in range(tm // GMLP_CHUNK):
        rows = slice(c * GMLP_CHUNK, (c + 1) * GMLP_CHUNK)
        vch = v[rows]
        mixed = jnp.zeros((GMLP_CHUNK, MIX_W), jnp.float32)
        for g in range(GMLP_GROUPS):
            r = jnp.dot(wsm[g], vch, preferred_element_type=jnp.float32)
            mixed = jnp.where(grp == g, r, mixed)
        yacd_ref[0, rows, 0:MIX_W] = (u[rows] * (mixed + bst_ref[0])).astype(yacd_ref.dtype)


def _proj_call(x, l, wa, wng, wb, g1, lng, lnb, ws, bst, qg, kg, cw, cbias, clg, clb, sw):
    B, S, D = x.shape
    tm, ch = min(TM_PROJ, S), CH_ATT
    full = lambda a: pl.BlockSpec(a.shape, lambda b, i: (0,) * a.ndim)
    seq = lambda width: pl.BlockSpec((1, tm, width), lambda b, i: (b, i, 0))
    stacked = (lng, lnb, ws, bst, qg, kg, cw, cbias, clg, clb, sw)
    ins = (g1, wa, wng, wb) + stacked
    in_specs = ([seq(D), _layer_spec(g1, l), full(wa), full(wng), full(wb)]
                + [_layer_spec(a, l) for a in stacked])
    out_shape = (
        jax.ShapeDtypeStruct((B, S, 3 * MIX_W), MXU_DTYPE),
        jax.ShapeDtypeStruct((B, S // ch, MIX_W, ch), MXU_DTYPE),
        jax.ShapeDtypeStruct((B, S, KV_W), jnp.float32),
        jax.ShapeDtypeStruct((B, S, KV_W), jnp.float32),
        jax.ShapeDtypeStruct((B, S, KV_W), MXU_DTYPE),
        jax.ShapeDtypeStruct((B, S // ch, KV_W, ch), MXU_DTYPE),
        jax.ShapeDtypeStruct((B, S, KV_W), MXU_DTYPE),
        jax.ShapeDtypeStruct((B, S // ch, KV_W, ch), MXU_DTYPE),
        jax.ShapeDtypeStruct((B, S // ch, 16, ch), jnp.float32),
    )
    vt_spec = pl.BlockSpec((1, tm // ch, KV_W, ch), lambda b, i: (b, i, 0, 0))
    out_specs = (
        seq(3 * MIX_W),
        pl.BlockSpec((1, tm // ch, MIX_W, ch), lambda b, i: (b, i, 0, 0)),
        seq(KV_W), seq(KV_W), seq(KV_W), vt_spec, seq(KV_W), vt_spec,
        pl.BlockSpec((1, tm // ch, 16, ch), lambda b, i: (b, i, 0, 0)),
    )
    return pl.pallas_call(
        functools.partial(_proj_kernel, tm=tm, ch=ch),
        grid=(B, S // tm),
        in_specs=in_specs,
        out_specs=out_specs,
        out_shape=out_shape,
        scratch_shapes=[pltpu.VMEM((CONV_HALO, MIX_W), jnp.float32),
                        pltpu.VMEM((SCONV_HALO, MIX_W), jnp.float32)],
        compiler_params=pltpu.CompilerParams(
            dimension_semantics=("arbitrary", "arbitrary"), vmem_limit_bytes=VMEM_LIMIT),
        name="proj_mixers",
    )(x, *ins)


def _compress_kernel(kc_ref, vc_ref, pet_ref, peb_ref, wt_ref, wb_ref, w2_ref, kg_ref,
                     kcmp_ref, vcmpt_ref):
    nc = kc_ref.shape[1] // CMP_STRIDE

    def compress(src_ref, idx):
        chunks = jnp.concatenate(
            [src_ref[0, pl.ds(l, nc, stride=CMP_STRIDE), :] for l in range(CMP_STRIDE)], axis=1)
        a1 = jnp.dot((chunks + pet_ref[0, idx]).astype(MXU_DTYPE), wt_ref[0, idx],
                     preferred_element_type=jnp.float32)
        a2 = jnp.dot((chunks + peb_ref[0, idx]).astype(MXU_DTYPE), wb_ref[0, idx],
                     preferred_element_type=jnp.float32)
        hid = jax.nn.gelu(a1 + pltpu.roll(a2, shift=nc - 1, axis=0))
        return jnp.dot(hid.astype(MXU_DTYPE), w2_ref[0, idx], preferred_element_type=jnp.float32)

    kcmp_ref[0] = _head_rms_rows(compress(kc_ref, 0), kg_ref[0, 0:1, :]).astype(kcmp_ref.dtype)
    vcmpt_ref[0] = compress(vc_ref, 1).T.astype(vcmpt_ref.dtype)


def _compress_call(kc, vc, l, pet, peb, wt, wb, w2, kg):
    B, S, _ = kc.shape
    nc = S // CMP_STRIDE
    per_b = lambda r, c: pl.BlockSpec((1, r, c), lambda b: (b, 0, 0))
    ins = (pet, peb, wt, wb, w2, kg)
    return pl.pallas_call(
        _compress_kernel,
        grid=(B,),
        in_specs=[per_b(S, KV_W)] * 2 + [_layer_spec(a, l) for a in ins],
        out_specs=(per_b(nc, KV_W), per_b(KV_W, nc)),
        out_shape=(jax.ShapeDtypeStruct((B, nc, KV_W), MXU_DTYPE),
                   jax.ShapeDtypeStruct((B, KV_W, nc), MXU_DTYPE)),
        compiler_params=pltpu.CompilerParams(
            dimension_semantics=("arbitrary",), vmem_limit_bytes=VMEM_LIMIT),
        name="nsa_compress",
    )(kc, vc, *ins)


def _attn_kernel(*refs, t, seq_len):
    for i in range(seq_len // t):
        _attn_tile(i, *refs, t=t, seq_len=seq_len)


def _attn_tile(i, qt_ref, ngt_ref, kcmp_ref, vcmpt_ref, ks_ref, vst_ref, kw_ref, vwt_ref,
               y_ref, selb_scr, pcmp_scr, *, t, seq_len):
    nc = seq_len // CMP_STRIDE
    n_cmp = nc - 1
    n_sel = seq_len // SEL_BLOCK
    k_top = min(SEL_TOPK, n_sel)
    bpc = t // SEL_BLOCK
    cmp_per_sel = SEL_BLOCK // CMP_STRIDE
    nw = WINDOW // t
    t2 = 2 * t
    hq = t // 2
    f32 = jnp.float32
    n_cmp_live = min(nc, (i + 1) * t // CMP_STRIDE)
    n_sel_live = (i + 1) * bpc
    n_groups = -(-n_sel_live // SUBLANES)

    def lanes_of(a0, a1):
        return jnp.concatenate([a0[:, :hq], a1[:, :hq], a0[:, hq:], a1[:, hq:]], axis=1)

    def both(a):
        return lanes_of(a, a)

    def keep_bias(cond):
        return jnp.where(cond, 0.0, NEG_INF).astype(f32)

    qt = qt_ref[0, i]
    qpads = []
    for h in range(N_KV_HEADS):
        qh = lanes_of(qt[(2 * h) * HEAD_DIM:(2 * h + 1) * HEAD_DIM],
                      qt[(2 * h + 1) * HEAD_DIM:(2 * h + 2) * HEAD_DIM])
        zero = jnp.zeros_like(qh)
        qpads.append(jnp.concatenate([qh, zero] if h == 0 else [zero, qh], axis=0))

    n_i = lax.broadcasted_iota(jnp.int32, (n_cmp_live, t), 0)
    qpos_c = i * t + lax.broadcasted_iota(jnp.int32, (n_cmp_live, t), 1)
    cmp_ok = (n_i * CMP_STRIDE + (CMP_LEN - 1) <= qpos_c) & (n_i < n_cmp)
    cmp_bias = both(keep_bias(cmp_ok))
    cmp_keep = both(jnp.where(cmp_ok, 1.0, 0.0).astype(f32))

    blk = lax.broadcasted_iota(jnp.int32, (n_groups * SUBLANES, t), 0)
    cur = (i * t + lax.broadcasted_iota(jnp.int32, (n_groups * SUBLANES, t), 1)) >> 6
    dist = cur - blk
    causal = dist >= 0
    forced = (blk == 0) | (causal & (dist < N_LOCAL_BLOCKS))
    blk8 = lax.broadcasted_iota(jnp.int32, (8, t), 0)

    kcmp = kcmp_ref[0, 0:n_cmp_live, :]
    pcmp_scr[:, 0:CMP_PAD, :] = jnp.zeros((t // LANES, CMP_PAD, LANES), f32)
    o_cmp = [None] * N_KV_HEADS

    def compressed_and_select(h):
        sm = jnp.dot(kcmp, qpads[h], preferred_element_type=f32) + cmp_bias
        e = jnp.exp2(sm - jnp.max(sm, axis=0, keepdims=True)) * cmp_keep
        den = jnp.sum(e, axis=0, keepdims=True)
        p_cmp = e * (1.0 / jnp.where(den > 0.0, den, 1.0))
        p_all = p_cmp.astype(MXU_DTYPE)
        if n_cmp_live < nc:
            p_all = jnp.concatenate([p_all, jnp.zeros((nc - n_cmp_live, t2), MXU_DTYPE)], axis=0)
        o_cmp[h] = jnp.dot(vcmpt_ref[0, h * HEAD_DIM:(h + 1) * HEAD_DIM, :], p_all,
                           preferred_element_type=f32)
        yield

        p_pair = jnp.concatenate([p_cmp[:, 0:hq] + p_cmp[:, hq:t],
                                  p_cmp[:, t:t + hq] + p_cmp[:, t + hq:t2]], axis=1)
        slabs = []
        for lb in range(t // LANES):
            pcmp_scr[lb, CMP_PAD:CMP_PAD + n_cmp_live, :] = p_pair[:, lb * LANES:(lb + 1) * LANES]
            slabs.append(functools.reduce(lambda a, b: a + b, [
                pcmp_scr[lb, pl.ds(CMP_PAD + off, n_sel_live, stride=cmp_per_sel), :]
                for off in range(1 - CMP_LEN // CMP_STRIDE, cmp_per_sel)]))
        p_slc = jnp.concatenate(slabs, axis=1)
        if n_sel_live < n_groups * SUBLANES:
            p_slc = jnp.concatenate(
                [p_slc, jnp.zeros((n_groups * SUBLANES - n_sel_live, t), f32)], axis=0)
        score = jnp.where(forced, jnp.inf, jnp.where(causal, p_slc, -jnp.inf))
        groups = [score[8 * g:8 * g + 8] for g in range(n_groups)]
        rank = [jnp.zeros((8, t), f32) for _ in groups]
        for r in range(n_sel_live):
            row = score[r:r + 1, :]
            for g, sg in enumerate(groups):
                if 8 * g > r:
                    beats = row >= sg
                elif 8 * g + 7 <= r:
                    beats = row > sg
                else:
                    beats = (row > sg) | ((row == sg) & (blk8 > r - 8 * g))
                rank[g] = rank[g] + jnp.where(beats, 1.0, 0.0)
            if r % 4 == 3:
                yield
        selected = (jnp.concatenate(rank, axis=0) < k_top) & causal
        selb_scr[h, 0:n_groups * SUBLANES, :] = both(keep_bias(selected))

    r_i = lax.broadcasted_iota(jnp.int32, (t, t), 0)
    q_i = lax.broadcasted_iota(jnp.int32, (t, t), 1)
    diag_bias = both(keep_bias(r_i <= q_i))
    ones_rows = jnp.where(lax.broadcasted_iota(jnp.int32, (V_AUG_ROWS, t), 0) == 0,
                          1.0, 0.0).astype(MXU_DTYPE)

    def k_chunk(k_ref, c):
        start = c * t if isinstance(c, int) else pl.multiple_of(c * t, t)
        return k_ref[0, pl.ds(start, t), :]

    def v_aug(vt_ref, c, h):
        return jnp.concatenate([vt_ref[0, c, h * HEAD_DIM:(h + 1) * HEAD_DIM, :], ones_rows], axis=0)

    def colmax8(s):
        return jnp.max(s.reshape(s.shape[0] // 8, 8, s.shape[1]), axis=0)

    def weighted_values(s, m, vaug):
        p = jnp.exp2(s - m).astype(MXU_DTYPE)
        return jnp.dot(vaug, p, preferred_element_type=f32)

    def normalise(acc):
        return acc[0:HEAD_DIM] * (1.0 / acc[HEAD_DIM:HEAD_DIM + 1])

    def on_lanes(full, part, lanes, op):
        pieces = [full[:, :lanes.start]] if lanes.start else []
        pieces.append(op(full[:, lanes], part))
        if lanes.stop < full.shape[1]:
            pieces.append(full[:, lanes.stop:])
        return pieces[0] if len(pieces) == 1 else jnp.concatenate(pieces, axis=1)

    top, bot = slice(0, hq), slice(hq, t)
    lo_q, hi_q, all_q = slice(0, t), slice(t, t2), slice(0, t2)
    full_pieces = ((slice(0, t), all_q),)
    diag_pieces = ((top, all_q), (bot, hi_q))
    first_pieces = ((top, lo_q), (bot, all_q))

    heads = range(N_KV_HEADS)
    neg8 = jnp.full((8, t2), NEG_INF, f32)

    def sel_scores(c, k, h, rows, lanes, extra):
        s = jnp.dot(k[rows], qpads[h][:, lanes], preferred_element_type=f32)
        parts = []
        for r in range((rows.stop - rows.start) // SEL_BLOCK):
            blk_rows = slice(r * SEL_BLOCK, (r + 1) * SEL_BLOCK)
            bias = selb_scr[h, pl.ds(c * bpc + rows.start // SEL_BLOCK + r, 1), lanes]
            if extra is not None:
                bias = bias + (extra[blk_rows] if extra.shape[0] > 1 else extra)
            parts.append(s[blk_rows] + bias)
        return jnp.concatenate(parts, axis=0)

    def win_scores(c, k, h, rows, lanes, extra):
        s = jnp.dot(k[rows], qpads[h][:, lanes], preferred_element_type=f32)
        return s if extra is None else s + extra

    def softmax_branch(segments, k_ref, vt_ref, score_fn):
        keys = {c: k_chunk(k_ref, c) for c in sorted({w[0] for seg in segments for w in seg})}
        m_run = [None for _ in heads]
        accs = [jnp.zeros((HEAD_DIM + V_AUG_ROWS, t2), f32) for _ in heads]
        phases = []
        for seg in segments:
            ss = [[] for _ in heads]
            pm = [neg8 for _ in heads]

            def pass1(item, ss=ss, pm=pm):
                c, rows, lanes, extra = item
                for h in heads:
                    s = score_fn(c, keys[c], h, rows, lanes, extra)
                    ss[h].append(s)
                    pm[h] = on_lanes(pm[h], colmax8(s), lanes, jnp.maximum)

            def fix(pm=pm):
                for h in heads:
                    m_new = jnp.max(pm[h], axis=0, keepdims=True)
                    if m_run[h] is not None:
                        m_new = jnp.maximum(m_run[h], m_new)
                        accs[h] = accs[h] * jnp.exp2(m_run[h] - m_new)
                    m_run[h] = m_new

            def pass2(n, seg=seg, ss=ss):
                c, rows, lanes, _ = seg[n]
                for h in heads:
                    part = weighted_values(ss[h][n], m_run[h][:, lanes], v_aug(vt_ref, c, h)[:, rows])
                    accs[h] = on_lanes(accs[h], part, lanes, lambda a, b: a + b)

            phases.append(([functools.partial(pass1, item) for item in seg], fix,
                           [functools.partial(pass2, n) for n in range(len(seg))]))
        return phases, lambda: [normalise(accs[h]) for h in heads]

    sel_items = [[(c, rows, lanes, None) for rows, lanes in full_pieces] for c in range(i)]
    sel_items.append([(i, rows, lanes, diag_bias[rows, lanes]) for rows, lanes in diag_pieces])
    sel_segments = [sum(sel_items[n:n + SEL_SEGMENT], []) for n in range(0, i + 1, SEL_SEGMENT)]

    first_bias = both(keep_bias(r_i > q_i))
    win_segments = []
    for cc in range(nw, -1, -1):
        ci = i - nw + cc
        if ci < 0:
            continue
        if cc == nw:
            seg = [(ci, rows, lanes, diag_bias[rows, lanes]) for rows, lanes in diag_pieces]
        elif cc == 0:
            seg = [(ci, rows, lanes, first_bias[rows, lanes]) for rows, lanes in first_pieces]
        else:
            seg = [(ci, rows, lanes, None) for rows, lanes in full_pieces]
        win_segments.append(seg)

    sel_phases, sel_out = softmax_branch(sel_segments, ks_ref, vst_ref, sel_scores)
    win_phases, win_out = softmax_branch(win_segments, kw_ref, vwt_ref, win_scores)

    def alternate(a, b):
        for n in range(max(len(a), len(b))):
            for steps in (a, b):
                if n < len(steps):
                    steps[n]()

    for h in heads:
        for _ in compressed_and_select(h):
            pass
    pending = []
    for steps1, fix, steps2 in win_phases + sel_phases:
        alternate(steps1, pending)
        fix()
        pending = steps2
    for step in pending:
        step()
    o_sel, o_win = sel_out(), win_out()

    sig = jax.nn.sigmoid(ngt_ref[0, i])
    out_rows = []
    for h in heads:
        def gate(r):
            a = (2 * h) * 3 + r
            b = (2 * h + 1) * 3 + r
            return lanes_of(sig[a:a + 1, :], sig[b:b + 1, :])

        o = gate(0) * o_cmp[h] + gate(1) * o_sel[h] + gate(2) * o_win[h]
        out_rows += [jnp.concatenate([o[:, 0:hq], o[:, t:t + hq]], axis=1),
                     jnp.concatenate([o[:, hq:t], o[:, t + hq:t2]], axis=1)]
    y_ref[0, i * t:(i + 1) * t, :] = jnp.concatenate(out_rows, axis=0).T.astype(y_ref.dtype)


def _attn_call(qt, ngt, kcmp, vcmpt, ks, vst, kw, vwt):
    B, n_tiles, _, t = qt.shape
    S = n_tiles * t
    assert t == T_ATT == CH_ATT and WINDOW % t == 0 and (S // SEL_BLOCK) % 8 == 0
    assert SEL_BLOCK % CMP_STRIDE == 0 and CMP_LEN % CMP_STRIDE == 0 and CMP_LEN // CMP_STRIDE <= CMP_PAD
    nc = S // CMP_STRIDE
    n_sel = S // SEL_BLOCK
    per_b = lambda a: pl.BlockSpec((1,) + a.shape[1:], lambda b: (b,) + (0,) * (a.ndim - 1))
    ins = (qt, ngt, kcmp, vcmpt, ks, vst, kw, vwt)
    return pl.pallas_call(
        functools.partial(_attn_kernel, t=t, seq_len=S),
        grid=(B,),
        in_specs=[per_b(a) for a in ins],
        out_specs=pl.BlockSpec((1, S, MIX_W), lambda b: (b, 0, 0)),
        out_shape=jax.ShapeDtypeStruct((B, S, MIX_W), MXU_DTYPE),
        scratch_shapes=[
            pltpu.VMEM((N_KV_HEADS, n_sel, 2 * t), jnp.float32),
            pltpu.VMEM((t // LANES, CMP_PAD + nc, LANES), jnp.float32),
        ],
        compiler_params=pltpu.CompilerParams(
            dimension_semantics=("arbitrary",), vmem_limit_bytes=VMEM_LIMIT),
        name="nsa_attention",
    )(*ins)


def _merge_mlp_kernel(x_ref, yacd_ref, yb_ref, g1_ref, wbr_ref, wg_ref, bg_ref, wo_ref,
                      g2_ref, w1_ref, w2_ref, o_ref):
    x = x_ref[0]
    xn = _rms_rows(x, g1_ref[0]).astype(MXU_DTYPE)
    ys = (yacd_ref[0, :, 0:MIX_W], yb_ref[0], yacd_ref[0, :, MIX_W:2 * MIX_W],
          yacd_ref[0, :, 2 * MIX_W:3 * MIX_W])
    mixed = jnp.zeros(x.shape, jnp.float32)
    for n, y in enumerate(ys):
        cols = slice(n * D_MODEL, (n + 1) * D_MODEL)
        proj = jnp.dot(y, wbr_ref[0, n], preferred_element_type=jnp.float32)
        logits = jnp.dot(xn, wg_ref[0, :, cols], preferred_element_type=jnp.float32) + bg_ref[0, :, cols]
        mixed = mixed + jax.nn.sigmoid(logits) * proj
    h = x + jnp.dot(mixed.astype(MXU_DTYPE), wo_ref[0], preferred_element_type=jnp.float32)

    hn = _rms_rows(h, g2_ref[0]).astype(MXU_DTYPE)
    acc = h
    for c in range(D_FF // D_MODEL):
        cols = slice(c * D_MODEL, (c + 1) * D_MODEL)
        hid = jnp.dot(hn, w1_ref[0, :, cols], preferred_element_type=jnp.float32)
        hid = jnp.square(jnp.maximum(hid, 0.0)).astype(MXU_DTYPE)
        acc = acc + jnp.dot(hid, w2_ref[0, cols, :], preferred_element_type=jnp.float32)
    o_ref[0] = acc


def _merge_mlp_call(x, l, yacd, yb, g1, wbr, wg, bg, wo, g2, w1, w2):
    B, S, d = x.shape
    tm = min(TM_MERGE, S)
    rows = lambda w: pl.BlockSpec((1, tm, w), lambda b, i: (b, i, 0))
    stacked = (g1, wbr, wg, bg, wo, g2, w1, w2)
    return pl.pallas_call(
        _merge_mlp_kernel,
        grid=(B, S // tm),
        in_specs=[rows(d), rows(3 * MIX_W), rows(MIX_W)] + [_layer_spec(a, l) for a in stacked],
        out_specs=rows(d),
        out_shape=jax.ShapeDtypeStruct((B, S, d), jnp.float32),
        compiler_params=pltpu.CompilerParams(
            dimension_semantics=("arbitrary", "arbitrary"), vmem_limit_bytes=VMEM_LIMIT),
        name="merge_mlp",
    )(x, yacd, yb, *stacked)


def _split_w_in_t(w_in, l):
    wt = jnp.swapaxes(w_in, 1, 2)[l]
    wng = jnp.pad(wt[_NG_SRC:_NG_SRC + NSA_GATE_W], ((0, LANES - NSA_GATE_W), (0, 0)))
    return (wt[:_NG_SRC].astype(MXU_DTYPE), wng.astype(MXU_DTYPE),
            wt[_NG_SRC + NSA_GATE_W:].astype(MXU_DTYPE))


def _compress_weights(pe, w1, w2):
    assert N_KV_HEADS == 2
    L = pe.shape[0]
    w1r = w1.reshape(L, 2, CMP_LEN, HEAD_DIM, CMP_HIDDEN)

    def block_diag(w):
        z = jnp.zeros_like(w)
        return jnp.concatenate([jnp.concatenate([w, z], axis=-1),
                                jnp.concatenate([z, w], axis=-1)], axis=-2)

    def expand(w):
        return block_diag(w).reshape(
            L, 2, CMP_STRIDE * KV_W, N_KV_HEADS * CMP_HIDDEN).astype(MXU_DTYPE)

    def expand_pe(p):
        return jnp.concatenate([p, p], axis=-1).reshape(L, 2, 1, CMP_STRIDE * KV_W)

    w2b = block_diag(w2).astype(MXU_DTYPE)
    return (expand_pe(pe[:, :, :CMP_STRIDE]), expand_pe(pe[:, :, CMP_STRIDE:]),
            expand(w1r[:, :, :CMP_STRIDE]), expand(w1r[:, :, CMP_STRIDE:]), w2b)


def _layer_mixers(x, l, p):
    (yacd, qt, kc, vc, ks, vst, kw, vwt, ngt) = _proj_call(
        x, l, *_split_w_in_t(p["w_in"], l), p["norm1_g"], p["gmlp_ln_g"], p["gmlp_ln_b"], p["gmlp_ws"],
        p["bst"], p["qg"], p["kg"], p["conf_conv_w"], p["conf_conv_b"], p["conf_ln_g"],
        p["conf_ln_b"], p["sconv_w"])
    kcmp, vcmpt = _compress_call(kc, vc, l, *p["cmp_w"], p["kg"])
    yb = _attn_call(qt, ngt, kcmp, vcmpt, ks, vst, kw, vwt)
    return yacd, yb


def _prepare(p):
    q = dict(p)
    q["bst"] = jnp.repeat(jnp.swapaxes(p["gmlp_bs"], 1, 2), MIX_W // GMLP_GROUPS, axis=2)
    qg = jnp.tile(p["nsa_q_norm_g"], (1, MIX_W // HEAD_DIM))
    q["qg"] = jnp.broadcast_to(qg[:, :, None], qg.shape + (LANES,))
    q["kg"] = jnp.tile(p["nsa_k_norm_g"], (1, 1, N_KV_HEADS))
    q["cmp_w"] = _compress_weights(p["nsa_cmp_pe"], p["nsa_cmp_w1"], p["nsa_cmp_w2"])
    for name in ("w_branch", "w_gate", "w_out", "w_mlp1", "w_mlp2"):
        q[name] = p[name].astype(MXU_DTYPE)
    for name in ("norm1_g", "gmlp_ln_g", "gmlp_ln_b", "conf_conv_b", "conf_ln_g", "conf_ln_b",
                 "b_gate", "norm2_g"):
        q[name] = p[name][:, None, :]
    return q


def kernel(x, norm1_g, w_in, gmlp_ln_g, gmlp_ln_b, gmlp_ws, gmlp_bs, nsa_q_norm_g, nsa_k_norm_g,
           nsa_cmp_pe, nsa_cmp_w1, nsa_cmp_w2, conf_conv_w, conf_conv_b, conf_ln_g, conf_ln_b,
           sconv_w, w_branch, w_gate, b_gate, w_out, norm2_g, w_mlp1, w_mlp2):
    B, S, D = x.shape
    assert D == D_MODEL and S % min(TM_PROJ, S) == 0 and S % min(T_ATT, S) == 0 and S % CH_ATT == 0
    p = _prepare(dict(
        norm1_g=norm1_g, w_in=w_in, gmlp_ln_g=gmlp_ln_g, gmlp_ln_b=gmlp_ln_b, gmlp_ws=gmlp_ws,
        gmlp_bs=gmlp_bs, nsa_q_norm_g=nsa_q_norm_g, nsa_k_norm_g=nsa_k_norm_g,
        nsa_cmp_pe=nsa_cmp_pe, nsa_cmp_w1=nsa_cmp_w1, nsa_cmp_w2=nsa_cmp_w2,
        conf_conv_w=conf_conv_w, conf_conv_b=conf_conv_b, conf_ln_g=conf_ln_g, conf_ln_b=conf_ln_b,
        sconv_w=sconv_w, w_branch=w_branch, w_gate=w_gate, b_gate=b_gate, w_out=w_out,
        norm2_g=norm2_g, w_mlp1=w_mlp1, w_mlp2=w_mlp2))
    for l in range(norm1_g.shape[0]):
        yacd, yb = _layer_mixers(x, l, p)
        x = _merge_mlp_call(x, l, yacd, yb, p["norm1_g"], p["w_branch"], p["w_gate"], p["b_gate"],
                            p["w_out"], p["norm2_g"], p["w_mlp1"], p["w_mlp2"])
    return x
```

```python
import functools

import jax
import jax.numpy as jnp
from jax import lax
from jax.experimental import pallas as pl
from jax.experimental.pallas import tpu as pltpu

D_MODEL = 1024
MIX_W = 256
HEAD_DIM = 64
N_KV_HEADS = 2
Q_PER_KV = 2
KV_W = N_KV_HEADS * HEAD_DIM
GMLP_GROUPS = 4
GMLP_CHUNK = 128
CMP_LEN = 32
CMP_STRIDE = 16
CMP_HIDDEN = 128
SEL_BLOCK = 64
SEL_TOPK = 8
N_LOCAL_BLOCKS = 2
WINDOW = 512
CONF_KERNEL = 31
SCONV_KERNEL = 3
D_FF = 4 * D_MODEL
NSA_GATE_W = 12
NEG_INF = -1e30

LANES = 128
SUBLANES = 8
MXU_DTYPE = jnp.bfloat16
VMEM_LIMIT = 56 * 1024 * 1024

_C_GU, _C_GV, _C_Q = 0, 256, 512
_C_KC, _C_VC, _C_KS, _C_VS, _C_KW, _C_VW = 768, 896, 1024, 1152, 1280, 1408
_C_NG = 1536
_C_CA, _C_CB, _C_SB, _C_SC, _C_SH = 1664, 1920, 2176, 2432, 2688
_NG_SRC = 1536

TM_PROJ = 1024
TM_MERGE = 512
T_ATT = 256
CH_ATT = 256
LOG2E = 1.4426950408889634
V_AUG_ROWS = 16
CMP_PAD = 8
SEL_SEGMENT = 1
CONV_HALO = 32
SCONV_HALO = 8


def _rms_rows(x, g, eps=1e-6):
    return x * lax.rsqrt(jnp.mean(x * x, axis=-1, keepdims=True) + eps) * g


def _layernorm_rows(x, g, b, eps=1e-5):
    mu = jnp.mean(x, axis=-1, keepdims=True)
    xc = x - mu
    var = jnp.mean(xc * xc, axis=-1, keepdims=True)
    return xc * lax.rsqrt(var + eps) * g + b


def _head_rms_rows(x, g, eps=1e-6):
    n_heads = x.shape[-1] // HEAD_DIM
    head = lax.broadcasted_iota(jnp.int32, x.shape, 1) >> 6
    xx = x * x
    scale = jnp.zeros_like(x)
    for h in range(n_heads):
        ms = jnp.sum(jnp.where(head == h, xx, 0.0), axis=-1, keepdims=True) * (1.0 / HEAD_DIM)
        scale = jnp.where(head == h, lax.rsqrt(ms + eps), scale)
    return x * scale * g


def _layer_spec(a, l):
    return pl.BlockSpec((1,) + a.shape[1:], lambda *_: (l,) + (0,) * (a.ndim - 1))


def _proj_kernel(x_ref, g1_ref, wa_ref, wng_ref, wb_ref, lng_ref, lnb_ref, ws_ref, bst_ref, qg_ref,
                 kg_ref, cw_ref, cbias_ref, clg_ref, clb_ref, sw_ref,
                 yacd_ref, qt_ref, kc_ref, vc_ref, ks_ref, vst_ref, kw_ref, vwt_ref, ngt_ref,
                 zbuf, cbuf, *, tm, ch):
    @pl.when(pl.program_id(1) == 0)
    def _():
        zbuf[...] = jnp.zeros((CONV_HALO, MIX_W), jnp.float32)
        cbuf[...] = jnp.zeros((SCONV_HALO, MIX_W), jnp.float32)

    x = x_ref[0]
    xn = _rms_rows(x, g1_ref[0]).astype(MXU_DTYPE)

    def proj(lo, width):
        if lo < _C_NG:
            w = wa_ref[lo:lo + width, :]
        elif lo == _C_NG:
            w = wng_ref[...]
        else:
            w = wb_ref[lo - _C_CA:lo - _C_CA + width, :]
        return lax.dot_general(xn, w, (((1,), (1,)), ((), ())), preferred_element_type=jnp.float32)

    def causal_conv(buf, w, width, halo):
        offsets = [halo - (width - 1) + j for j in range(width)]
        out = jnp.zeros((tm, MIX_W), jnp.float32)
        for b in range(SUBLANES):
            taps = [j for j in range(width) if offsets[j] % SUBLANES == b]
            if not taps:
                continue
            rows = tm if b == 0 else tm + SUBLANES
            part = jnp.zeros((rows, MIX_W), jnp.float32)
            for j in taps:
                part = part + w[j:j + 1] * buf[offsets[j] - b:offsets[j] - b + rows]
            out = out + part[b:b + tm]
            yield out


    z = proj(_C_CA, MIX_W) * jax.nn.sigmoid(proj(_C_CB, MIX_W))
    conv = causal_conv(jnp.concatenate([zbuf[...], z], axis=0), cw_ref[0], CONF_KERNEL, CONV_HALO)
    zbuf[...] = z[tm - CONV_HALO:tm]

    def emit_q():
        qt = proj(_C_Q, MIX_W).T
        gq = jnp.concatenate([qg_ref[0]] * (tm // LANES), axis=1)
        pieces = []
        for h in range(MIX_W // HEAD_DIM):
            blk = qt[h * HEAD_DIM:(h + 1) * HEAD_DIM]
            ms = jnp.sum(blk * blk, axis=0, keepdims=True) * (1.0 / HEAD_DIM)
            pieces.append(blk * lax.rsqrt(ms + 1e-6))
        qn = (jnp.concatenate(pieces, axis=0) * gq * (HEAD_DIM ** -0.5 * LOG2E)).astype(qt_ref.dtype)
        for c in range(tm // ch):
            qt_ref[0, c] = qn[:, c * ch:(c + 1) * ch]

    def emit_raw(dst_ref, col):
        dst_ref[0] = proj(col, KV_W)

    def emit_key(dst_ref, col, branch):
        dst_ref[0] = _head_rms_rows(proj(col, KV_W), kg_ref[0, branch:branch + 1, :]).astype(dst_ref.dtype)

    def emit_value_t(dst_ref, col):
        vt = proj(col, KV_W).T
        for c in range(tm // ch):
            dst_ref[0, c] = vt[:, c * ch:(c + 1) * ch].astype(dst_ref.dtype)

    def emit_gates():
        ngt = proj(_C_NG, LANES).T[0:16, :]
        for c in range(tm // ch):
            ngt_ref[0, c] = ngt[:, c * ch:(c + 1) * ch]

    held = {}

    def hold(name, col):
        held[name] = proj(col, MIX_W)

    slices = [
        [lambda: emit_raw(kc_ref, _C_KC), lambda: emit_raw(vc_ref, _C_VC)],
        [lambda: emit_key(ks_ref, _C_KS, 1), lambda: emit_key(kw_ref, _C_KW, 2)],
        [lambda: emit_value_t(vst_ref, _C_VS), lambda: emit_value_t(vwt_ref, _C_VW)],
        [emit_q],
        [emit_gates, lambda: hold("sb", _C_SB)],
        [lambda: hold("sc", _C_SC), lambda: hold("sh", _C_SH)],
        [lambda: hold("gu", _C_GU)],
        [lambda: hold("gv", _C_GV)],
    ]
    acc = None
    for tasks in slices:
        for task in tasks:
            task()
        acc = next(conv, acc)
    for acc in conv:
        pass
    yacd_ref[0, :, MIX_W:2 * MIX_W] = jax.nn.silu(_layernorm_rows(
        acc + cbias_ref[0], clg_ref[0], clb_ref[0])).astype(yacd_ref.dtype)

    sc = held["sc"] * held["sh"]
    for acc in causal_conv(jnp.concatenate([cbuf[...], sc], axis=0), sw_ref[0], SCONV_KERNEL, SCONV_HALO):
        pass
    cbuf[...] = sc[tm - SCONV_HALO:tm]
    yacd_ref[0, :, 2 * MIX_W:3 * MIX_W] = (held["sb"] * acc).astype(yacd_ref.dtype)

    u = jax.nn.gelu(held["gu"])
    v = jax.nn.gelu(held["gv"])
    v = _layernorm_rows(v, lng_ref[0], lnb_ref[0]).astype(MXU_DTYPE)
    tri = (lax.broadcasted_iota(jnp.int32, (GMLP_CHUNK, GMLP_CHUNK), 0)
           >= lax.broadcasted_iota(jnp.int32, (GMLP_CHUNK, GMLP_CHUNK), 1))
    grp = lax.broadcasted_iota(jnp.int32, (GMLP_CHUNK, MIX_W), 1) >> 6
    wsm = [jnp.where(tri, ws_ref[0, g], 0.0).astype(MXU_DTYPE) for g in range(GMLP_GROUPS)]
    for c in range(tm // GMLP_CHUNK):
        rows = slice(c * GMLP_CHUNK, (c + 1) * GMLP_CHUNK)
        vch = v[rows]
        mixed = jnp.zeros((GMLP_CHUNK, MIX_W), jnp.float32)
        for g in range(GMLP_GROUPS):
            r = jnp.dot(wsm[g], vch, preferred_element_type=jnp.float32)
            mixed = jnp.where(grp == g, r, mixed)
        yacd_ref[0, rows, 0:MIX_W] = (u[rows] * (mixed + bst_ref[0])).astype(yacd_ref.dtype)


def _proj_call(x, l, wa, wng, wb, g1, lng, lnb, ws, bst, qg, kg, cw, cbias, clg, clb, sw):
    B, S, D = x.shape
    tm, ch = min(TM_PROJ, S), CH_ATT
    full = lambda a: pl.BlockSpec(a.shape, lambda b, i: (0,) * a.ndim)
    seq = lambda width: pl.BlockSpec((1, tm, width), lambda b, i: (b, i, 0))
    stacked = (lng, lnb, ws, bst, qg, kg, cw, cbias, clg, clb, sw)
    ins = (g1, wa, wng, wb) + stacked
    in_specs = ([seq(D), _layer_spec(g1, l), full(wa), full(wng), full(wb)]
                + [_layer_spec(a, l) for a in stacked])
    out_shape = (
        jax.ShapeDtypeStruct((B, S, 3 * MIX_W), MXU_DTYPE),
        jax.ShapeDtypeStruct((B, S // ch, MIX_W, ch), MXU_DTYPE),
        jax.ShapeDtypeStruct((B, S, KV_W), jnp.float32),
        jax.ShapeDtypeStruct((B, S, KV_W), jnp.float32),
        jax.ShapeDtypeStruct((B, S, KV_W), MXU_DTYPE),
        jax.ShapeDtypeStruct((B, S // ch, KV_W, ch), MXU_DTYPE),
        jax.ShapeDtypeStruct((B, S, KV_W), MXU_DTYPE),
        jax.ShapeDtypeStruct((B, S // ch, KV_W, ch), MXU_DTYPE),
        jax.ShapeDtypeStruct((B, S // ch, 16, ch), jnp.float32),
    )
    vt_spec = pl.BlockSpec((1, tm // ch, KV_W, ch), lambda b, i: (b, i, 0, 0))
    out_specs = (
        seq(3 * MIX_W),
        pl.BlockSpec((1, tm // ch, MIX_W, ch), lambda b, i: (b, i, 0, 0)),
        seq(KV_W), seq(KV_W), seq(KV_W), vt_spec, seq(KV_W), vt_spec,
        pl.BlockSpec((1, tm // ch, 16, ch), lambda b, i: (b, i, 0, 0)),
    )
    return pl.pallas_call(
        functools.partial(_proj_kernel, tm=tm, ch=ch),
        grid=(B, S // tm),
        in_specs=in_specs,
        out_specs=out_specs,
        out_shape=out_shape,
        scratch_shapes=[pltpu.VMEM((CONV_HALO, MIX_W), jnp.float32),
                        pltpu.VMEM((SCONV_HALO, MIX_W), jnp.float32)],
        compiler_params=pltpu.CompilerParams(
            dimension_semantics=("arbitrary", "arbitrary"), vmem_limit_bytes=VMEM_LIMIT),
        name="proj_mixers",
    )(x, *ins)


def _compress_kernel(kc_ref, vc_ref, pet_ref, peb_ref, wt_ref, wb_ref, w2_ref, kg_ref,
                     kcmp_ref, vcmpt_ref):
    nc = kc_ref.shape[1] // CMP_STRIDE

    def compress(src_ref, idx):
        chunks = jnp.concatenate(
            [src_ref[0, pl.ds(l, nc, stride=CMP_STRIDE), :] for l in range(CMP_STRIDE)], axis=1)
        a1 = jnp.dot((chunks + pet_ref[0, idx]).astype(MXU_DTYPE), wt_ref[0, idx],
                     preferred_element_type=jnp.float32)
        a2 = jnp.dot((chunks + peb_ref[0, idx]).astype(MXU_DTYPE), wb_ref[0, idx],
                     preferred_element_type=jnp.float32)
        hid = jax.nn.gelu(a1 + pltpu.roll(a2, shift=nc - 1, axis=0))
        return jnp.dot(hid.astype(MXU_DTYPE), w2_ref[0, idx], preferred_element_type=jnp.float32)

    kcmp_ref[0] = _head_rms_rows(compress(kc_ref, 0), kg_ref[0, 0:1, :]).astype(kcmp_ref.dtype)
    vcmpt_ref[0] = compress(vc_ref, 1).T.astype(vcmpt_ref.dtype)


def _compress_call(kc, vc, l, pet, peb, wt, wb, w2, kg):
    B, S, _ = kc.shape
    nc = S // CMP_STRIDE
    per_b = lambda r, c: pl.BlockSpec((1, r, c), lambda b: (b, 0, 0))
    ins = (pet, peb, wt, wb, w2, kg)
    return pl.pallas_call(
        _compress_kernel,
        grid=(B,),
        in_specs=[per_b(S, KV_W)] * 2 + [_layer_spec(a, l) for a in ins],
        out_specs=(per_b(nc, KV_W), per_b(KV_W, nc)),
        out_shape=(jax.ShapeDtypeStruct((B, nc, KV_W), MXU_DTYPE),
                   jax.ShapeDtypeStruct((B, KV_W, nc), MXU_DTYPE)),
        compiler_params=pltpu.CompilerParams(
            dimension_semantics=("arbitrary",), vmem_limit_bytes=VMEM_LIMIT),
        name="nsa_compress",
    )(kc, vc, *ins)


def _attn_kernel(*refs, t, seq_len):
    for i in range(seq_len // t):
        _attn_tile(i, *refs, t=t, seq_len=seq_len)


def _attn_tile(i, qt_ref, ngt_ref, kcmp_ref, vcmpt_ref, ks_ref, vst_ref, kw_ref, vwt_ref,
               y_ref, selb_scr, pcmp_scr, *, t, seq_len):
    nc = seq_len // CMP_STRIDE
    n_cmp = nc - 1
    n_sel = seq_len // SEL_BLOCK
    k_top = min(SEL_TOPK, n_sel)
    bpc = t // SEL_BLOCK
    cmp_per_sel = SEL_BLOCK // CMP_STRIDE
    nw = WINDOW // t
    t2 = 2 * t
    hq = t // 2
    f32 = jnp.float32
    n_cmp_live = min(nc, (i + 1) * t // CMP_STRIDE)
    n_sel_live = (i + 1) * bpc
    n_groups = -(-n_sel_live // SUBLANES)

    def lanes_of(a0, a1):
        return jnp.concatenate([a0[:, :hq], a1[:, :hq], a0[:, hq:], a1[:, hq:]], axis=1)

    def both(a):
        return lanes_of(a, a)

    def keep_bias(cond):
        return jnp.where(cond, 0.0, NEG_INF).astype(f32)

    qt = qt_ref[0, i]
    qpads = []
    for h in range(N_KV_HEADS):
        qh = lanes_of(qt[(2 * h) * HEAD_DIM:(2 * h + 1) * HEAD_DIM],
                      qt[(2 * h + 1) * HEAD_DIM:(2 * h + 2) * HEAD_DIM])
        zero = jnp.zeros_like(qh)
        qpads.append(jnp.concatenate([qh, zero] if h == 0 else [zero, qh], axis=0))

    n_i = lax.broadcasted_iota(jnp.int32, (n_cmp_live, t), 0)
    qpos_c = i * t + lax.broadcasted_iota(jnp.int32, (n_cmp_live, t), 1)
    cmp_ok = (n_i * CMP_STRIDE + (CMP_LEN - 1) <= qpos_c) & (n_i < n_cmp)
    cmp_bias = both(keep_bias(cmp_ok))
    cmp_keep = both(jnp.where(cmp_ok, 1.0, 0.0).astype(f32))

    blk = lax.broadcasted_iota(jnp.int32, (n_groups * SUBLANES, t), 0)
    cur = (i * t + lax.broadcasted_iota(jnp.int32, (n_groups * SUBLANES, t), 1)) >> 6
    dist = cur - blk
    causal = dist >= 0
    forced = (blk == 0) | (causal & (dist < N_LOCAL_BLOCKS))
    blk8 = lax.broadcasted_iota(jnp.int32, (8, t), 0)

    kcmp = kcmp_ref[0, 0:n_cmp_live, :]
    pcmp_scr[:, 0:CMP_PAD, :] = jnp.zeros((t // LANES, CMP_PAD, LANES), f32)
    o_cmp = [None] * N_KV_HEADS

    def compressed_and_select(h):
        sm = jnp.dot(kcmp, qpads[h], preferred_element_type=f32) + cmp_bias
        e = jnp.exp2(sm - jnp.max(sm, axis=0, keepdims=True)) * cmp_keep
        den = jnp.sum(e, axis=0, keepdims=True)
        p_cmp = e * (1.0 / jnp.where(den > 0.0, den, 1.0))
        p_all = p_cmp.astype(MXU_DTYPE)
        if n_cmp_live < nc:
            p_all = jnp.concatenate([p_all, jnp.zeros((nc - n_cmp_live, t2), MXU_DTYPE)], axis=0)
        o_cmp[h] = jnp.dot(vcmpt_ref[0, h * HEAD_DIM:(h + 1) * HEAD_DIM, :], p_all,
                           preferred_element_type=f32)

        p_pair = jnp.concatenate([p_cmp[:, 0:hq] + p_cmp[:, hq:t],
                                  p_cmp[:, t:t + hq] + p_cmp[:, t + hq:t2]], axis=1)
        slabs = []
        for lb in range(t // LANES):
            pcmp_scr[lb, CMP_PAD:CMP_PAD + n_cmp_live, :] = p_pair[:, lb * LANES:(lb + 1) * LANES]
            slabs.append(functools.reduce(lambda a, b: a + b, [
                pcmp_scr[lb, pl.ds(CMP_PAD + off, n_sel_live, stride=cmp_per_sel), :]
                for off in range(1 - CMP_LEN // CMP_STRIDE, cmp_per_sel)]))
        p_slc = jnp.concatenate(slabs, axis=1)
        if n_sel_live < n_groups * SUBLANES:
            p_slc = jnp.concatenate(
                [p_slc, jnp.zeros((n_groups * SUBLANES - n_sel_live, t), f32)], axis=0)
        score = jnp.where(forced, jnp.inf, jnp.where(causal, p_slc, -jnp.inf))
        groups = [score[8 * g:8 * g + 8] for g in range(n_groups)]
        rank = [jnp.zeros((8, t), f32) for _ in groups]
        for r in range(n_sel_live):
            row = score[r:r + 1, :]
            for g, sg in enumerate(groups):
                if 8 * g > r:
                    beats = row >= sg
                elif 8 * g + 7 <= r:
                    beats = row > sg
                else:
                    beats = (row > sg) | ((row == sg) & (blk8 > r - 8 * g))
                rank[g] = rank[g] + jnp.where(beats, 1.0, 0.0)
        selected = (jnp.concatenate(rank, axis=0) < k_top) & causal
        selb_scr[h, 0:n_groups * SUBLANES, :] = both(keep_bias(selected))

    r_i = lax.broadcasted_iota(jnp.int32, (t, t), 0)
    q_i = lax.broadcasted_iota(jnp.int32, (t, t), 1)
    diag_bias = both(keep_bias(r_i <= q_i))
    ones_rows = jnp.where(lax.broadcasted_iota(jnp.int32, (V_AUG_ROWS, t), 0) == 0,
                          1.0, 0.0).astype(MXU_DTYPE)

    def k_chunk(k_ref, c):
        start = c * t if isinstance(c, int) else pl.multiple_of(c * t, t)
        return k_ref[0, pl.ds(start, t), :]

    def v_aug(vt_ref, c, h):
        return jnp.concatenate([vt_ref[0, c, h * HEAD_DIM:(h + 1) * HEAD_DIM, :], ones_rows], axis=0)

    def colmax8(s):
        return jnp.max(s.reshape(s.shape[0] // 8, 8, s.shape[1]), axis=0)

    def weighted_values(s, m, vaug):
        p = jnp.exp2(s - m).astype(MXU_DTYPE)
        return jnp.dot(vaug, p, preferred_element_type=f32)

    def normalise(acc):
        return acc[0:HEAD_DIM] * (1.0 / acc[HEAD_DIM:HEAD_DIM + 1])

    def on_lanes(full, part, lanes, op):
        pieces = [full[:, :lanes.start]] if lanes.start else []
        pieces.append(op(full[:, lanes], part))
        if lanes.stop < full.shape[1]:
            pieces.append(full[:, lanes.stop:])
        return pieces[0] if len(pieces) == 1 else jnp.concatenate(pieces, axis=1)

    top, bot = slice(0, hq), slice(hq, t)
    lo_q, hi_q, all_q = slice(0, t), slice(t, t2), slice(0, t2)
    full_pieces = ((slice(0, t), all_q),)
    diag_pieces = ((top, all_q), (bot, hi_q))
    first_pieces = ((top, lo_q), (bot, all_q))

    heads = range(N_KV_HEADS)
    neg8 = jnp.full((8, t2), NEG_INF, f32)

    def sel_scores(c, k, h, rows, lanes, extra):
        s = jnp.dot(k[rows], qpads[h][:, lanes], preferred_element_type=f32)
        parts = []
        for r in range((rows.stop - rows.start) // SEL_BLOCK):
            blk_rows = slice(r * SEL_BLOCK, (r + 1) * SEL_BLOCK)
            bias = selb_scr[h, pl.ds(c * bpc + rows.start // SEL_BLOCK + r, 1), lanes]
            if extra is not None:
                bias = bias + (extra[blk_rows] if extra.shape[0] > 1 else extra)
            parts.append(s[blk_rows] + bias)
        return jnp.concatenate(parts, axis=0)

    def win_scores(c, k, h, rows, lanes, extra):
        s = jnp.dot(k[rows], qpads[h][:, lanes], preferred_element_type=f32)
        return s if extra is None else s + extra

    def softmax_branch(segments, k_ref, vt_ref, score_fn):
        keys = {c: k_chunk(k_ref, c) for c in sorted({w[0] for seg in segments for w in seg})}
        m_run = [None for _ in heads]
        accs = [jnp.zeros((HEAD_DIM + V_AUG_ROWS, t2), f32) for _ in heads]
        phases = []
        for seg in segments:
            ss = [[] for _ in heads]
            pm = [neg8 for _ in heads]

            def pass1(item, ss=ss, pm=pm):
                c, rows, lanes, extra = item
                for h in heads:
                    s = score_fn(c, keys[c], h, rows, lanes, extra)
                    ss[h].append(s)
                    pm[h] = on_lanes(pm[h], colmax8(s), lanes, jnp.maximum)

            def fix(pm=pm):
                for h in heads:
                    m_new = jnp.max(pm[h], axis=0, keepdims=True)
                    if m_run[h] is not None:
                        m_new = jnp.maximum(m_run[h], m_new)
                        accs[h] = accs[h] * jnp.exp2(m_run[h] - m_new)
                    m_run[h] = m_new

            def pass2(n, seg=seg, ss=ss):
                c, rows, lanes, _ = seg[n]
                for h in heads:
                    part = weighted_values(ss[h][n], m_run[h][:, lanes], v_aug(vt_ref, c, h)[:, rows])
                    accs[h] = on_lanes(accs[h], part, lanes, lambda a, b: a + b)

            phases.append(([functools.partial(pass1, item) for item in seg], fix,
                           [functools.partial(pass2, n) for n in range(len(seg))]))
        return phases, lambda: [normalise(accs[h]) for h in heads]

    sel_items = [[(c, rows, lanes, None) for rows, lanes in full_pieces] for c in range(i)]
    sel_items.append([(i, rows, lanes, diag_bias[rows, lanes]) for rows, lanes in diag_pieces])
    sel_segments = [sum(sel_items[n:n + SEL_SEGMENT], []) for n in range(0, i + 1, SEL_SEGMENT)]

    first_bias = both(keep_bias(r_i > q_i))
    win_segments = []
    for cc in range(nw, -1, -1):
        ci = i - nw + cc
        if ci < 0:
            continue
        if cc == nw:
            seg = [(ci, rows, lanes, diag_bias[rows, lanes]) for rows, lanes in diag_pieces]
        elif cc == 0:
            seg = [(ci, rows, lanes, first_bias[rows, lanes]) for rows, lanes in first_pieces]
        else:
            seg = [(ci, rows, lanes, None) for rows, lanes in full_pieces]
        win_segments.append(seg)

    sel_phases, sel_out = softmax_branch(sel_segments, ks_ref, vst_ref, sel_scores)
    win_phases, win_out = softmax_branch(win_segments, kw_ref, vwt_ref, win_scores)

    def alternate(a, b):
        for n in range(max(len(a), len(b))):
            for steps in (a, b):
                if n < len(steps):
                    steps[n]()

    for h in heads:
        compressed_and_select(h)
    pending = []
    for steps1, fix, steps2 in win_phases + sel_phases:
        alternate(steps1, pending)
        fix()
        pending = steps2
    for step in pending:
        step()
    o_sel, o_win = sel_out(), win_out()

    sig = jax.nn.sigmoid(ngt_ref[0, i])
    out_rows = []
    for h in heads:
        def gate(r):
            a = (2 * h) * 3 + r
            b = (2 * h + 1) * 3 + r
            return lanes_of(sig[a:a + 1, :], sig[b:b + 1, :])

        o = gate(0) * o_cmp[h] + gate(1) * o_sel[h] + gate(2) * o_win[h]
        out_rows += [jnp.concatenate([o[:, 0:hq], o[:, t:t + hq]], axis=1),
                     jnp.concatenate([o[:, hq:t], o[:, t + hq:t2]], axis=1)]
    y_ref[0, i * t:(i + 1) * t, :] = jnp.concatenate(out_rows, axis=0).T.astype(y_ref.dtype)


def _attn_call(qt, ngt, kcmp, vcmpt, ks, vst, kw, vwt):
    B, n_tiles, _, t = qt.shape
    S = n_tiles * t
    assert t == T_ATT == CH_ATT and WINDOW % t == 0 and (S // SEL_BLOCK) % 8 == 0
    assert SEL_BLOCK % CMP_STRIDE == 0 and CMP_LEN % CMP_STRIDE == 0 and CMP_LEN // CMP_STRIDE <= CMP_PAD
    nc = S // CMP_STRIDE
    n_sel = S // SEL_BLOCK
    per_b = lambda a: pl.BlockSpec((1,) + a.shape[1:], lambda b: (b,) + (0,) * (a.ndim - 1))
    ins = (qt, ngt, kcmp, vcmpt, ks, vst, kw, vwt)
    return pl.pallas_call(
        functools.partial(_attn_kernel, t=t, seq_len=S),
        grid=(B,),
        in_specs=[per_b(a) for a in ins],
        out_specs=pl.BlockSpec((1, S, MIX_W), lambda b: (b, 0, 0)),
        out_shape=jax.ShapeDtypeStruct((B, S, MIX_W), MXU_DTYPE),
        scratch_shapes=[
            pltpu.VMEM((N_KV_HEADS, n_sel, 2 * t), jnp.float32),
            pltpu.VMEM((t // LANES, CMP_PAD + nc, LANES), jnp.float32),
        ],
        compiler_params=pltpu.CompilerParams(
            dimension_semantics=("arbitrary",), vmem_limit_bytes=VMEM_LIMIT),
        name="nsa_attention",
    )(*ins)


def _merge_mlp_kernel(x_ref, yacd_ref, yb_ref, g1_ref, wbr_ref, wg_ref, bg_ref, wo_ref,
                      g2_ref, w1_ref, w2_ref, o_ref):
    x = x_ref[0]
    xn = _rms_rows(x, g1_ref[0]).astype(MXU_DTYPE)
    ys = (yacd_ref[0, :, 0:MIX_W], yb_ref[0], yacd_ref[0, :, MIX_W:2 * MIX_W],
          yacd_ref[0, :, 2 * MIX_W:3 * MIX_W])
    mixed = jnp.zeros(x.shape, jnp.float32)
    for n, y in enumerate(ys):
        cols = slice(n * D_MODEL, (n + 1) * D_MODEL)
        proj = jnp.dot(y, wbr_ref[0, n], preferred_element_type=jnp.float32)
        logits = jnp.dot(xn, wg_ref[0, :, cols], preferred_element_type=jnp.float32) + bg_ref[0, :, cols]
        mixed = mixed + jax.nn.sigmoid(logits) * proj
    h = x + jnp.dot(mixed.astype(MXU_DTYPE), wo_ref[0], preferred_element_type=jnp.float32)

    hn = _rms_rows(h, g2_ref[0]).astype(MXU_DTYPE)
    acc = h
    for c in range(D_FF // D_MODEL):
        cols = slice(c * D_MODEL, (c + 1) * D_MODEL)
        hid = jnp.dot(hn, w1_ref[0, :, cols], preferred_element_type=jnp.float32)
        hid = jnp.square(jnp.maximum(hid, 0.0)).astype(MXU_DTYPE)
        acc = acc + jnp.dot(hid, w2_ref[0, cols, :], preferred_element_type=jnp.float32)
    o_ref[0] = acc


def _merge_mlp_call(x, l, yacd, yb, g1, wbr, wg, bg, wo, g2, w1, w2):
    B, S, d = x.shape
    tm = min(TM_MERGE, S)
    rows = lambda w: pl.BlockSpec((1, tm, w), lambda b, i: (b, i, 0))
    stacked = (g1, wbr, wg, bg, wo, g2, w1, w2)
    return pl.pallas_call(
        _merge_mlp_kernel,
        grid=(B, S // tm),
        in_specs=[rows(d), rows(3 * MIX_W), rows(MIX_W)] + [_layer_spec(a, l) for a in stacked],
        out_specs=rows(d),
        out_shape=jax.ShapeDtypeStruct((B, S, d), jnp.float32),
        compiler_params=pltpu.CompilerParams(
            dimension_semantics=("arbitrary", "arbitrary"), vmem_limit_bytes=VMEM_LIMIT),
        name="merge_mlp",
    )(x, yacd, yb, *stacked)


def _split_w_in_t(w_in, l):
    wt = jnp.swapaxes(w_in, 1, 2)[l]
    wng = jnp.pad(wt[_NG_SRC:_NG_SRC + NSA_GATE_W], ((0, LANES - NSA_GATE_W), (0, 0)))
    return (wt[:_NG_SRC].astype(MXU_DTYPE), wng.astype(MXU_DTYPE),
            wt[_NG_SRC + NSA_GATE_W:].astype(MXU_DTYPE))


def _compress_weights(pe, w1, w2):
    assert N_KV_HEADS == 2
    L = pe.shape[0]
    w1r = w1.reshape(L, 2, CMP_LEN, HEAD_DIM, CMP_HIDDEN)

    def block_diag(w):
        z = jnp.zeros_like(w)
        return jnp.concatenate([jnp.concatenate([w, z], axis=-1),
                                jnp.concatenate([z, w], axis=-1)], axis=-2)

    def expand(w):
        return block_diag(w).reshape(
            L, 2, CMP_STRIDE * KV_W, N_KV_HEADS * CMP_HIDDEN).astype(MXU_DTYPE)

    def expand_pe(p):
        return jnp.concatenate([p, p], axis=-1).reshape(L, 2, 1, CMP_STRIDE * KV_W)

    w2b = block_diag(w2).astype(MXU_DTYPE)
    return (expand_pe(pe[:, :, :CMP_STRIDE]), expand_pe(pe[:, :, CMP_STRIDE:]),
            expand(w1r[:, :, :CMP_STRIDE]), expand(w1r[:, :, CMP_STRIDE:]), w2b)


def _layer_mixers(x, l, p):
    (yacd, qt, kc, vc, ks, vst, kw, vwt, ngt) = _proj_call(
        x, l, *_split_w_in_t(p["w_in"], l), p["norm1_g"], p["gmlp_ln_g"], p["gmlp_ln_b"], p["gmlp_ws"],
        p["bst"], p["qg"], p["kg"], p["conf_conv_w"], p["conf_conv_b"], p["conf_ln_g"],
        p["conf_ln_b"], p["sconv_w"])
    kcmp, vcmpt = _compress_call(kc, vc, l, *p["cmp_w"], p["kg"])
    yb = _attn_call(qt, ngt, kcmp, vcmpt, ks, vst, kw, vwt)
    return yacd, yb


def _prepare(p):
    q = dict(p)
    q["bst"] = jnp.repeat(jnp.swapaxes(p["gmlp_bs"], 1, 2), MIX_W // GMLP_GROUPS, axis=2)
    qg = jnp.tile(p["nsa_q_norm_g"], (1, MIX_W // HEAD_DIM))
    q["qg"] = jnp.broadcast_to(qg[:, :, None], qg.shape + (LANES,))
    q["kg"] = jnp.tile(p["nsa_k_norm_g"], (1, 1, N_KV_HEADS))
    q["cmp_w"] = _compress_weights(p["nsa_cmp_pe"], p["nsa_cmp_w1"], p["nsa_cmp_w2"])
    for name in ("w_branch", "w_gate", "w_out", "w_mlp1", "w_mlp2"):
        q[name] = p[name].astype(MXU_DTYPE)
    for name in ("norm1_g", "gmlp_ln_g", "gmlp_ln_b", "conf_conv_b", "conf_ln_g", "conf_ln_b",
                 "b_gate", "norm2_g"):
        q[name] = p[name][:, None, :]
    return q


def kernel(x, norm1_g, w_in, gmlp_ln_g, gmlp_ln_b, gmlp_ws, gmlp_bs, nsa_q_norm_g, nsa_k_norm_g,
           nsa_cmp_pe, nsa_cmp_w1, nsa_cmp_w2, conf_conv_w, conf_conv_b, conf_ln_g, conf_ln_b,
           sconv_w, w_branch, w_gate, b_gate, w_out, norm2_g, w_mlp1, w_mlp2):
    B, S, D = x.shape
    assert D == D_MODEL and S % min(TM_PROJ, S) == 0 and S % min(T_ATT, S) == 0 and S % CH_ATT == 0
    p = _prepare(dict(
        norm1_g=norm1_g, w_in=w_in, gmlp_ln_g=gmlp_ln_g, gmlp_ln_b=gmlp_ln_b, gmlp_ws=gmlp_ws,
        gmlp_bs=gmlp_bs, nsa_q_norm_g=nsa_q_norm_g, nsa_k_norm_g=nsa_k_norm_g,
        nsa_cmp_pe=nsa_cmp_pe, nsa_cmp_w1=nsa_cmp_w1, nsa_cmp_w2=nsa_cmp_w2,
        conf_conv_w=conf_conv_w, conf_conv_b=conf_conv_b, conf_ln_g=conf_ln_g, conf_ln_b=conf_ln_b,
        sconv_w=sconv_w, w_branch=w_branch, w_gate=w_gate, b_gate=b_gate, w_out=w_out,
        norm2_g=norm2_g, w_mlp1=w_mlp1, w_mlp2=w_mlp2))
    for l in range(norm1_g.shape[0]):
        yacd, yb = _layer_mixers(x, l, p)
        x = _merge_mlp_call(x, l, yacd, yb, p["norm1_g"], p["w_branch"], p["w_gate"], p["b_gate"],
                            p["w_out"], p["norm2_g"], p["w_mlp1"], p["w_mlp2"])
    return x
```

```python
import functools

import jax
import jax.numpy as jnp
from jax import lax
from jax.experimental import pallas as pl
from jax.experimental.pallas import tpu as pltpu

D_MODEL = 1024
MIX_W = 256
HEAD_DIM = 64
N_KV_HEADS = 2
Q_PER_KV = 2
KV_W = N_KV_HEADS * HEAD_DIM
GMLP_GROUPS = 4
GMLP_CHUNK = 128
CMP_LEN = 32
CMP_STRIDE = 16
CMP_HIDDEN = 128
SEL_BLOCK = 64
SEL_TOPK = 8
N_LOCAL_BLOCKS = 2
WINDOW = 512
CONF_KERNEL = 31
SCONV_KERNEL = 3
D_FF = 4 * D_MODEL
NSA_GATE_W = 12
NEG_INF = -1e30

LANES = 128
SUBLANES = 8
MXU_DTYPE = jnp.bfloat16
VMEM_LIMIT = 56 * 1024 * 1024

_C_GU, _C_GV, _C_Q = 0, 256, 512
_C_KC, _C_VC, _C_KS, _C_VS, _C_KW, _C_VW = 768, 896, 1024, 1152, 1280, 1408
_C_NG = 1536
_C_CA, _C_CB, _C_SB, _C_SC, _C_SH = 1664, 1920, 2176, 2432, 2688
_NG_SRC = 1536

TM_PROJ = 1024
TM_MERGE = 512
T_ATT = 256
CH_ATT = 256
LOG2E = 1.4426950408889634
V_AUG_ROWS = 16
CMP_PAD = 8
SEL_SEGMENT = 1
CONV_HALO = 32
SCONV_HALO = 8


def _rms_rows(x, g, eps=1e-6):
    return x * lax.rsqrt(jnp.mean(x * x, axis=-1, keepdims=True) + eps) * g


def _layernorm_rows(x, g, b, eps=1e-5):
    mu = jnp.mean(x, axis=-1, keepdims=True)
    xc = x - mu
    var = jnp.mean(xc * xc, axis=-1, keepdims=True)
    return xc * lax.rsqrt(var + eps) * g + b


def _head_rms_rows(x, g, eps=1e-6):
    n_heads = x.shape[-1] // HEAD_DIM
    head = lax.broadcasted_iota(jnp.int32, x.shape, 1) >> 6
    xx = x * x
    scale = jnp.zeros_like(x)
    for h in range(n_heads):
        ms = jnp.sum(jnp.where(head == h, xx, 0.0), axis=-1, keepdims=True) * (1.0 / HEAD_DIM)
        scale = jnp.where(head == h, lax.rsqrt(ms + eps), scale)
    return x * scale * g


def _layer_spec(a, l):
    return pl.BlockSpec((1,) + a.shape[1:], lambda *_: (l,) + (0,) * (a.ndim - 1))


def _proj_kernel(x_ref, g1_ref, wa_ref, wng_ref, wb_ref, lng_ref, lnb_ref, ws_ref, bst_ref, qg_ref,
                 kg_ref, cw_ref, cbias_ref, clg_ref, clb_ref, sw_ref,
                 yacd_ref, qt_ref, kc_ref, vc_ref, ks_ref, vst_ref, kw_ref, vwt_ref, ngt_ref,
                 zbuf, cbuf, *, tm, ch):
    @pl.when(pl.program_id(1) == 0)
    def _():
        zbuf[...] = jnp.zeros((CONV_HALO, MIX_W), jnp.float32)
        cbuf[...] = jnp.zeros((SCONV_HALO, MIX_W), jnp.float32)

    x = x_ref[0]
    xn = _rms_rows(x, g1_ref[0]).astype(MXU_DTYPE)

    def proj(lo, width):
        if lo < _C_NG:
            w = wa_ref[lo:lo + width, :]
        elif lo == _C_NG:
            w = wng_ref[...]
        else:
            w = wb_ref[lo - _C_CA:lo - _C_CA + width, :]
        return lax.dot_general(xn, w, (((1,), (1,)), ((), ())), preferred_element_type=jnp.float32)

    def causal_conv(buf, w, width, halo):
        offsets = [halo - (width - 1) + j for j in range(width)]
        out = jnp.zeros((tm, MIX_W), jnp.float32)
        for b in range(SUBLANES):
            taps = [j for j in range(width) if offsets[j] % SUBLANES == b]
            if not taps:
                continue
            rows = tm if b == 0 else tm + SUBLANES
            part = jnp.zeros((rows, MIX_W), jnp.float32)
            for j in taps:
                part = part + w[j:j + 1] * buf[offsets[j] - b:offsets[j] - b + rows]
            out = out + part[b:b + tm]
            yield out


    z = proj(_C_CA, MIX_W) * jax.nn.sigmoid(proj(_C_CB, MIX_W))
    conv = causal_conv(jnp.concatenate([zbuf[...], z], axis=0), cw_ref[0], CONF_KERNEL, CONV_HALO)
    zbuf[...] = z[tm - CONV_HALO:tm]

    def emit_q():
        qt = proj(_C_Q, MIX_W).T
        gq = jnp.concatenate([qg_ref[0]] * (tm // LANES), axis=1)
        pieces = []
        for h in range(MIX_W // HEAD_DIM):
            blk = qt[h * HEAD_DIM:(h + 1) * HEAD_DIM]
            ms = jnp.sum(blk * blk, axis=0, keepdims=True) * (1.0 / HEAD_DIM)
            pieces.append(blk * lax.rsqrt(ms + 1e-6))
        qn = (jnp.concatenate(pieces, axis=0) * gq * (HEAD_DIM ** -0.5 * LOG2E)).astype(qt_ref.dtype)
        for c in range(tm // ch):
            qt_ref[0, c] = qn[:, c * ch:(c + 1) * ch]

    def emit_raw(dst_ref, col):
        dst_ref[0] = proj(col, KV_W)

    def emit_key(dst_ref, col, branch):
        dst_ref[0] = _head_rms_rows(proj(col, KV_W), kg_ref[0, branch:branch + 1, :]).astype(dst_ref.dtype)

    def emit_value_t(dst_ref, col):
        vt = proj(col, KV_W).T
        for c in range(tm // ch):
            dst_ref[0, c] = vt[:, c * ch:(c + 1) * ch].astype(dst_ref.dtype)

    def emit_gates():
        ngt = proj(_C_NG, LANES).T[0:16, :]
        for c in range(tm // ch):
            ngt_ref[0, c] = ngt[:, c * ch:(c + 1) * ch]

    held = {}

    def hold(name, col):
        held[name] = proj(col, MIX_W)

    slices = [
        [lambda: emit_raw(kc_ref, _C_KC), lambda: emit_raw(vc_ref, _C_VC)],
        [lambda: emit_key(ks_ref, _C_KS, 1), lambda: emit_key(kw_ref, _C_KW, 2)],
        [lambda: emit_value_t(vst_ref, _C_VS), lambda: emit_value_t(vwt_ref, _C_VW)],
        [emit_q],
        [emit_gates, lambda: hold("sb", _C_SB)],
        [lambda: hold("sc", _C_SC), lambda: hold("sh", _C_SH)],
        [lambda: hold("gu", _C_GU)],
        [lambda: hold("gv", _C_GV)],
    ]
    acc = None
    for tasks in slices:
        for task in tasks:
            task()
        acc = next(conv, acc)
    for acc in conv:
        pass
    yacd_ref[0, :, MIX_W:2 * MIX_W] = jax.nn.silu(_layernorm_rows(
        acc + cbias_ref[0], clg_ref[0], clb_ref[0])).astype(yacd_ref.dtype)

    sc = held["sc"] * held["sh"]
    for acc in causal_conv(jnp.concatenate([cbuf[...], sc], axis=0), sw_ref[0], SCONV_KERNEL, SCONV_HALO):
        pass
    cbuf[...] = sc[tm - SCONV_HALO:tm]
    yacd_ref[0, :, 2 * MIX_W:3 * MIX_W] = (held["sb"] * acc).astype(yacd_ref.dtype)

    u = jax.nn.gelu(held["gu"])
    v = jax.nn.gelu(held["gv"])
    v = _layernorm_rows(v, lng_ref[0], lnb_ref[0]).astype(MXU_DTYPE)
    tri = (lax.broadcasted_iota(jnp.int32, (GMLP_CHUNK, GMLP_CHUNK), 0)
           >= lax.broadcasted_iota(jnp.int32, (GMLP_CHUNK, GMLP_CHUNK), 1))
    grp = lax.broadcasted_iota(jnp.int32, (GMLP_CHUNK, MIX_W), 1) >> 6
    wsm = [jnp.where(tri, ws_ref[0, g], 0.0).astype(MXU_DTYPE) for g in range(GMLP_GROUPS)]
    for c in range(tm // GMLP_CHUNK):
        rows = slice(c * GMLP_CHUNK, (c + 1) * GMLP_CHUNK)
        vch = v[rows]
        mixed = jnp.zeros((GMLP_CHUNK, MIX_W), jnp.float32)
        for g in range(GMLP_GROUPS):
            r = jnp.dot(wsm[g], vch, preferred_element_type=jnp.float32)
            mixed = jnp.where(grp == g, r, mixed)
        yacd_ref[0, rows, 0:MIX_W] = (u[rows] * (mixed + bst_ref[0])).astype(yacd_ref.dtype)


def _proj_call(x, l, wa, wng, wb, g1, lng, lnb, ws, bst, qg, kg, cw, cbias, clg, clb, sw):
    B, S, D = x.shape
    tm, ch = min(TM_PROJ, S), CH_ATT
    full = lambda a: pl.BlockSpec(a.shape, lambda b, i: (0,) * a.ndim)
    seq = lambda width: pl.BlockSpec((1, tm, width), lambda b, i: (b, i, 0))
    stacked = (lng, lnb, ws, bst, qg, kg, cw, cbias, clg, clb, sw)
    ins = (g1, wa, wng, wb) + stacked
    in_specs = ([seq(D), _layer_spec(g1, l), full(wa), full(wng), full(wb)]
                + [_layer_spec(a, l) for a in stacked])
    out_shape = (
        jax.ShapeDtypeStruct((B, S, 3 * MIX_W), MXU_DTYPE),
        jax.ShapeDtypeStruct((B, S // ch, MIX_W, ch), MXU_DTYPE),
        jax.ShapeDtypeStruct((B, S, KV_W), jnp.float32),
        jax.ShapeDtypeStruct((B, S, KV_W), jnp.float32),
        jax.ShapeDtypeStruct((B, S, KV_W), MXU_DTYPE),
        jax.ShapeDtypeStruct((B, S // ch, KV_W, ch), MXU_DTYPE),
        jax.ShapeDtypeStruct((B, S, KV_W), MXU_DTYPE),
        jax.ShapeDtypeStruct((B, S // ch, KV_W, ch), MXU_DTYPE),
        jax.ShapeDtypeStruct((B, S // ch, 16, ch), jnp.float32),
    )
    vt_spec = pl.BlockSpec((1, tm // ch, KV_W, ch), lambda b, i: (b, i, 0, 0))
    out_specs = (
        seq(3 * MIX_W),
        pl.BlockSpec((1, tm // ch, MIX_W, ch), lambda b, i: (b, i, 0, 0)),
        seq(KV_W), seq(KV_W), seq(KV_W), vt_spec, seq(KV_W), vt_spec,
        pl.BlockSpec((1, tm // ch, 16, ch), lambda b, i: (b, i, 0, 0)),
    )
    return pl.pallas_call(
        functools.partial(_proj_kernel, tm=tm, ch=ch),
        grid=(B, S // tm),
        in_specs=in_specs,
        out_specs=out_specs,
        out_shape=out_shape,
        scratch_shapes=[pltpu.VMEM((CONV_HALO, MIX_W), jnp.float32),
                        pltpu.VMEM((SCONV_HALO, MIX_W), jnp.float32)],
        compiler_params=pltpu.CompilerParams(
            dimension_semantics=("arbitrary", "arbitrary"), vmem_limit_bytes=VMEM_LIMIT),
        name="proj_mixers",
    )(x, *ins)


def _compress_kernel(kc_ref, vc_ref, pet_ref, peb_ref, wt_ref, wb_ref, w2_ref, kg_ref,
                     kcmp_ref, vcmpt_ref):
    nc = kc_ref.shape[1] // CMP_STRIDE

    def compress(src_ref, idx):
        chunks = jnp.concatenate(
            [src_ref[0, pl.ds(l, nc, stride=CMP_STRIDE), :] for l in range(CMP_STRIDE)], axis=1)
        a1 = jnp.dot((chunks + pet_ref[0, idx]).astype(MXU_DTYPE), wt_ref[0, idx],
                     preferred_element_type=jnp.float32)
        a2 = jnp.dot((chunks + peb_ref[0, idx]).astype(MXU_DTYPE), wb_ref[0, idx],
                     preferred_element_type=jnp.float32)
        hid = jax.nn.gelu(a1 + pltpu.roll(a2, shift=nc - 1, axis=0))
        return jnp.dot(hid.astype(MXU_DTYPE), w2_ref[0, idx], preferred_element_type=jnp.float32)

    kcmp_ref[0] = _head_rms_rows(compress(kc_ref, 0), kg_ref[0, 0:1, :]).astype(kcmp_ref.dtype)
    vcmpt_ref[0] = compress(vc_ref, 1).T.astype(vcmpt_ref.dtype)


def _compress_call(kc, vc, l, pet, peb, wt, wb, w2, kg):
    B, S, _ = kc.shape
    nc = S // CMP_STRIDE
    per_b = lambda r, c: pl.BlockSpec((1, r, c), lambda b: (b, 0, 0))
    ins = (pet, peb, wt, wb, w2, kg)
    return pl.pallas_call(
        _compress_kernel,
        grid=(B,),
        in_specs=[per_b(S, KV_W)] * 2 + [_layer_spec(a, l) for a in ins],
        out_specs=(per_b(nc, KV_W), per_b(KV_W, nc)),
        out_shape=(jax.ShapeDtypeStruct((B, nc, KV_W), MXU_DTYPE),
                   jax.ShapeDtypeStruct((B, KV_W, nc), MXU_DTYPE)),
        compiler_params=pltpu.CompilerParams(
            dimension_semantics=("arbitrary",), vmem_limit_bytes=VMEM_LIMIT),
        name="nsa_compress",
    )(kc, vc, *ins)


def _attn_kernel(*refs, t, seq_len):
    qt_ref, ngt_ref = refs[:2]
    compress_refs, rest = refs[2:10], refs[10:]
    kcmp_scr, vcmpt_scr = rest[-2:]
    _compress_kernel(*compress_refs, kcmp_scr, vcmpt_scr)
    for i in range(seq_len // t):
        _attn_tile(i, qt_ref, ngt_ref, kcmp_scr, vcmpt_scr, *rest[:-2], t=t, seq_len=seq_len)


def _attn_tile(i, qt_ref, ngt_ref, kcmp_ref, vcmpt_ref, ks_ref, vst_ref, kw_ref, vwt_ref,
               y_ref, selb_scr, pcmp_scr, *, t, seq_len):
    nc = seq_len // CMP_STRIDE
    n_cmp = nc - 1
    n_sel = seq_len // SEL_BLOCK
    k_top = min(SEL_TOPK, n_sel)
    bpc = t // SEL_BLOCK
    cmp_per_sel = SEL_BLOCK // CMP_STRIDE
    nw = WINDOW // t
    t2 = 2 * t
    hq = t // 2
    f32 = jnp.float32
    n_cmp_live = min(nc, (i + 1) * t // CMP_STRIDE)
    n_sel_live = (i + 1) * bpc
    n_groups = -(-n_sel_live // SUBLANES)

    def lanes_of(a0, a1):
        return jnp.concatenate([a0[:, :hq], a1[:, :hq], a0[:, hq:], a1[:, hq:]], axis=1)

    def both(a):
        return lanes_of(a, a)

    def keep_bias(cond):
        return jnp.where(cond, 0.0, NEG_INF).astype(f32)

    qt = qt_ref[0, i]
    qpads = []
    for h in range(N_KV_HEADS):
        qh = lanes_of(qt[(2 * h) * HEAD_DIM:(2 * h + 1) * HEAD_DIM],
                      qt[(2 * h + 1) * HEAD_DIM:(2 * h + 2) * HEAD_DIM])
        zero = jnp.zeros_like(qh)
        qpads.append(jnp.concatenate([qh, zero] if h == 0 else [zero, qh], axis=0))

    n_i = lax.broadcasted_iota(jnp.int32, (n_cmp_live, t), 0)
    qpos_c = i * t + lax.broadcasted_iota(jnp.int32, (n_cmp_live, t), 1)
    cmp_ok = (n_i * CMP_STRIDE + (CMP_LEN - 1) <= qpos_c) & (n_i < n_cmp)
    cmp_bias = both(keep_bias(cmp_ok))
    cmp_keep = both(jnp.where(cmp_ok, 1.0, 0.0).astype(f32))

    blk = lax.broadcasted_iota(jnp.int32, (n_groups * SUBLANES, t), 0)
    cur = (i * t + lax.broadcasted_iota(jnp.int32, (n_groups * SUBLANES, t), 1)) >> 6
    dist = cur - blk
    causal = dist >= 0
    forced = (blk == 0) | (causal & (dist < N_LOCAL_BLOCKS))
    blk8 = lax.broadcasted_iota(jnp.int32, (8, t), 0)

    kcmp = kcmp_ref[0, 0:n_cmp_live, :]
    pcmp_scr[:, 0:CMP_PAD, :] = jnp.zeros((t // LANES, CMP_PAD, LANES), f32)
    o_cmp = [None] * N_KV_HEADS

    def compressed_and_select(h):
        sm = jnp.dot(kcmp, qpads[h], preferred_element_type=f32) + cmp_bias
        e = jnp.exp2(sm - jnp.max(sm, axis=0, keepdims=True)) * cmp_keep
        den = jnp.sum(e, axis=0, keepdims=True)
        p_cmp = e * (1.0 / jnp.where(den > 0.0, den, 1.0))
        p_all = p_cmp.astype(MXU_DTYPE)
        if n_cmp_live < nc:
            p_all = jnp.concatenate([p_all, jnp.zeros((nc - n_cmp_live, t2), MXU_DTYPE)], axis=0)
        o_cmp[h] = jnp.dot(vcmpt_ref[0, h * HEAD_DIM:(h + 1) * HEAD_DIM, :], p_all,
                           preferred_element_type=f32)

        p_pair = jnp.concatenate([p_cmp[:, 0:hq] + p_cmp[:, hq:t],
                                  p_cmp[:, t:t + hq] + p_cmp[:, t + hq:t2]], axis=1)
        slabs = []
        for lb in range(t // LANES):
            pcmp_scr[lb, CMP_PAD:CMP_PAD + n_cmp_live, :] = p_pair[:, lb * LANES:(lb + 1) * LANES]
            slabs.append(functools.reduce(lambda a, b: a + b, [
                pcmp_scr[lb, pl.ds(CMP_PAD + off, n_sel_live, stride=cmp_per_sel), :]
                for off in range(1 - CMP_LEN // CMP_STRIDE, cmp_per_sel)]))
        p_slc = jnp.concatenate(slabs, axis=1)
        if n_sel_live < n_groups * SUBLANES:
            p_slc = jnp.concatenate(
                [p_slc, jnp.zeros((n_groups * SUBLANES - n_sel_live, t), f32)], axis=0)
        score = jnp.where(forced, jnp.inf, jnp.where(causal, p_slc, -jnp.inf))
        groups = [score[8 * g:8 * g + 8] for g in range(n_groups)]
        rank = [jnp.zeros((8, t), f32) for _ in groups]
        for r in range(n_sel_live):
            row = score[r:r + 1, :]
            for g, sg in enumerate(groups):
                if 8 * g > r:
                    beats = row >= sg
                elif 8 * g + 7 <= r:
                    beats = row > sg
                else:
                    beats = (row > sg) | ((row == sg) & (blk8 > r - 8 * g))
                rank[g] = rank[g] + jnp.where(beats, 1.0, 0.0)
        selected = (jnp.concatenate(rank, axis=0) < k_top) & causal
        selb_scr[h, 0:n_groups * SUBLANES, :] = both(keep_bias(selected))

    r_i = lax.broadcasted_iota(jnp.int32, (t, t), 0)
    q_i = lax.broadcasted_iota(jnp.int32, (t, t), 1)
    diag_bias = both(keep_bias(r_i <= q_i))
    ones_rows = jnp.where(lax.broadcasted_iota(jnp.int32, (V_AUG_ROWS, t), 0) == 0,
                          1.0, 0.0).astype(MXU_DTYPE)

    def k_chunk(k_ref, c):
        start = c * t if isinstance(c, int) else pl.multiple_of(c * t, t)
        return k_ref[0, pl.ds(start, t), :]

    def v_aug(vt_ref, c, h):
        return jnp.concatenate([vt_ref[0, c, h * HEAD_DIM:(h + 1) * HEAD_DIM, :], ones_rows], axis=0)

    def colmax8(s):
        return jnp.max(s.reshape(s.shape[0] // 8, 8, s.shape[1]), axis=0)

    def weighted_values(s, m, vaug):
        p = jnp.exp2(s - m).astype(MXU_DTYPE)
        return jnp.dot(vaug, p, preferred_element_type=f32)

    def normalise(acc):
        return acc[0:HEAD_DIM] * (1.0 / acc[HEAD_DIM:HEAD_DIM + 1])

    def on_lanes(full, part, lanes, op):
        pieces = [full[:, :lanes.start]] if lanes.start else []
        pieces.append(op(full[:, lanes], part))
        if lanes.stop < full.shape[1]:
            pieces.append(full[:, lanes.stop:])
        return pieces[0] if len(pieces) == 1 else jnp.concatenate(pieces, axis=1)

    top, bot = slice(0, hq), slice(hq, t)
    lo_q, hi_q, all_q = slice(0, t), slice(t, t2), slice(0, t2)
    full_pieces = ((slice(0, t), all_q),)
    diag_pieces = ((top, all_q), (bot, hi_q))
    first_pieces = ((top, lo_q), (bot, all_q))

    heads = range(N_KV_HEADS)
    neg8 = jnp.full((8, t2), NEG_INF, f32)

    def sel_scores(c, k, h, rows, lanes, extra):
        s = jnp.dot(k[rows], qpads[h][:, lanes], preferred_element_type=f32)
        parts = []
        for r in range((rows.stop - rows.start) // SEL_BLOCK):
            blk_rows = slice(r * SEL_BLOCK, (r + 1) * SEL_BLOCK)
            bias = selb_scr[h, pl.ds(c * bpc + rows.start // SEL_BLOCK + r, 1), lanes]
            if extra is not None:
                bias = bias + (extra[blk_rows] if extra.shape[0] > 1 else extra)
            parts.append(s[blk_rows] + bias)
        return jnp.concatenate(parts, axis=0)

    def win_scores(c, k, h, rows, lanes, extra):
        s = jnp.dot(k[rows], qpads[h][:, lanes], preferred_element_type=f32)
        return s if extra is None else s + extra

    def softmax_branch(segments, k_ref, vt_ref, score_fn):
        keys = {c: k_chunk(k_ref, c) for c in sorted({w[0] for seg in segments for w in seg})}
        m_run = [None for _ in heads]
        accs = [jnp.zeros((HEAD_DIM + V_AUG_ROWS, t2), f32) for _ in heads]
        phases = []
        for seg in segments:
            ss = [[] for _ in heads]
            pm = [neg8 for _ in heads]

            def pass1(item, ss=ss, pm=pm):
                c, rows, lanes, extra = item
                for h in heads:
                    s = score_fn(c, keys[c], h, rows, lanes, extra)
                    ss[h].append(s)
                    pm[h] = on_lanes(pm[h], colmax8(s), lanes, jnp.maximum)

            def fix(pm=pm):
                for h in heads:
                    m_new = jnp.max(pm[h], axis=0, keepdims=True)
                    if m_run[h] is not None:
                        m_new = jnp.maximum(m_run[h], m_new)
                        accs[h] = accs[h] * jnp.exp2(m_run[h] - m_new)
                    m_run[h] = m_new

            def pass2(n, seg=seg, ss=ss):
                c, rows, lanes, _ = seg[n]
                for h in heads:
                    part = weighted_values(ss[h][n], m_run[h][:, lanes], v_aug(vt_ref, c, h)[:, rows])
                    accs[h] = on_lanes(accs[h], part, lanes, lambda a, b: a + b)

            phases.append(([functools.partial(pass1, item) for item in seg], fix,
                           [functools.partial(pass2, n) for n in range(len(seg))]))
        return phases, lambda: [normalise(accs[h]) for h in heads]

    sel_items = [[(c, rows, lanes, None) for rows, lanes in full_pieces] for c in range(i)]
    sel_items.append([(i, rows, lanes, diag_bias[rows, lanes]) for rows, lanes in diag_pieces])
    sel_segments = [sum(sel_items[n:n + SEL_SEGMENT], []) for n in range(0, i + 1, SEL_SEGMENT)]

    first_bias = both(keep_bias(r_i > q_i))
    win_segments = []
    for cc in range(nw, -1, -1):
        ci = i - nw + cc
        if ci < 0:
            continue
        if cc == nw:
            seg = [(ci, rows, lanes, diag_bias[rows, lanes]) for rows, lanes in diag_pieces]
        elif cc == 0:
            seg = [(ci, rows, lanes, first_bias[rows, lanes]) for rows, lanes in first_pieces]
        else:
            seg = [(ci, rows, lanes, None) for rows, lanes in full_pieces]
        win_segments.append(seg)

    sel_phases, sel_out = softmax_branch(sel_segments, ks_ref, vst_ref, sel_scores)
    win_phases, win_out = softmax_branch(win_segments, kw_ref, vwt_ref, win_scores)

    def alternate(a, b):
        for n in range(max(len(a), len(b))):
            for steps in (a, b):
                if n < len(steps):
                    steps[n]()

    for h in heads:
        compressed_and_select(h)
    pending = []
    for steps1, fix, steps2 in win_phases + sel_phases:
        alternate(steps1, pending)
        fix()
        pending = steps2
    for step in pending:
        step()
    o_sel, o_win = sel_out(), win_out()

    sig = jax.nn.sigmoid(ngt_ref[0, i])
    out_rows = []
    for h in heads:
        def gate(r):
            a = (2 * h) * 3 + r
            b = (2 * h + 1) * 3 + r
            return lanes_of(sig[a:a + 1, :], sig[b:b + 1, :])

        o = gate(0) * o_cmp[h] + gate(1) * o_sel[h] + gate(2) * o_win[h]
        out_rows += [jnp.concatenate([o[:, 0:hq], o[:, t:t + hq]], axis=1),
                     jnp.concatenate([o[:, hq:t], o[:, t + hq:t2]], axis=1)]
    y_ref[0, i * t:(i + 1) * t, :] = jnp.concatenate(out_rows, axis=0).T.astype(y_ref.dtype)


def _attn_call(qt, ngt, kc, vc, l, cmp_w, kg, ks, vst, kw, vwt):
    B, n_tiles, _, t = qt.shape
    S = n_tiles * t
    assert t == T_ATT == CH_ATT and WINDOW % t == 0 and (S // SEL_BLOCK) % 8 == 0
    assert SEL_BLOCK % CMP_STRIDE == 0 and CMP_LEN % CMP_STRIDE == 0 and CMP_LEN // CMP_STRIDE <= CMP_PAD
    nc = S // CMP_STRIDE
    n_sel = S // SEL_BLOCK
    per_b = lambda a: pl.BlockSpec((1,) + a.shape[1:], lambda b: (b,) + (0,) * (a.ndim - 1))
    stacked = tuple(cmp_w) + (kg,)
    ins = (qt, ngt, kc, vc) + stacked + (ks, vst, kw, vwt)
    return pl.pallas_call(
        functools.partial(_attn_kernel, t=t, seq_len=S),
        grid=(B,),
        in_specs=([per_b(a) for a in (qt, ngt, kc, vc)] + [_layer_spec(a, l) for a in stacked]
                  + [per_b(a) for a in (ks, vst, kw, vwt)]),
        out_specs=pl.BlockSpec((1, S, MIX_W), lambda b: (b, 0, 0)),
        out_shape=jax.ShapeDtypeStruct((B, S, MIX_W), MXU_DTYPE),
        scratch_shapes=[
            pltpu.VMEM((N_KV_HEADS, n_sel, 2 * t), jnp.float32),
            pltpu.VMEM((t // LANES, CMP_PAD + nc, LANES), jnp.float32),
            pltpu.VMEM((1, nc, KV_W), MXU_DTYPE),
            pltpu.VMEM((1, KV_W, nc), MXU_DTYPE),
        ],
        compiler_params=pltpu.CompilerParams(
            dimension_semantics=("arbitrary",), vmem_limit_bytes=VMEM_LIMIT),
        name="nsa_attention",
    )(*ins)


def _merge_mlp_kernel(x_ref, yacd_ref, yb_ref, g1_ref, wbr_ref, wg_ref, bg_ref, wo_ref,
                      g2_ref, w1_ref, w2_ref, o_ref):
    x = x_ref[0]
    xn = _rms_rows(x, g1_ref[0]).astype(MXU_DTYPE)
    ys = (yacd_ref[0, :, 0:MIX_W], yb_ref[0], yacd_ref[0, :, MIX_W:2 * MIX_W],
          yacd_ref[0, :, 2 * MIX_W:3 * MIX_W])
    mixed = jnp.zeros(x.shape, jnp.float32)
    for n, y in enumerate(ys):
        cols = slice(n * D_MODEL, (n + 1) * D_MODEL)
        proj = jnp.dot(y, wbr_ref[0, n], preferred_element_type=jnp.float32)
        logits = jnp.dot(xn, wg_ref[0, :, cols], preferred_element_type=jnp.float32) + bg_ref[0, :, cols]
        mixed = mixed + jax.nn.sigmoid(logits) * proj
    h = x + jnp.dot(mixed.astype(MXU_DTYPE), wo_ref[0], preferred_element_type=jnp.float32)

    hn = _rms_rows(h, g2_ref[0]).astype(MXU_DTYPE)
    acc = h
    for c in range(D_FF // D_MODEL):
        cols = slice(c * D_MODEL, (c + 1) * D_MODEL)
        hid = jnp.dot(hn, w1_ref[0, :, cols], preferred_element_type=jnp.float32)
        hid = jnp.square(jnp.maximum(hid, 0.0)).astype(MXU_DTYPE)
        acc = acc + jnp.dot(hid, w2_ref[0, cols, :], preferred_element_type=jnp.float32)
    o_ref[0] = acc


def _merge_mlp_call(x, l, yacd, yb, g1, wbr, wg, bg, wo, g2, w1, w2):
    B, S, d = x.shape
    tm = min(TM_MERGE, S)
    rows = lambda w: pl.BlockSpec((1, tm, w), lambda b, i: (b, i, 0))
    stacked = (g1, wbr, wg, bg, wo, g2, w1, w2)
    return pl.pallas_call(
        _merge_mlp_kernel,
        grid=(B, S // tm),
        in_specs=[rows(d), rows(3 * MIX_W), rows(MIX_W)] + [_layer_spec(a, l) for a in stacked],
        out_specs=rows(d),
        out_shape=jax.ShapeDtypeStruct((B, S, d), jnp.float32),
        compiler_params=pltpu.CompilerParams(
            dimension_semantics=("arbitrary", "arbitrary"), vmem_limit_bytes=VMEM_LIMIT),
        name="merge_mlp",
    )(x, yacd, yb, *stacked)


def _split_w_in_t(w_in, l):
    wt = jnp.swapaxes(w_in, 1, 2)[l]
    wng = jnp.pad(wt[_NG_SRC:_NG_SRC + NSA_GATE_W], ((0, LANES - NSA_GATE_W), (0, 0)))
    return (wt[:_NG_SRC].astype(MXU_DTYPE), wng.astype(MXU_DTYPE),
            wt[_NG_SRC + NSA_GATE_W:].astype(MXU_DTYPE))


def _compress_weights(pe, w1, w2):
    assert N_KV_HEADS == 2
    L = pe.shape[0]
    w1r = w1.reshape(L, 2, CMP_LEN, HEAD_DIM, CMP_HIDDEN)

    def block_diag(w):
        z = jnp.zeros_like(w)
        return jnp.concatenate([jnp.concatenate([w, z], axis=-1),
                                jnp.concatenate([z, w], axis=-1)], axis=-2)

    def expand(w):
        return block_diag(w).reshape(
            L, 2, CMP_STRIDE * KV_W, N_KV_HEADS * CMP_HIDDEN).astype(MXU_DTYPE)

    def expand_pe(p):
        return jnp.concatenate([p, p], axis=-1).reshape(L, 2, 1, CMP_STRIDE * KV_W)

    w2b = block_diag(w2).astype(MXU_DTYPE)
    return (expand_pe(pe[:, :, :CMP_STRIDE]), expand_pe(pe[:, :, CMP_STRIDE:]),
            expand(w1r[:, :, :CMP_STRIDE]), expand(w1r[:, :, CMP_STRIDE:]), w2b)


def _layer_mixers(x, l, p):
    (yacd, qt, kc, vc, ks, vst, kw, vwt, ngt) = _proj_call(
        x, l, *_split_w_in_t(p["w_in"], l), p["norm1_g"], p["gmlp_ln_g"], p["gmlp_ln_b"], p["gmlp_ws"],
        p["bst"], p["qg"], p["kg"], p["conf_conv_w"], p["conf_conv_b"], p["conf_ln_g"],
        p["conf_ln_b"], p["sconv_w"])
    yb = _attn_call(qt, ngt, kc, vc, l, p["cmp_w"], p["kg"], ks, vst, kw, vwt)
    return yacd, yb


def _prepare(p):
    q = dict(p)
    q["bst"] = jnp.repeat(jnp.swapaxes(p["gmlp_bs"], 1, 2), MIX_W // GMLP_GROUPS, axis=2)
    qg = jnp.tile(p["nsa_q_norm_g"], (1, MIX_W // HEAD_DIM))
    q["qg"] = jnp.broadcast_to(qg[:, :, None], qg.shape + (LANES,))
    q["kg"] = jnp.tile(p["nsa_k_norm_g"], (1, 1, N_KV_HEADS))
    q["cmp_w"] = _compress_weights(p["nsa_cmp_pe"], p["nsa_cmp_w1"], p["nsa_cmp_w2"])
    for name in ("w_branch", "w_gate", "w_out", "w_mlp1", "w_mlp2"):
        q[name] = p[name].astype(MXU_DTYPE)
    for name in ("norm1_g", "gmlp_ln_g", "gmlp_ln_b", "conf_conv_b", "conf_ln_g", "conf_ln_b",
                 "b_gate", "norm2_g"):
        q[name] = p[name][:, None, :]
    return q


def kernel(x, norm1_g, w_in, gmlp_ln_g, gmlp_ln_b, gmlp_ws, gmlp_bs, nsa_q_norm_g, nsa_k_norm_g,
           nsa_cmp_pe, nsa_cmp_w1, nsa_cmp_w2, conf_conv_w, conf_conv_b, conf_ln_g, conf_ln_b,
           sconv_w, w_branch, w_gate, b_gate, w_out, norm2_g, w_mlp1, w_mlp2):
    B, S, D = x.shape
    assert D == D_MODEL and S % min(TM_PROJ, S) == 0 and S % min(T_ATT, S) == 0 and S % CH_ATT == 0
    p = _prepare(dict(
        norm1_g=norm1_g, w_in=w_in, gmlp_ln_g=gmlp_ln_g, gmlp_ln_b=gmlp_ln_b, gmlp_ws=gmlp_ws,
        gmlp_bs=gmlp_bs, nsa_q_norm_g=nsa_q_norm_g, nsa_k_norm_g=nsa_k_norm_g,
        nsa_cmp_pe=nsa_cmp_pe, nsa_cmp_w1=nsa_cmp_w1, nsa_cmp_w2=nsa_cmp_w2,
        conf_conv_w=conf_conv_w, conf_conv_b=conf_conv_b, conf_ln_g=conf_ln_g, conf_ln_b=conf_ln_b,
        sconv_w=sconv_w, w_branch=w_branch, w_gate=w_gate, b_gate=b_gate, w_out=w_out,
        norm2_g=norm2_g, w_mlp1=w_mlp1, w_mlp2=w_mlp2))
    for l in range(norm1_g.shape[0]):
        yacd, yb = _layer_mixers(x, l, p)
        x = _merge_mlp_call(x, l, yacd, yb, p["norm1_g"], p["w_branch"], p["w_gate"], p["b_gate"],
                            p["w_out"], p["norm2_g"], p["w_mlp1"], p["w_mlp2"])
    return x
```
